```python
import math
import jax, jax.numpy as jnp
from jax import lax
import numpy as np

D_MODEL = 1024
BATCH = 8
SEQ = 2048
DEPTH = 1

MIX_WIDTH = D_MODEL
ATTN_WIDTH = MIX_WIDTH // 2
CONV_WIDTH = MIX_WIDTH - ATTN_WIDTH
DIFF_HEAD_DIM = 64
DIFF_V_DIM = 2 * DIFF_HEAD_DIM
N_DIFF_HEADS = ATTN_WIDTH // DIFF_V_DIM
Q_COLS = N_DIFF_HEADS * 2 * DIFF_HEAD_DIM
K_COLS = N_DIFF_HEADS * 2 * DIFF_HEAD_DIM
V_COLS = N_DIFF_HEADS * DIFF_V_DIM
CONV_IN_COLS = 2 * CONV_WIDTH
IN_COLS = Q_COLS + K_COLS + V_COLS + CONV_IN_COLS
Q_BLOCK = 128
CONV_KERNEL = 31
N_EXPERTS = 32
TOP_K = 4
D_EXPERT = D_MODEL
SWIGLU_LIMIT = 7.0
SWIGLU_ALPHA = 1.702
LN_EPS = 1e-5
DEEPNORM_ALPHA = (2.0 * DEPTH) ** 0.25
DEEPNORM_BETA = (8.0 * DEPTH) ** -0.25

kernel_name = "hymba_diffattn_conformer_moe_deepnorm"


def layer_norm(x, g, b):
    xf = x.astype(jnp.float32)
    mu = jnp.mean(xf, axis=-1, keepdims=True)
    xc = xf - mu
    var = jnp.mean(xc * xc, axis=-1, keepdims=True)
    y = xc * lax.rsqrt(var + LN_EPS) * g.astype(jnp.float32) + b.astype(jnp.float32)
    return y.astype(x.dtype)


def rms_norm(x, g):
    xf = x.astype(jnp.float32)
    y = xf * lax.rsqrt(jnp.mean(xf * xf, axis=-1, keepdims=True) + LN_EPS)
    return (y * g.astype(jnp.float32)).astype(x.dtype)


def alibi_slopes(n_heads):
    start = 2.0 ** (-8.0 / n_heads)
    return jnp.array([start ** (i + 1) for i in range(n_heads)], dtype=jnp.float32)


def diff_attention(q, k, v, lam, slopes):
    B, S, H, _, dh = q.shape
    dv = v.shape[-1]
    nb = S // Q_BLOCK
    q = q * (dh ** -0.5)
    qb = q.reshape(B, nb, Q_BLOCK, H, 2, dh).transpose(1, 0, 3, 4, 2, 5)
    kt = k.transpose(0, 2, 3, 1, 4)
    vt = v.transpose(0, 2, 1, 3)
    key_pos = jnp.arange(S, dtype=jnp.int32)

    def block(args):
        q_blk, start = args
        s = jnp.einsum('bhcqd,bhckd->bhcqk', q_blk, kt).astype(jnp.float32)
        dist = (start + jnp.arange(Q_BLOCK, dtype=jnp.int32))[:, None] - key_pos[None, :]
        bias = -slopes[:, None, None] * dist.astype(jnp.float32)
        s = jnp.where((dist >= 0)[None, None, None], s + bias[None, :, None], -jnp.inf)
        p = jax.nn.softmax(s, axis=-1)
        w = p[:, :, 0] - lam * p[:, :, 1]
        return jnp.einsum('bhqk,bhkd->bhqd', w.astype(vt.dtype), vt)

    starts = jnp.arange(nb, dtype=jnp.int32) * Q_BLOCK
    out = lax.map(block, (qb, starts))
    return out.transpose(1, 0, 3, 2, 4).reshape(B, S, H, dv)


def conformer_conv(u, conv_w, conv_b, ln_g, ln_b):
    a, g = jnp.split(u, 2, axis=-1)
    h = a * jax.nn.sigmoid(g)
    C = h.shape[-1]
    h = lax.conv_general_dilated(
        h, conv_w[:, None, :].astype(h.dtype), window_strides=(1,),
        padding=[(CONV_KERNEL - 1, 0)], dimension_numbers=('NWC', 'WIO', 'NWC'),
        feature_group_count=C) + conv_b
    h = layer_norm(h, ln_g, ln_b)
    return jax.nn.silu(h)


def clamped_swiglu(h):
    gate, up = jnp.split(h, 2, axis=-1)
    gate = jnp.minimum(gate, SWIGLU_LIMIT)
    up = jnp.clip(up, -SWIGLU_LIMIT, SWIGLU_LIMIT)
    return gate * jax.nn.sigmoid(SWIGLU_ALPHA * gate) * (up + 1.0)


def moe(x, router_w, router_b, w_gate_up, b_gate_up, w_down, b_down):
    B, S, D = x.shape
    xf = x.reshape(B * S, D)
    logits = (xf @ router_w + router_b).astype(jnp.float32)
    top_v, top_i = lax.top_k(logits, TOP_K)
    gates = jax.nn.softmax(top_v, axis=-1)
    combine = jnp.einsum('nk,nke->ne', gates, jax.nn.one_hot(top_i, N_EXPERTS, dtype=jnp.float32))
    combine = combine.astype(x.dtype)
    out = jnp.zeros_like(xf)
    for e in range(N_EXPERTS):
        h = clamped_swiglu(xf @ w_gate_up[e] + b_gate_up[e])
        out = out + combine[:, e:e + 1] * (h @ w_down[e] + b_down[e])
    return out.reshape(B, S, D)


def setup_inputs(seed: int = 0) -> dict:
    key = jax.random.key(seed)
    ks = jax.random.split(key, 24)
    L, D, E, F = DEPTH, D_MODEL, N_EXPERTS, D_EXPERT
    nrm = lambda k, shape: jax.random.normal(k, shape, dtype=jnp.float32)
    col_scale = jnp.concatenate([
        jnp.ones((Q_COLS + K_COLS,), jnp.float32),
        jnp.full((V_COLS,), DEEPNORM_BETA, jnp.float32),
        jnp.ones((CONV_IN_COLS,), jnp.float32)])
    return {
        "x": nrm(ks[0], (BATCH, SEQ, D)),
        "emb_ln_g": 1.0 + 0.02 * nrm(ks[1], (D,)),
        "emb_ln_b": 0.02 * nrm(ks[2], (D,)),
        "w_in": nrm(ks[3], (L, D, IN_COLS)) * (D ** -0.5) * col_scale,
        "lambda_q1": 0.1 * nrm(ks[4], (L, DIFF_HEAD_DIM)),
        "lambda_k1": 0.1 * nrm(ks[5], (L, DIFF_HEAD_DIM)),
        "lambda_q2": 0.1 * nrm(ks[6], (L, DIFF_HEAD_DIM)),
        "lambda_k2": 0.1 * nrm(ks[7], (L, DIFF_HEAD_DIM)),
        "subln_g": 1.0 + 0.02 * nrm(ks[8], (L, DIFF_V_DIM)),
        "conv_w": nrm(ks[9], (L, CONV_KERNEL, CONV_WIDTH)) * (CONV_KERNEL ** -0.5),
        "conv_b": 0.02 * nrm(ks[10], (L, CONV_WIDTH)),
        "conv_ln_g": 1.0 + 0.02 * nrm(ks[11], (L, CONV_WIDTH)),
        "conv_ln_b": 0.02 * nrm(ks[12], (L, CONV_WIDTH)),
        "w_out": nrm(ks[13], (L, MIX_WIDTH, D)) * (MIX_WIDTH ** -0.5) * DEEPNORM_BETA,
        "ln1_g": 1.0 + 0.02 * nrm(ks[14], (L, D)),
        "ln1_b": 0.02 * nrm(ks[15], (L, D)),
        "router_w": nrm(ks[16], (L, D, E)) * (D ** -0.5),
        "router_b": 0.01 * nrm(ks[17], (L, E)),
        "w_gate_up": nrm(ks[18], (L, E, D, 2 * F)) * (D ** -0.5) * DEEPNORM_BETA,
        "b_gate_up": 0.02 * nrm(ks[19], (L, E, 2 * F)),
        "w_down": nrm(ks[20], (L, E, F, D)) * (F ** -0.5) * DEEPNORM_BETA,
        "b_down": 0.02 * nrm(ks[21], (L, E, D)),
        "ln2_g": 1.0 + 0.02 * nrm(ks[22], (L, D)),
        "ln2_b": 0.02 * nrm(ks[23], (L, D)),
    }


def reference(x, emb_ln_g, emb_ln_b, w_in, lambda_q1, lambda_k1, lambda_q2, lambda_k2,
              subln_g, conv_w, conv_b, conv_ln_g, conv_ln_b, w_out, ln1_g, ln1_b,
              router_w, router_b, w_gate_up, b_gate_up, w_down, b_down, ln2_g, ln2_b):
    B, S, D = x.shape
    slopes = alibi_slopes(N_DIFF_HEADS)
    x = layer_norm(x, emb_ln_g, emb_ln_b)
    for i in range(DEPTH):
        lambda_init = 0.8 - 0.6 * math.exp(-0.3 * i)
        h = x @ w_in[i]
        q = h[..., :Q_COLS].reshape(B, S, N_DIFF_HEADS, 2, DIFF_HEAD_DIM)
        k = h[..., Q_COLS:Q_COLS + K_COLS].reshape(B, S, N_DIFF_HEADS, 2, DIFF_HEAD_DIM)
        v = h[..., Q_COLS + K_COLS:Q_COLS + K_COLS + V_COLS].reshape(B, S, N_DIFF_HEADS, DIFF_V_DIM)
        u = h[..., Q_COLS + K_COLS + V_COLS:]
        lam = (jnp.exp(jnp.sum(lambda_q1[i].astype(jnp.float32) * lambda_k1[i].astype(jnp.float32)))
               - jnp.exp(jnp.sum(lambda_q2[i].astype(jnp.float32) * lambda_k2[i].astype(jnp.float32)))
               + lambda_init)
        a = diff_attention(q, k, v, lam, slopes)
        a = (rms_norm(a, subln_g[i]) * (1.0 - lambda_init)).reshape(B, S, ATTN_WIDTH)
        c = conformer_conv(u, conv_w[i], conv_b[i], conv_ln_g[i], conv_ln_b[i])
        mix = jnp.concatenate([a, c], axis=-1) @ w_out[i]
        x = layer_norm(DEEPNORM_ALPHA * x + mix, ln1_g[i], ln1_b[i])
        y = moe(x, router_w[i], router_b[i], w_gate_up[i], b_gate_up[i], w_down[i], b_down[i])
        x = layer_norm(DEEPNORM_ALPHA * x + y, ln2_g[i], ln2_b[i])
    return x
```

```python
import functools

import jax
import jax.numpy as jnp
from jax import lax
from jax.experimental import pallas as pl
from jax.experimental.pallas import tpu as pltpu

F32 = jnp.float32
BF16 = jnp.bfloat16
I32 = jnp.int32

N_HEADS = 4
HEAD_DIM = 64
V_DIM = 128
ATTN_WIDTH = N_HEADS * V_DIM
CONV_WIDTH = 512
CONV_KERNEL = 31
N_EXPERTS = 32
TOP_K = 4
SWIGLU_LIMIT = 7.0
SWIGLU_ALPHA = 1.702
LN_EPS = 1e-5
DEPTH = 1
DEEPNORM_ALPHA = (2.0 * DEPTH) ** 0.25
LAMBDA_INIT = 0.2

LANES = 128
BF16_ROWS = 16

TOKEN_TILE = 512
ATTN_BLOCK = 256
CONV_TILE = 64
MOE_TILE = 512
CHUNKS_PER_TILE = MOE_TILE // BF16_ROWS
LOCAL_ROWS = TOP_K * TOKEN_TILE + N_EXPERTS * BF16_ROWS
CHUNKS_PER_BLOCK = LOCAL_ROWS // BF16_ROWS


def _cparams(semantics, vmem_mib):
    return pltpu.CompilerParams(dimension_semantics=semantics, vmem_limit_bytes=vmem_mib * 1024 * 1024)


def _layer_norm(x, g, b):
    mu = jnp.mean(x, axis=-1, keepdims=True)
    xc = x - mu
    var = jnp.mean(xc * xc, axis=-1, keepdims=True)
    return xc * lax.rsqrt(var + LN_EPS) * g + b


def _sigmoid(x):
    return 1.0 / (1.0 + jnp.exp(-x))


def _inproj_body(x_ref, g_ref, b_ref, w_ref, h_ref, *, n_chunk):
    xn = _layer_norm(x_ref[...], g_ref[...], b_ref[...]).astype(BF16)
    for j in range(w_ref.shape[1] // n_chunk):
        sl = slice(j * n_chunk, (j + 1) * n_chunk)
        h_ref[:, sl] = jnp.dot(xn, w_ref[:, sl], preferred_element_type=F32).astype(BF16)


def _inproj(x2, g, b, w_bf16):
    n, d = x2.shape
    cols = w_bf16.shape[1]
    tm = TOKEN_TILE
    return pl.pallas_call(
        functools.partial(_inproj_body, n_chunk=512),
        grid=(n // tm,),
        in_specs=[
            pl.BlockSpec((tm, d), lambda i: (i, 0)),
            pl.BlockSpec((1, d), lambda i: (0, 0)),
            pl.BlockSpec((1, d), lambda i: (0, 0)),
            pl.BlockSpec((d, cols), lambda i: (0, 0)),
        ],
        out_specs=pl.BlockSpec((tm, cols), lambda i: (i, 0)),
        out_shape=jax.ShapeDtypeStruct((n, cols), BF16),
        compiler_params=_cparams(("arbitrary",), 40),
        name="inproj",
    )(x2, g, b, w_bf16)


def _attn_body(slopes_ref, lq1_ref, lk1_ref, lq2_ref, lk2_ref, g_ref, q_ref, k_ref, v_ref, o_ref, *, blk):
    head = pl.program_id(1)
    qi = pl.program_id(2)
    slope = slopes_ref[head]
    lam = (jnp.exp(jnp.sum(lq1_ref[...] * lk1_ref[...], axis=-1, keepdims=True))
           - jnp.exp(jnp.sum(lq2_ref[...] * lk2_ref[...], axis=-1, keepdims=True))
           + LAMBDA_INIT)

    q = q_ref[...]
    lane = lax.broadcasted_iota(I32, q.shape, 1)
    zero = jnp.zeros_like(q)
    q2 = jnp.concatenate([jnp.where(lane < HEAD_DIM, q, zero), jnp.where(lane >= HEAD_DIM, q, zero)], axis=0)
    q2 = q2 * jnp.asarray(HEAD_DIM ** -0.5, BF16)

    def block(j, carry, diag):
        m, l, acc = carry
        start = pl.multiple_of(j * blk, blk)
        kb = k_ref[pl.ds(start, blk), :]
        vb = v_ref[pl.ds(start, blk), :]
        s = lax.dot_general(q2, kb, (((1,), (1,)), ((), ())), preferred_element_type=F32)
        col = lax.broadcasted_iota(I32, (1, blk), 1)
        s = s + ((j - qi) * blk + col).astype(F32) * slope
        if diag:
            row = lax.broadcasted_iota(I32, s.shape, 0)
            row = jnp.where(row >= blk, row - blk, row)
            s = jnp.where(lax.broadcasted_iota(I32, s.shape, 1) <= row, s, -jnp.inf)
        m_new = jnp.maximum(m, jnp.max(s, axis=1, keepdims=True))
        alpha = jnp.exp(m - m_new)
        p = jnp.exp(s - m_new)
        l = alpha * l + jnp.sum(p, axis=1, keepdims=True)
        acc = alpha * acc + jnp.dot(p.astype(BF16), vb, preferred_element_type=F32)
        return m_new, l, acc

    init = (jnp.full((2 * blk, 1), -jnp.inf, F32), jnp.zeros((2 * blk, 1), F32), jnp.zeros((2 * blk, V_DIM), F32))
    carry = lax.fori_loop(0, qi, functools.partial(block, diag=False), init)
    _, l, acc = block(qi, carry, True)
    o = acc / l
    od = o[:blk] - lam * o[blk:]
    ms = jnp.mean(od * od, axis=-1, keepdims=True)
    out = od * lax.rsqrt(ms + LN_EPS) * g_ref[...] * (1.0 - LAMBDA_INIT)
    o_ref[...] = out.astype(BF16)


def _attention(h, slopes, lq1, lk1, lq2, lk2, subln_g, *, batch, seq):
    n = h.shape[0]
    blk = ATTN_BLOCK
    nq = seq // blk
    vec = lambda: pl.BlockSpec((1, HEAD_DIM), lambda b, hd, qi, *_: (0, 0))
    grid_spec = pltpu.PrefetchScalarGridSpec(
        num_scalar_prefetch=1,
        grid=(batch, N_HEADS, nq),
        in_specs=[
            vec(), vec(), vec(), vec(),
            pl.BlockSpec((1, V_DIM), lambda b, hd, qi, *_: (0, 0)),
            pl.BlockSpec((blk, V_DIM), lambda b, hd, qi, *_: (b * nq + qi, hd)),
            pl.BlockSpec((seq, V_DIM), lambda b, hd, qi, *_: (b, N_HEADS + hd)),
            pl.BlockSpec((seq, V_DIM), lambda b, hd, qi, *_: (b, 2 * N_HEADS + hd)),
        ],
        out_specs=pl.BlockSpec((blk, V_DIM), lambda b, hd, qi, *_: (b * nq + qi, hd)),
    )
    return pl.pallas_call(
        functools.partial(_attn_body, blk=blk),
        grid_spec=grid_spec,
        out_shape=jax.ShapeDtypeStruct((n, ATTN_WIDTH), BF16),
        compiler_params=_cparams(("arbitrary", "arbitrary", "arbitrary"), 32),
        name="diff_attn",
    )(slopes, lq1, lk1, lq2, lk2, subln_g, h, h, h)


CONV_PAD = 32


def _conv_body(a_ref, g_ref, w_ref, cb_ref, lg_ref, lb_ref, o_ref, glu_ref, *, seq):
    tile = CONV_TILE
    glu_ref[0:CONV_PAD, :] = jnp.zeros((CONV_PAD, CONV_WIDTH), F32)

    def glu(i, _):
        st = pl.multiple_of(i * tile, tile)
        a = a_ref[pl.ds(st, tile), :].astype(F32)
        g = g_ref[pl.ds(st, tile), :].astype(F32)
        glu_ref[pl.ds(CONV_PAD + st, tile), :] = a * _sigmoid(g)
        return 0

    lax.fori_loop(0, seq // tile, glu, 0)

    def conv(i, _):
        st = pl.multiple_of(i * tile, tile)
        acc = jnp.zeros((tile, CONV_WIDTH), F32) + cb_ref[...]
        window = glu_ref[pl.ds(st, tile + CONV_PAD), :]
        for r in range(8):
            taps = [j for j in range(CONV_KERNEL) if (CONV_PAD - (CONV_KERNEL - 1) + j) % 8 == r]
            if not taps:
                continue
            hi = max(CONV_PAD - (CONV_KERNEL - 1) + j for j in taps) - r + tile
            shifted = window[r:r + hi, :]
            for j in taps:
                off = CONV_PAD - (CONV_KERNEL - 1) + j - r
                acc = acc + shifted[off:off + tile, :] * w_ref[j:j + 1, :]
        y = _layer_norm(acc, lg_ref[...], lb_ref[...])
        o_ref[pl.ds(st, tile), :] = (y * _sigmoid(y)).astype(BF16)
        return 0

    lax.fori_loop(0, seq // tile, conv, 0)


def _conv_module(h, conv_w, conv_b, ln_g, ln_b, *, batch, seq):
    n = h.shape[0]
    u_col = 3 * ATTN_WIDTH // CONV_WIDTH
    row = lambda: pl.BlockSpec((1, CONV_WIDTH), lambda b: (0, 0))
    return pl.pallas_call(
        functools.partial(_conv_body, seq=seq),
        grid=(batch,),
        in_specs=[
            pl.BlockSpec((seq, CONV_WIDTH), lambda b: (b, u_col)),
            pl.BlockSpec((seq, CONV_WIDTH), lambda b: (b, u_col + 1)),
            pl.BlockSpec((CONV_KERNEL, CONV_WIDTH), lambda b: (0, 0)),
            row(), row(), row(),
        ],
        out_specs=pl.BlockSpec((seq, CONV_WIDTH), lambda b: (b, 0)),
        out_shape=jax.ShapeDtypeStruct((n, CONV_WIDTH), BF16),
        scratch_shapes=[pltpu.VMEM((CONV_PAD + seq, CONV_WIDTH), F32)],
        compiler_params=_cparams(("arbitrary",), 32),
        name="conformer_conv",
    )(h, h, conv_w, conv_b, ln_g, ln_b)


def _mix_body(x_ref, a_ref, c_ref, wo_ref, g0_ref, b0_ref, g1_ref, b1_ref, rwt_ref, rb_ref,
              x1_ref, xs_ref, pos_ref, gate_ref, cc_ref):
    tb = x_ref.shape[0]
    xn = _layer_norm(x_ref[...], g0_ref[...], b0_ref[...])
    mix = (jnp.dot(a_ref[...], wo_ref[0:ATTN_WIDTH, :], preferred_element_type=F32)
           + jnp.dot(c_ref[...], wo_ref[ATTN_WIDTH:, :], preferred_element_type=F32))
    x1 = _layer_norm(DEEPNORM_ALPHA * xn + mix, g1_ref[...], b1_ref[...])
    x1_ref[...] = x1
    x1b = x1.astype(BF16)

    logits = lax.dot_general(rwt_ref[...], x1, (((1,), (1,)), ((), ())),
                             precision=lax.Precision.HIGHEST, preferred_element_type=F32) + rb_ref[...]
    eidx = lax.broadcasted_iota(I32, logits.shape, 0)
    work = logits
    vals, hots = [], []
    for _ in range(TOP_K):
        mx = jnp.max(work, axis=0, keepdims=True)
        sel = jnp.min(jnp.where(work == mx, eidx, N_EXPERTS), axis=0, keepdims=True)
        hot = eidx == sel
        vals.append(mx)
        hots.append(hot)
        work = jnp.where(hot, -jnp.inf, work)
    ex = [jnp.exp(v - vals[0]) for v in vals]
    den = ex[0] + ex[1] + ex[2] + ex[3]
    for k in range(TOP_K):
        gate_ref[k:k + 1, :] = ex[k] / den

    member = jnp.zeros(logits.shape, F32)
    for hot in hots:
        member = member + jnp.where(hot, 1.0, 0.0)
    before = (lax.broadcasted_iota(I32, (tb, tb), 0) < lax.broadcasted_iota(I32, (tb, tb), 1))
    rank = jnp.dot(member.astype(BF16), jnp.where(before, 1.0, 0.0).astype(BF16), preferred_element_type=F32)
    count = jnp.sum(member, axis=1, keepdims=True)
    chunks = jnp.floor((count + (BF16_ROWS - 1.0)) * (1.0 / BF16_ROWS))
    cc_ref[...] = jnp.broadcast_to(chunks, cc_ref.shape).astype(I32)
    lower = (lax.broadcasted_iota(I32, (N_EXPERTS, N_EXPERTS), 1) < lax.broadcasted_iota(I32, (N_EXPERTS, N_EXPERTS), 0))
    seg_start = jnp.dot(jnp.where(lower, 1.0, 0.0).astype(BF16),
                        jnp.broadcast_to(chunks, (N_EXPERTS, LANES)).astype(BF16),
                        preferred_element_type=F32)[:, 0:1]
    slot = seg_start * float(BF16_ROWS) + rank
    pos = [jnp.sum(jnp.where(hot, slot, 0.0), axis=0, keepdims=True).astype(I32) for hot in hots]
    for k in range(TOP_K):
        pos_ref[k:k + 1, :] = pos[k]

    rows = 256
    for rc in range(xs_ref.shape[0] // rows):
        r = lax.broadcasted_iota(I32, (rows, tb), 0) + rc * rows
        sel = jnp.zeros((rows, tb), F32)
        for k in range(TOP_K):
            sel = sel + jnp.where(r == pos[k], 1.0, 0.0)
        xs_ref[rc * rows:(rc + 1) * rows, :] = jnp.dot(sel.astype(BF16), x1b, preferred_element_type=F32).astype(BF16)


def _mix(x2, a, c, wo_bf16, g0, b0, g1, b1, rwt, rb):
    n, d = x2.shape
    tb = TOKEN_TILE
    nb = n // tb
    row = lambda: pl.BlockSpec((1, d), lambda i: (0, 0))
    return pl.pallas_call(
        _mix_body,
        grid=(nb,),
        in_specs=[
            pl.BlockSpec((tb, d), lambda i: (i, 0)),
            pl.BlockSpec((tb, ATTN_WIDTH), lambda i: (i, 0)),
            pl.BlockSpec((tb, CONV_WIDTH), lambda i: (i, 0)),
            pl.BlockSpec((d, d), lambda i: (0, 0)),
            row(), row(), row(), row(),
            pl.BlockSpec((N_EXPERTS, d), lambda i: (0, 0)),
            pl.BlockSpec((N_EXPERTS, 1), lambda i: (0, 0)),
        ],
        out_specs=[
            pl.BlockSpec((tb, d), lambda i: (i, 0)),
            pl.BlockSpec((None, LOCAL_ROWS, d), lambda i: (i, 0, 0)),
            pl.BlockSpec((None, TOP_K, tb), lambda i: (i, 0, 0)),
            pl.BlockSpec((None, TOP_K, tb), lambda i: (i, 0, 0)),
            pl.BlockSpec((None, N_EXPERTS, LANES), lambda i: (i, 0, 0)),
        ],
        out_shape=[
            jax.ShapeDtypeStruct((n, d), F32),
            jax.ShapeDtypeStruct((nb, LOCAL_ROWS, d), BF16),
            jax.ShapeDtypeStruct((nb, TOP_K, tb), I32),
            jax.ShapeDtypeStruct((nb, TOP_K, tb), F32),
            jax.ShapeDtypeStruct((nb, N_EXPERTS, LANES), I32),
        ],
        compiler_params=_cparams(("arbitrary",), 48),
        name="mix_router_sort",
    )(x2, a, c, wo_bf16, g0, b0, g1, b1, rwt, rb)


def _max_tiles(nb):
    return (nb * CHUNKS_PER_BLOCK + CHUNKS_PER_TILE - 1) // CHUNKS_PER_TILE + N_EXPERTS + 1


def _build_schedule(cc):
    nb = cc.shape[0]
    cpt = CHUNKS_PER_TILE
    ne1 = N_EXPERTS + 1
    tail = CHUNKS_PER_BLOCK - jnp.sum(cc, axis=1)
    cce = jnp.concatenate([cc, tail[:, None]], axis=1)
    seg_start = jnp.cumsum(cce, axis=1) - cce
    cum_b = jnp.cumsum(cce, axis=0)
    tot = cum_b[-1]
    tiles_e = (tot + cpt - 1) // cpt
    tile_end = jnp.cumsum(tiles_e)
    tile_start = tile_end - tiles_e
    n_tiles = tile_end[-1]
    max_tiles = _max_tiles(nb)
    t = jnp.arange(max_tiles, dtype=I32)
    te = jnp.minimum(jnp.sum(tile_end[None, :] <= t[:, None], axis=1), ne1 - 1).astype(I32)
    te = jnp.where(t < n_tiles, te, te[jnp.maximum(n_tiles - 1, 0)])
    nv = jnp.clip(tot[te] - (t - tile_start[te]) * cpt, 0, cpt)
    nv = jnp.where(t < n_tiles, nv, 0).astype(I32)
    s = jnp.arange(max_tiles * cpt, dtype=I32)
    ts = s // cpt
    es = te[ts]
    q = (ts - tile_start[es]) * cpt + s % cpt
    valid = (q < tot[es]) & (ts < n_tiles)
    cum_t = cum_b.T[es]
    b = jnp.minimum(jnp.sum(cum_t <= q[:, None], axis=1), nb - 1)
    excl = (cum_b - cce)[b, es]
    src = b * CHUNKS_PER_BLOCK + seg_start[b, es] + (q - excl)
    src = jnp.where(valid, src, 0).astype(I32)
    return te, nv, src, jnp.reshape(n_tiles, (1,)).astype(I32)


def _moe_body(te_ref, nv_ref, src_ref, nt_ref, xs_hbm, wgu_ref, bgu_ref, wd_ref, bd_ref, y_hbm,
              xbuf, ybuf, act, wgu_b, wd_b, in_sem, out_sem):
    t = pl.program_id(0)
    nt = nt_ref[0]
    cpt = CHUNKS_PER_TILE
    f = wd_ref.shape[0]

    def rows(c):
        return pl.ds(pl.multiple_of(c * BF16_ROWS, BF16_ROWS), BF16_ROWS)

    def in_copy(tile, slot, c):
        return pltpu.make_async_copy(xs_hbm.at[src_ref[tile * cpt + c]], xbuf.at[slot, rows(c)], in_sem.at[slot])

    def out_copy(tile, slot, c):
        return pltpu.make_async_copy(ybuf.at[slot, rows(c)], y_hbm.at[src_ref[tile * cpt + c]], out_sem.at[slot])

    def start_gather(tile, slot):
        lax.fori_loop(0, cpt, lambda c, _: (in_copy(tile, slot, c).start(), 0)[1], 0)

    def wait_gather(tile, slot):
        lax.fori_loop(0, cpt, lambda c, _: (in_copy(tile, slot, c).wait(), 0)[1], 0)

    def start_scatter(tile, slot):
        lax.fori_loop(0, nv_ref[tile], lambda c, _: (out_copy(tile, slot, c).start(), 0)[1], 0)

    def wait_scatter(tile, slot):
        lax.fori_loop(0, nv_ref[tile], lambda c, _: (out_copy(tile, slot, c).wait(), 0)[1], 0)

    @pl.when(t == 0)
    def _():
        start_gather(0, 0)

    @pl.when(t + 1 < nt)
    def _():
        start_gather(t + 1, (t + 1) % 2)

    @pl.when(t < nt)
    def _():
        slot = t % 2
        e = te_ref[t]
        wait_gather(t, slot)

        @pl.when((e < N_EXPERTS) & ((t == 0) | (te_ref[jnp.maximum(t - 1, 0)] != e)))
        def _():
            wgu_b[...] = wgu_ref[...].astype(BF16)
            wd_b[...] = wd_ref[...].astype(BF16)

        @pl.when(t >= 2)
        def _():
            wait_scatter(t - 2, slot)

        @pl.when(e < N_EXPERTS)
        def _():
            x = xbuf[slot]
            nc = 512
            for j in range(f // nc):
                gsl = slice(j * nc, (j + 1) * nc)
                usl = slice(f + j * nc, f + (j + 1) * nc)
                g = jnp.dot(x, wgu_b[:, gsl], preferred_element_type=F32) + bgu_ref[:, gsl]
                u = jnp.dot(x, wgu_b[:, usl], preferred_element_type=F32) + bgu_ref[:, usl]
                g = jnp.minimum(g, SWIGLU_LIMIT)
                u = jnp.clip(u, -SWIGLU_LIMIT, SWIGLU_LIMIT)
                act[:, gsl] = (g * _sigmoid(SWIGLU_ALPHA * g) * (u + 1.0)).astype(BF16)
            a = act[...]
            for j in range(wd_ref.shape[1] // nc):
                sl = slice(j * nc, (j + 1) * nc)
                y = jnp.dot(a, wd_b[:, sl], preferred_element_type=F32) + bd_ref[:, sl]
                ybuf[slot, :, sl] = y.astype(BF16)

        @pl.when(e >= N_EXPERTS)
        def _():
            ybuf[slot] = jnp.zeros(ybuf.shape[1:], BF16)

        start_scatter(t, slot)

        @pl.when(t == nt - 1)
        def _():
            wait_scatter(t, slot)

            @pl.when(t >= 1)
            def _():
                wait_scatter(t - 1, 1 - slot)


def _moe(xs_chunks, te, nv, src, nt, w_gate_up, b_gate_up, w_down, b_down):
    nch, _, d = xs_chunks.shape
    f = w_down.shape[1]
    max_tiles = te.shape[0]
    wsel = lambda t, te_r, nv_r, src_r, nt_r: (jnp.minimum(te_r[t], N_EXPERTS - 1), 0, 0)
    grid_spec = pltpu.PrefetchScalarGridSpec(
        num_scalar_prefetch=4,
        grid=(max_tiles,),
        in_specs=[
            pl.BlockSpec(memory_space=pl.ANY),
            pl.BlockSpec((None, d, 2 * f), wsel),
            pl.BlockSpec((None, 1, 2 * f), wsel),
            pl.BlockSpec((None, f, d), wsel),
            pl.BlockSpec((None, 1, d), wsel),
        ],
        out_specs=pl.BlockSpec(memory_space=pl.ANY),
        scratch_shapes=[
            pltpu.VMEM((2, MOE_TILE, d), BF16),
            pltpu.VMEM((2, MOE_TILE, d), BF16),
            pltpu.VMEM((MOE_TILE, f), BF16),
            pltpu.VMEM((d, 2 * f), BF16),
            pltpu.VMEM((f, d), BF16),
            pltpu.SemaphoreType.DMA((2,)),
            pltpu.SemaphoreType.DMA((2,)),
        ],
    )
    return pl.pallas_call(
        _moe_body,
        grid_spec=grid_spec,
        out_shape=jax.ShapeDtypeStruct((nch, BF16_ROWS, d), BF16),
        compiler_params=_cparams(("arbitrary",), 56),
        name="moe_grouped",
    )(te, nv, src, nt, xs_chunks, w_gate_up, b_gate_up, w_down, b_down)


def _combine_body(y_ref, pos_ref, gate_ref, x1_ref, g_ref, b_ref, o_ref):
    tb = x1_ref.shape[0]
    pos = pos_ref[...]
    gate = gate_ref[...]
    acc = jnp.zeros(x1_ref.shape, F32)
    cols = 512
    for rc in range(y_ref.shape[0] // cols):
        r = lax.broadcasted_iota(I32, (tb, cols), 1) + rc * cols
        w = jnp.zeros((tb, cols), F32)
        for k in range(TOP_K):
            w = w + jnp.where(r == pos[:, k:k + 1], gate[:, k:k + 1], 0.0)
        acc = acc + jnp.dot(w.astype(BF16), y_ref[rc * cols:(rc + 1) * cols, :], preferred_element_type=F32)
    o_ref[...] = _layer_norm(DEEPNORM_ALPHA * x1_ref[...] + acc, g_ref[...], b_ref[...])


def _combine(y_local, pos, gates, x1, g, b):
    n, d = x1.shape
    tb = TOKEN_TILE
    nb = n // tb
    row = lambda: pl.BlockSpec((1, d), lambda i: (0, 0))
    return pl.pallas_call(
        _combine_body,
        grid=(nb,),
        in_specs=[
            pl.BlockSpec((None, LOCAL_ROWS, d), lambda i: (i, 0, 0)),
            pl.BlockSpec((None, tb, TOP_K), lambda i: (i, 0, 0)),
            pl.BlockSpec((None, tb, TOP_K), lambda i: (i, 0, 0)),
            pl.BlockSpec((tb, d), lambda i: (i, 0)),
            row(), row(),
        ],
        out_specs=pl.BlockSpec((tb, d), lambda i: (i, 0)),
        out_shape=jax.ShapeDtypeStruct((n, d), F32),
        compiler_params=_cparams(("arbitrary",), 40),
        name="combine_ln",
    )(y_local, pos, gates, x1, g, b)


def kernel(x, emb_ln_g, emb_ln_b, w_in, lambda_q1, lambda_k1, lambda_q2, lambda_k2, subln_g, conv_w, conv_b,
           conv_ln_g, conv_ln_b, w_out, ln1_g, ln1_b, router_w, router_b, w_gate_up, b_gate_up, w_down, b_down,
           ln2_g, ln2_b):
    batch, seq, d = x.shape
    n = batch * seq
    assert n % TOKEN_TILE == 0 and seq % ATTN_BLOCK == 0 and seq % CONV_TILE == 0
    assert w_in.shape[0] == DEPTH
    x2 = x.reshape(n, d)
    row = lambda v: v.reshape(1, -1).astype(F32)
    slopes = jnp.array([(2.0 ** (-8.0 / N_HEADS)) ** (i + 1) for i in range(N_HEADS)], dtype=F32)

    h = _inproj(x2, row(emb_ln_g), row(emb_ln_b), w_in[0].astype(BF16))
    a = _attention(h, slopes, row(lambda_q1[0]), row(lambda_k1[0]), row(lambda_q2[0]), row(lambda_k2[0]),
                   row(subln_g[0]), batch=batch, seq=seq)
    c = _conv_module(h, conv_w[0], row(conv_b[0]), row(conv_ln_g[0]), row(conv_ln_b[0]), batch=batch, seq=seq)
    x1, xs_local, pos_t, gates_t, cc = _mix(
        x2, a, c, w_out[0].astype(BF16), row(emb_ln_g), row(emb_ln_b), row(ln1_g[0]), row(ln1_b[0]),
        router_w[0].T, router_b[0].reshape(N_EXPERTS, 1))

    te, nv, src, nt = _build_schedule(cc[:, :, 0])
    nb = n // TOKEN_TILE
    y_chunks = _moe(xs_local.reshape(nb * CHUNKS_PER_BLOCK, BF16_ROWS, d), te, nv, src, nt,
                    w_gate_up[0], b_gate_up[0].reshape(N_EXPERTS, 1, -1), w_down[0],
                    b_down[0].reshape(N_EXPERTS, 1, -1))
    y_local = y_chunks.reshape(nb, LOCAL_ROWS, d)
    out = _combine(y_local, jnp.transpose(pos_t, (0, 2, 1)), jnp.transpose(gates_t, (0, 2, 1)), x1,
                   row(ln2_g[0]), row(ln2_b[0]))
    return out.reshape(batch, seq, d)
```

```python
import functools

import jax
import jax.numpy as jnp
from jax import lax
from jax.experimental import pallas as pl
from jax.experimental.pallas import tpu as pltpu

F32 = jnp.float32
BF16 = jnp.bfloat16
I32 = jnp.int32

N_HEADS = 4
HEAD_DIM = 64
V_DIM = 128
ATTN_WIDTH = N_HEADS * V_DIM
CONV_WIDTH = 512
CONV_KERNEL = 31
N_EXPERTS = 32
TOP_K = 4
SWIGLU_LIMIT = 7.0
SWIGLU_ALPHA = 1.702
LN_EPS = 1e-5
DEPTH = 1
DEEPNORM_ALPHA = (2.0 * DEPTH) ** 0.25
LAMBDA_INIT = 0.2
LOG2_E = 1.4426950408889634
Q_COLS = N_HEADS * 2 * HEAD_DIM
Q_SCALE = HEAD_DIM ** -0.5 * LOG2_E

LANES = 128
BF16_ROWS = 16

TOKEN_TILE = 512
ATTN_KV_BLOCK = 256
ATTN_SUB = 128
CONV_TILE = 64
CONV_ROW_STRIDE = 4
CONV_GLU_TILE = 128
MOE_TILE = 512
CHUNKS_PER_TILE = MOE_TILE // BF16_ROWS
LOCAL_ROWS = TOP_K * TOKEN_TILE + N_EXPERTS * BF16_ROWS
CHUNKS_PER_BLOCK = LOCAL_ROWS // BF16_ROWS


def _cparams(semantics, vmem_mib):
    return pltpu.CompilerParams(dimension_semantics=semantics, vmem_limit_bytes=vmem_mib * 1024 * 1024)


def _layer_norm(x, g, b):
    mu = jnp.mean(x, axis=-1, keepdims=True)
    xc = x - mu
    var = jnp.mean(xc * xc, axis=-1, keepdims=True)
    return xc * lax.rsqrt(var + LN_EPS) * g + b


def _sigmoid(x):
    return 1.0 / (1.0 + jnp.exp(-x))


def _inproj_body(x_ref, g_ref, b_ref, w_ref, h_ref, *, n_chunk):
    xn = _layer_norm(x_ref[...], g_ref[...], b_ref[...]).astype(BF16)
    for j in range(w_ref.shape[1] // n_chunk):
        sl = slice(j * n_chunk, (j + 1) * n_chunk)
        hj = jnp.dot(xn, w_ref[:, sl], preferred_element_type=F32)
        if (j + 1) * n_chunk <= Q_COLS:
            hj = hj * Q_SCALE
        h_ref[:, sl] = hj.astype(BF16)


def _inproj(x2, g, b, w_bf16):
    n, d = x2.shape
    cols = w_bf16.shape[1]
    tm = TOKEN_TILE
    return pl.pallas_call(
        functools.partial(_inproj_body, n_chunk=512),
        grid=(n // tm,),
        in_specs=[
            pl.BlockSpec((tm, d), lambda i: (i, 0)),
            pl.BlockSpec((1, d), lambda i: (0, 0)),
            pl.BlockSpec((1, d), lambda i: (0, 0)),
            pl.BlockSpec((d, cols), lambda i: (0, 0)),
        ],
        out_specs=pl.BlockSpec((tm, cols), lambda i: (i, 0)),
        out_shape=jax.ShapeDtypeStruct((n, cols), BF16),
        compiler_params=_cparams(("arbitrary",), 40),
        name="inproj",
    )(x2, g, b, w_bf16)


def _attn_body(slopes_ref, lq1_ref, lk1_ref, lq2_ref, lk2_ref, g_ref, q_ref, k_ref, v_ref, o_ref,
               q2_ref, m_ref, l_ref, acc_ref, *, seq):
    head = pl.program_id(1)
    slope = slopes_ref[head] * LOG2_E
    lam =(jnp.exp(jnp.sum(lq1_ref[...] * lk1_ref[...], axis=-1, keepdims=True))
           - jnp.exp(jnp.sum(lq2_ref[...] * lk2_ref[...], axis=-1, keepdims=True))
           + LAMBDA_INIT)

    q = q_ref[...]
    lane = lax.broadcasted_iota(I32, q.shape, 1)
    zero = jnp.zeros_like(q)
    q2_ref[0:seq, :] = jnp.where(lane < HEAD_DIM, q, zero)
    q2_ref[seq:, :] = jnp.where(lane >= HEAD_DIM, q, zero)
    m_ref[...] = jnp.full(m_ref.shape, -jnp.inf, F32)
    l_ref[...] = jnp.zeros(l_ref.shape, F32)
    acc_ref[...] = jnp.zeros(acc_ref.shape, F32)
    kb_rows = ATTN_KV_BLOCK
    sub = ATTN_SUB
    ones = jnp.ones((kb_rows, V_DIM), BF16)
    col = lax.broadcasted_iota(I32, (1, kb_rows), 1)

    for d in range(seq // kb_rows):
        k0 = d * kb_rows
        kb = k_ref[k0:k0 + kb_rows, :]
        vext = jnp.concatenate([v_ref[k0:k0 + kb_rows, :], ones], axis=1)
        for half in range(2):
            for i in range(k0 // sub, seq // sub):
                r0 = i * sub
                rows = slice(half * seq + r0, half * seq + r0 + sub)
                ncol = min(kb_rows, -(-(r0 + sub - k0) // LANES) * LANES)
                bias = (col[:, 0:ncol] + (k0 - r0)).astype(F32) * slope
                s = lax.dot_general(q2_ref[rows, :], kb[0:ncol, :], (((1,), (1,)), ((), ())),
                                    preferred_element_type=F32) + bias
                if k0 + ncol - 1 > r0:
                    row = lax.broadcasted_iota(I32, s.shape, 0) + (r0 - k0)
                    s = jnp.where(lax.broadcasted_iota(I32, s.shape, 1) <= row, s, -jnp.inf)
                m_prev = m_ref[rows, :]
                m_new = jnp.maximum(m_prev, jnp.max(s, axis=1, keepdims=True))
                alpha = jnp.exp2(m_prev - m_new)
                p = jnp.exp2(s - jnp.concatenate([m_new] * (ncol // LANES), axis=1))
                pv = jnp.dot(p.astype(BF16), vext[0:ncol, :], preferred_element_type=F32)
                acc_ref[rows, :] = alpha * acc_ref[rows, :] + pv[:, :V_DIM]
                l_ref[rows, :] = alpha * l_ref[rows, :] + pv[:, V_DIM:]
                m_ref[rows, :] = m_new

    o = acc_ref[...] / l_ref[...]
    od = o[:seq] - lam * o[seq:]
    ms = jnp.mean(od * od, axis=-1, keepdims=True)
    out = od * lax.rsqrt(ms + LN_EPS) * g_ref[...] * (1.0 - LAMBDA_INIT)
    o_ref[...] = out.astype(BF16)


def _attention(h, slopes, lq1, lk1, lq2, lk2, subln_g, *, batch, seq):
    n = h.shape[0]
    vec = lambda: pl.BlockSpec((1, HEAD_DIM), lambda b, hd, *_: (0, 0))
    grid_spec = pltpu.PrefetchScalarGridSpec(
        num_scalar_prefetch=1,
        grid=(batch, N_HEADS),
        in_specs=[
            vec(), vec(), vec(), vec(),
            pl.BlockSpec((1, V_DIM), lambda b, hd, *_: (0, 0)),
            pl.BlockSpec((seq, V_DIM), lambda b, hd, *_: (b, hd)),
            pl.BlockSpec((seq, V_DIM), lambda b, hd, *_: (b, N_HEADS + hd)),
            pl.BlockSpec((seq, V_DIM), lambda b, hd, *_: (b, 2 * N_HEADS + hd)),
        ],
        out_specs=pl.BlockSpec((seq, V_DIM), lambda b, hd, *_: (b, hd)),
        scratch_shapes=[
            pltpu.VMEM((2 * seq, V_DIM), BF16),
            pltpu.VMEM((2 * seq, LANES), F32),
            pltpu.VMEM((2 * seq, LANES), F32),
            pltpu.VMEM((2 * seq, V_DIM), F32),
        ],
    )
    return pl.pallas_call(
        functools.partial(_attn_body, seq=seq),
        grid_spec=grid_spec,
        out_shape=jax.ShapeDtypeStruct((n, ATTN_WIDTH), BF16),
        compiler_params=_cparams(("arbitrary", "arbitrary"), 32),
        name="diff_attn",
    )(slopes, lq1, lk1, lq2, lk2, subln_g, h, h, h)


CONV_PAD = 32


def _conv_body(a_ref, g_ref, w_ref, cb_ref, lg_ref, lb_ref, o_ref, glu_ref, conv_ref, *, seq):
    n_slab = CONV_WIDTH // LANES
    for c in range(n_slab):
        glu_ref[c, 0:CONV_PAD, :] = jnp.zeros((CONV_PAD, LANES), F32)

    def glu(i, _):
        st = pl.multiple_of(i * CONV_GLU_TILE, CONV_GLU_TILE)
        a = a_ref[pl.ds(st, CONV_GLU_TILE), :].astype(F32)
        g = g_ref[pl.ds(st, CONV_GLU_TILE), :].astype(F32)
        val = a * _sigmoid(g)
        for c in range(n_slab):
            glu_ref[c, pl.ds(CONV_PAD + st, CONV_GLU_TILE), :] = val[:, c * LANES:(c + 1) * LANES]
        return 0

    lax.fori_loop(0, seq // CONV_GLU_TILE, glu, 0)

    span = 8 * CONV_ROW_STRIDE
    for c in range(n_slab):
        lanes = slice(c * LANES, (c + 1) * LANES)
        taps = [jnp.broadcast_to(w_ref[j:j + 1, lanes], (8, LANES)) for j in range(CONV_KERNEL)]
        bias = jnp.broadcast_to(cb_ref[:, lanes], (8, LANES))

        def conv(i, _):
            st = i * CONV_TILE
            for grp in range(CONV_TILE // span):
                for p in range(CONV_ROW_STRIDE):
                    t0 = st + grp * span + p
                    acc = bias
                    for j in range(CONV_KERNEL):
                        src = pl.ds(t0 + (CONV_PAD - (CONV_KERNEL - 1) + j), 8, stride=CONV_ROW_STRIDE)
                        acc = acc + glu_ref[c, src, :] * taps[j]
                    conv_ref[c, pl.ds(t0, 8, stride=CONV_ROW_STRIDE), :] = acc
            return 0

        lax.fori_loop(0, seq // CONV_TILE, conv, 0)

    def norm(i, _):
        st = pl.multiple_of(i * CONV_GLU_TILE, CONV_GLU_TILE)
        x = jnp.concatenate([conv_ref[c, pl.ds(st, CONV_GLU_TILE), :] for c in range(n_slab)], axis=1)
        y = _layer_norm(x, lg_ref[...], lb_ref[...])
        o_ref[pl.ds(st, CONV_GLU_TILE), :] = (y * _sigmoid(y)).astype(BF16)
        return 0

    lax.fori_loop(0, seq // CONV_GLU_TILE, norm, 0)


def _conv_module(h, conv_w, conv_b, ln_g, ln_b, *, batch, seq):
    n = h.shape[0]
    u_col = 3 * ATTN_WIDTH // CONV_WIDTH
    row = lambda: pl.BlockSpec((1, CONV_WIDTH), lambda b: (0, 0))
    return pl.pallas_call(
        functools.partial(_conv_body, seq=seq),
        grid=(batch,),
        in_specs=[
            pl.BlockSpec((seq, CONV_WIDTH), lambda b: (b, u_col)),
            pl.BlockSpec((seq, CONV_WIDTH), lambda b: (b, u_col + 1)),
            pl.BlockSpec((CONV_KERNEL, CONV_WIDTH), lambda b: (0, 0)),
            row(), row(), row(),
        ],
        out_specs=pl.BlockSpec((seq, CONV_WIDTH), lambda b: (b, 0)),
        out_shape=jax.ShapeDtypeStruct((n, CONV_WIDTH), BF16),
        scratch_shapes=[
            pltpu.VMEM((CONV_WIDTH // LANES, CONV_PAD + seq, LANES), F32),
            pltpu.VMEM((CONV_WIDTH // LANES, seq, LANES), F32),
        ],
        compiler_params=_cparams(("arbitrary",), 32),
        name="conformer_conv",
    )(h, h, conv_w, conv_b, ln_g, ln_b)


def _mix_body(x_ref, a_ref, c_ref, wo_ref, g0_ref, b0_ref, g1_ref, b1_ref, rwt_ref, rb_ref,
              x1_ref, xs_ref, pos_ref, gate_ref, cc_ref):
    tb = x_ref.shape[0]
    xn = _layer_norm(x_ref[...], g0_ref[...], b0_ref[...])
    mix = (jnp.dot(a_ref[...], wo_ref[0:ATTN_WIDTH, :], preferred_element_type=F32)
           + jnp.dot(c_ref[...], wo_ref[ATTN_WIDTH:, :], preferred_element_type=F32))
    x1 = _layer_norm(DEEPNORM_ALPHA * xn + mix, g1_ref[...], b1_ref[...])
    x1_ref[...] = x1
    x1b = x1.astype(BF16)

    logits = lax.dot_general(rwt_ref[...], x1, (((1,), (1,)), ((), ())),
                             precision=lax.Precision.HIGHEST, preferred_element_type=F32) + rb_ref[...]
    eidx = lax.broadcasted_iota(I32, logits.shape, 0)
    work = logits
    vals, hots = [], []
    for _ in range(TOP_K):
        mx = jnp.max(work, axis=0, keepdims=True)
        sel = jnp.min(jnp.where(work == mx, eidx, N_EXPERTS), axis=0, keepdims=True)
        hot = eidx == sel
        vals.append(mx)
        hots.append(hot)
        work = jnp.where(hot, -jnp.inf, work)
    ex = [jnp.exp(v - vals[0]) for v in vals]
    den = ex[0] + ex[1] + ex[2] + ex[3]
    for k in range(TOP_K):
        gate_ref[k:k + 1, :] = ex[k] / den

    member = jnp.zeros(logits.shape, F32)
    for hot in hots:
        member = member + jnp.where(hot, 1.0, 0.0)
    before = (lax.broadcasted_iota(I32, (tb, tb), 0) < lax.broadcasted_iota(I32, (tb, tb), 1))
    rank = jnp.dot(member.astype(BF16), jnp.where(before, 1.0, 0.0).astype(BF16), preferred_element_type=F32)
    count = jnp.sum(member, axis=1, keepdims=True)
    chunks = jnp.floor((count + (BF16_ROWS - 1.0)) * (1.0 / BF16_ROWS))
    cc_ref[...] = jnp.broadcast_to(chunks, cc_ref.shape).astype(I32)
    lower = (lax.broadcasted_iota(I32, (N_EXPERTS, N_EXPERTS), 1) < lax.broadcasted_iota(I32, (N_EXPERTS, N_EXPERTS), 0))
    seg_start = jnp.dot(jnp.where(lower, 1.0, 0.0).astype(BF16),
                        jnp.broadcast_to(chunks, (N_EXPERTS, LANES)).astype(BF16),
                        preferred_element_type=F32)[:, 0:1]
    slot = seg_start * float(BF16_ROWS) + rank
    pos = [jnp.sum(jnp.where(hot, slot, 0.0), axis=0, keepdims=True).astype(I32) for hot in hots]
    for k in range(TOP_K):
        pos_ref[k:k + 1, :] = pos[k]

    rows = 256
    for rc in range(xs_ref.shape[0] // rows):
        r = lax.broadcasted_iota(I32, (rows, tb), 0) + rc * rows
        sel = jnp.zeros((rows, tb), F32)
        for k in range(TOP_K):
            sel = sel + jnp.where(r == pos[k], 1.0, 0.0)
        xs_ref[rc * rows:(rc + 1) * rows, :] = jnp.dot(sel.astype(BF16), x1b, preferred_element_type=F32).astype(BF16)


def _mix(x2, a, c, wo_bf16, g0, b0, g1, b1, rwt, rb):
    n, d = x2.shape
    tb = TOKEN_TILE
    nb = n // tb
    row = lambda: pl.BlockSpec((1, d), lambda i: (0, 0))
    return pl.pallas_call(
        _mix_body,
        grid=(nb,),
        in_specs=[
            pl.BlockSpec((tb, d), lambda i: (i, 0)),
            pl.BlockSpec((tb, ATTN_WIDTH), lambda i: (i, 0)),
            pl.BlockSpec((tb, CONV_WIDTH), lambda i: (i, 0)),
            pl.BlockSpec((d, d), lambda i: (0, 0)),
            row(), row(), row(), row(),
            pl.BlockSpec((N_EXPERTS, d), lambda i: (0, 0)),
            pl.BlockSpec((N_EXPERTS, 1), lambda i: (0, 0)),
        ],
        out_specs=[
            pl.BlockSpec((tb, d), lambda i: (i, 0)),
            pl.BlockSpec((None, LOCAL_ROWS, d), lambda i: (i, 0, 0)),
            pl.BlockSpec((None, TOP_K, tb), lambda i: (i, 0, 0)),
            pl.BlockSpec((None, TOP_K, tb), lambda i: (i, 0, 0)),
            pl.BlockSpec((None, N_EXPERTS, LANES), lambda i: (i, 0, 0)),
        ],
        out_shape=[
            jax.ShapeDtypeStruct((n, d), F32),
            jax.ShapeDtypeStruct((nb, LOCAL_ROWS, d), BF16),
            jax.ShapeDtypeStruct((nb, TOP_K, tb), I32),
            jax.ShapeDtypeStruct((nb, TOP_K, tb), F32),
            jax.ShapeDtypeStruct((nb, N_EXPERTS, LANES), I32),
        ],
        compiler_params=_cparams(("arbitrary",), 48),
        name="mix_router_sort",
    )(x2, a, c, wo_bf16, g0, b0, g1, b1, rwt, rb)


def _max_tiles(nb):
    return (nb * CHUNKS_PER_BLOCK + CHUNKS_PER_TILE - 1) // CHUNKS_PER_TILE + N_EXPERTS + 1


def _build_schedule(cc):
    nb = cc.shape[0]
    cpt = CHUNKS_PER_TILE
    ne1 = N_EXPERTS + 1
    tail = CHUNKS_PER_BLOCK - jnp.sum(cc, axis=1)
    cce = jnp.concatenate([cc, tail[:, None]], axis=1)
    seg_start = jnp.cumsum(cce, axis=1) - cce
    tot = jnp.sum(cce, axis=0)
    tiles_e = (tot + cpt - 1) // cpt
    tile_end = jnp.cumsum(tiles_e)
    tile_start = tile_end - tiles_e
    n_tiles = tile_end[-1]
    max_tiles = _max_tiles(nb)

    def lookup(passed, table):
        return table[0] + jnp.sum(jnp.where(passed, (table[1:] - table[:-1])[None, :], 0), axis=1)

    t = jnp.arange(max_tiles, dtype=I32)
    passed = tile_end[None, :] <= t[:, None]
    last_e = jnp.max(jnp.where(tiles_e > 0, jnp.arange(ne1, dtype=I32), 0))
    te = jnp.where(t < n_tiles, jnp.minimum(jnp.sum(passed, axis=1), ne1 - 1), last_e).astype(I32)
    chunk_end = jnp.concatenate([tot + tile_start * cpt, jnp.zeros((1,), I32)])
    nv = jnp.where(t < n_tiles, jnp.clip(lookup(passed, chunk_end) - t * cpt, 0, cpt), 0).astype(I32)

    pad_len = tiles_e * cpt - tot
    seg_len = jnp.concatenate([cce.T, pad_len[:, None]], axis=1).reshape(-1)
    seg_end = jnp.cumsum(seg_len)
    base = jnp.arange(nb, dtype=I32)[None, :] * CHUNKS_PER_BLOCK + seg_start.T
    is_real = jnp.concatenate([jnp.ones((ne1, nb), I32), jnp.zeros((ne1, 1), I32)], axis=1).reshape(-1)
    offset = jnp.concatenate([base, jnp.zeros((ne1, 1), I32)], axis=1).reshape(-1) - (seg_end - seg_len) * is_real
    p = jnp.arange(max_tiles * cpt, dtype=I32)
    passed = seg_end[None, :] <= p[:, None]
    zero = jnp.zeros((1,), I32)
    src = (lookup(passed, jnp.concatenate([is_real, zero])) * p
           + lookup(passed, jnp.concatenate([offset, zero]))).astype(I32)
    return te, nv, src, jnp.reshape(n_tiles, (1,)).astype(I32)


def _moe_body(te_ref, nv_ref, src_ref, nt_ref, xs_hbm, wgu_ref, bgu_ref, wd_ref, bd_ref, y_hbm,
              xbuf, ybuf, act, wgu_b, wd_b, in_sem, out_sem):
    t = pl.program_id(0)
    nt = nt_ref[0]
    cpt = CHUNKS_PER_TILE
    f = wd_ref.shape[0]

    def rows(c):
        return pl.ds(pl.multiple_of(c * BF16_ROWS, BF16_ROWS), BF16_ROWS)

    def in_copy(tile, slot, c):
        return pltpu.make_async_copy(xs_hbm.at[src_ref[tile * cpt + c]], xbuf.at[slot, rows(c)], in_sem.at[slot])

    def out_copy(tile, slot, c):
        return pltpu.make_async_copy(ybuf.at[slot, rows(c)], y_hbm.at[src_ref[tile * cpt + c]], out_sem.at[slot])

    def start_gather(tile, slot):
        lax.fori_loop(0, cpt, lambda c, _: (in_copy(tile, slot, c).start(), 0)[1], 0)

    def wait_gather(tile, slot):
        lax.fori_loop(0, cpt, lambda c, _: (in_copy(tile, slot, c).wait(), 0)[1], 0)

    def start_scatter(tile, slot):
        lax.fori_loop(0, nv_ref[tile], lambda c, _: (out_copy(tile, slot, c).start(), 0)[1], 0)

    def wait_scatter(tile, slot):
        lax.fori_loop(0, nv_ref[tile], lambda c, _: (out_copy(tile, slot, c).wait(), 0)[1], 0)

    @pl.when(t == 0)
    def _():
        start_gather(0, 0)

    @pl.when(t + 1 < nt)
    def _():
        start_gather(t + 1, (t + 1) % 2)

    @pl.when(t < nt)
    def _():
        slot = t % 2
        e = te_ref[t]
        wait_gather(t, slot)

        @pl.when((e < N_EXPERTS) & ((t == 0) | (te_ref[jnp.maximum(t - 1, 0)] != e)))
        def _():
            wgu_b[...] = wgu_ref[...].astype(BF16)
            wd_b[...] = wd_ref[...].astype(BF16)

        @pl.when(t >= 2)
        def _():
            wait_scatter(t - 2, slot)

        @pl.when(e < N_EXPERTS)
        def _():
            x = xbuf[slot]
            nc = 512
            for j in range(f // nc):
                gsl = slice(j * nc, (j + 1) * nc)
                usl = slice(f + j * nc, f + (j + 1) * nc)
                g = jnp.dot(x, wgu_b[:, gsl], preferred_element_type=F32) + bgu_ref[:, gsl]
                u = jnp.dot(x, wgu_b[:, usl], preferred_element_type=F32) + bgu_ref[:, usl]
                g = jnp.minimum(g, SWIGLU_LIMIT)
                u = jnp.clip(u, -SWIGLU_LIMIT, SWIGLU_LIMIT)
                act[:, gsl] = (g * _sigmoid(SWIGLU_ALPHA * g) * (u + 1.0)).astype(BF16)
            a = act[...]
            for j in range(wd_ref.shape[1] // nc):
                sl = slice(j * nc, (j + 1) * nc)
                y = jnp.dot(a, wd_b[:, sl], preferred_element_type=F32) + bd_ref[:, sl]
                ybuf[slot, :, sl] = y.astype(BF16)

        @pl.when(e >= N_EXPERTS)
        def _():
            ybuf[slot] = jnp.zeros(ybuf.shape[1:], BF16)

        start_scatter(t, slot)

        @pl.when(t == nt - 1)
        def _():
            wait_scatter(t, slot)

            @pl.when(t >= 1)
            def _():
                wait_scatter(t - 1, 1 - slot)


def _moe(xs_chunks, te, nv, src, nt, w_gate_up, b_gate_up, w_down, b_down):
    nch, _, d = xs_chunks.shape
    f = w_down.shape[1]
    max_tiles = te.shape[0]
    wsel = lambda t, te_r, nv_r, src_r, nt_r: (jnp.minimum(te_r[t], N_EXPERTS - 1), 0, 0)
    grid_spec = pltpu.PrefetchScalarGridSpec(
        num_scalar_prefetch=4,
        grid=(max_tiles,),
        in_specs=[
            pl.BlockSpec(memory_space=pl.ANY),
            pl.BlockSpec((None, d, 2 * f), wsel),
            pl.BlockSpec((None, 1, 2 * f), wsel),
            pl.BlockSpec((None, f, d), wsel),
            pl.BlockSpec((None, 1, d), wsel),
        ],
        out_specs=pl.BlockSpec(memory_space=pl.ANY),
        scratch_shapes=[
            pltpu.VMEM((2, MOE_TILE, d), BF16),
            pltpu.VMEM((2, MOE_TILE, d), BF16),
            pltpu.VMEM((MOE_TILE, f), BF16),
            pltpu.VMEM((d, 2 * f), BF16),
            pltpu.VMEM((f, d), BF16),
            pltpu.SemaphoreType.DMA((2,)),
            pltpu.SemaphoreType.DMA((2,)),
        ],
    )
    return pl.pallas_call(
        _moe_body,
        grid_spec=grid_spec,
        out_shape=jax.ShapeDtypeStruct((nch, BF16_ROWS, d), BF16),
        compiler_params=_cparams(("arbitrary",), 56),
        name="moe_grouped",
    )(te, nv, src, nt, xs_chunks, w_gate_up, b_gate_up, w_down, b_down)


def _combine_body(y_ref, pos_ref, gate_ref, x1_ref, g_ref, b_ref, o_ref):
    tb = x1_ref.shape[0]
    pos = pos_ref[...]
    gate = gate_ref[...]
    acc = jnp.zeros(x1_ref.shape, F32)
    cols = 512
    for rc in range(y_ref.shape[0] // cols):
        r = lax.broadcasted_iota(I32, (tb, cols), 1) + rc * cols
        w = jnp.zeros((tb, cols), F32)
        for k in range(TOP_K):
            w = w + jnp.where(r == pos[:, k:k + 1], gate[:, k:k + 1], 0.0)
        acc = acc + jnp.dot(w.astype(BF16), y_ref[rc * cols:(rc + 1) * cols, :], preferred_element_type=F32)
    o_ref[...] = _layer_norm(DEEPNORM_ALPHA * x1_ref[...] + acc, g_ref[...], b_ref[...])


def _combine(y_local, pos, gates, x1, g, b):
    n, d = x1.shape
    tb = TOKEN_TILE
    nb = n // tb
    row = lambda: pl.BlockSpec((1, d), lambda i: (0, 0))
    return pl.pallas_call(
        _combine_body,
        grid=(nb,),
        in_specs=[
            pl.BlockSpec((None, LOCAL_ROWS, d), lambda i: (i, 0, 0)),
            pl.BlockSpec((None, tb, TOP_K), lambda i: (i, 0, 0)),
            pl.BlockSpec((None, tb, TOP_K), lambda i: (i, 0, 0)),
            pl.BlockSpec((tb, d), lambda i: (i, 0)),
            row(), row(),
        ],
        out_specs=pl.BlockSpec((tb, d), lambda i: (i, 0)),
        out_shape=jax.ShapeDtypeStruct((n, d), F32),
        compiler_params=_cparams(("arbitrary",), 40),
        name="combine_ln",
    )(y_local, pos, gates, x1, g, b)


def kernel(x, emb_ln_g, emb_ln_b, w_in, lambda_q1, lambda_k1, lambda_q2, lambda_k2, subln_g, conv_w, conv_b,
           conv_ln_g, conv_ln_b, w_out, ln1_g, ln1_b, router_w, router_b, w_gate_up, b_gate_up, w_down, b_down,
           ln2_g, ln2_b):
    batch, seq, d = x.shape
    n = batch * seq
    assert n % TOKEN_TILE == 0 and seq % ATTN_KV_BLOCK == 0 and seq % CONV_TILE == 0
    assert w_in.shape[0] == DEPTH
    x2 = x.reshape(n, d)
    row = lambda v: v.reshape(1, -1).astype(F32)
    slopes = jnp.array([(2.0 ** (-8.0 / N_HEADS)) ** (i + 1) for i in range(N_HEADS)], dtype=F32)

    h = _inproj(x2, row(emb_ln_g), row(emb_ln_b), w_in[0].astype(BF16))
    a = _attention(h, slopes, row(lambda_q1[0]), row(lambda_k1[0]), row(lambda_q2[0]), row(lambda_k2[0]),
                   row(subln_g[0]), batch=batch, seq=seq)
    c = _conv_module(h, conv_w[0], row(conv_b[0]), row(conv_ln_g[0]), row(conv_ln_b[0]), batch=batch, seq=seq)
    x1, xs_local, pos_t, gates_t, cc = _mix(
        x2, a, c, w_out[0].astype(BF16), row(emb_ln_g), row(emb_ln_b), row(ln1_g[0]), row(ln1_b[0]),
        router_w[0].T, router_b[0].reshape(N_EXPERTS, 1))

    te, nv, src, nt = _build_schedule(cc[:, :, 0])
    nb = n // TOKEN_TILE
    y_chunks = _moe(xs_local.reshape(nb * CHUNKS_PER_BLOCK, BF16_ROWS, d), te, nv, src, nt,
                    w_gate_up[0], b_gate_up[0].reshape(N_EXPERTS, 1, -1), w_down[0],
                    b_down[0].reshape(N_EXPERTS, 1, -1))
    y_local = y_chunks.reshape(nb, LOCAL_ROWS, d)
    out = _combine(y_local, jnp.transpose(pos_t, (0, 2, 1)), jnp.transpose(gates_t, (0, 2, 1)), x1,
                   row(ln2_g[0]), row(ln2_b[0]))
    return out.reshape(batch, seq, d)
```

```python
import functools

import jax
import jax.numpy as jnp
from jax import lax
from jax.experimental import pallas as pl
from jax.experimental.pallas import tpu as pltpu

F32 = jnp.float32
BF16 = jnp.bfloat16
I32 = jnp.int32

N_HEADS = 4
HEAD_DIM = 64
V_DIM = 128
ATTN_WIDTH = N_HEADS * V_DIM
CONV_WIDTH = 512
CONV_KERNEL = 31
N_EXPERTS = 32
TOP_K = 4
SWIGLU_LIMIT = 7.0
SWIGLU_ALPHA = 1.702
LN_EPS = 1e-5
DEPTH = 1
DEEPNORM_ALPHA = (2.0 * DEPTH) ** 0.25
LAMBDA_INIT = 0.2
LOG2_E = 1.4426950408889634
Q_COLS = N_HEADS * 2 * HEAD_DIM
Q_SCALE = HEAD_DIM ** -0.5 * LOG2_E

LANES = 128
BF16_ROWS = 16

TOKEN_TILE = 512
ATTN_KV_BLOCK = 256
ATTN_SUB = 128
CONV_TILE = 64
CONV_ROW_STRIDE = 4
CONV_GLU_TILE = 128
MOE_TILE = 512
CHUNKS_PER_TILE = MOE_TILE // BF16_ROWS
LOCAL_ROWS = TOP_K * TOKEN_TILE + N_EXPERTS * BF16_ROWS
CHUNKS_PER_BLOCK = LOCAL_ROWS // BF16_ROWS


def _cparams(semantics, vmem_mib):
    return pltpu.CompilerParams(dimension_semantics=semantics, vmem_limit_bytes=vmem_mib * 1024 * 1024)


def _layer_norm(x, g, b):
    mu = jnp.mean(x, axis=-1, keepdims=True)
    xc = x - mu
    var = jnp.mean(xc * xc, axis=-1, keepdims=True)
    return xc * lax.rsqrt(var + LN_EPS) * g + b


def _sigmoid(x):
    return 1.0 / (1.0 + jnp.exp(-x))


def _inproj_body(x_ref, g_ref, b_ref, w_ref, h_ref, *, n_chunk):
    xn = _layer_norm(x_ref[...], g_ref[...], b_ref[...]).astype(BF16)
    for j in range(w_ref.shape[1] // n_chunk):
        sl = slice(j * n_chunk, (j + 1) * n_chunk)
        hj = jnp.dot(xn, w_ref[:, sl], preferred_element_type=F32)
        if (j + 1) * n_chunk <= Q_COLS:
            hj = hj * Q_SCALE
        h_ref[:, sl] = hj.astype(BF16)


def _inproj(x2, g, b, w_bf16):
    n, d = x2.shape
    cols = w_bf16.shape[1]
    tm = TOKEN_TILE
    return pl.pallas_call(
        functools.partial(_inproj_body, n_chunk=512),
        grid=(n // tm,),
        in_specs=[
            pl.BlockSpec((tm, d), lambda i: (i, 0)),
            pl.BlockSpec((1, d), lambda i: (0, 0)),
            pl.BlockSpec((1, d), lambda i: (0, 0)),
            pl.BlockSpec((d, cols), lambda i: (0, 0)),
        ],
        out_specs=pl.BlockSpec((tm, cols), lambda i: (i, 0)),
        out_shape=jax.ShapeDtypeStruct((n, cols), BF16),
        compiler_params=_cparams(("arbitrary",), 40),
        name="inproj",
    )(x2, g, b, w_bf16)


def _attn_body(slopes_ref, lq1_ref, lk1_ref, lq2_ref, lk2_ref, g_ref, q_ref, k_ref, v_ref, o_ref,
               q2_ref, m_ref, l_ref, acc_ref, *, seq):
    head = pl.program_id(1)
    slope = slopes_ref[head] * LOG2_E
    lam =(jnp.exp(jnp.sum(lq1_ref[...] * lk1_ref[...], axis=-1, keepdims=True))
           - jnp.exp(jnp.sum(lq2_ref[...] * lk2_ref[...], axis=-1, keepdims=True))
           + LAMBDA_INIT)

    q = q_ref[...]
    lane = lax.broadcasted_iota(I32, q.shape, 1)
    zero = jnp.zeros_like(q)
    q2_ref[0:seq, :] = jnp.where(lane < HEAD_DIM, q, zero)
    q2_ref[seq:, :] = jnp.where(lane >= HEAD_DIM, q, zero)
    m_ref[...] = jnp.full(m_ref.shape, -jnp.inf, F32)
    l_ref[...] = jnp.zeros(l_ref.shape, F32)
    acc_ref[...] = jnp.zeros(acc_ref.shape, F32)
    kb_rows = ATTN_KV_BLOCK
    sub = ATTN_SUB
    ones = jnp.ones((kb_rows, V_DIM), BF16)
    col = lax.broadcasted_iota(I32, (1, kb_rows), 1)

    for d in range(seq // kb_rows):
        k0 = d * kb_rows
        kb = k_ref[k0:k0 + kb_rows, :]
        vext = jnp.concatenate([v_ref[k0:k0 + kb_rows, :], ones], axis=1)
        for half in range(2):
            for i in range(k0 // sub, seq // sub):
                r0 = i * sub
                rows = slice(half * seq + r0, half * seq + r0 + sub)
                ncol = min(kb_rows, -(-(r0 + sub - k0) // LANES) * LANES)
                bias = (col[:, 0:ncol] + (k0 - r0)).astype(F32) * slope
                s = lax.dot_general(q2_ref[rows, :], kb[0:ncol, :], (((1,), (1,)), ((), ())),
                                    preferred_element_type=F32) + bias
                if k0 + ncol - 1 > r0:
                    row = lax.broadcasted_iota(I32, s.shape, 0) + (r0 - k0)
                    s = jnp.where(lax.broadcasted_iota(I32, s.shape, 1) <= row, s, -jnp.inf)
                m_prev = m_ref[rows, :]
                m_new = jnp.maximum(m_prev, jnp.max(s, axis=1, keepdims=True))
                alpha = jnp.exp2(m_prev - m_new)
                p = jnp.exp2(s - jnp.concatenate([m_new] * (ncol // LANES), axis=1))
                pv = jnp.dot(p.astype(BF16), vext[0:ncol, :], preferred_element_type=F32)
                acc_ref[rows, :] = alpha * acc_ref[rows, :] + pv[:, :V_DIM]
                l_ref[rows, :] = alpha * l_ref[rows, :] + pv[:, V_DIM:]
                m_ref[rows, :] = m_new

    o = acc_ref[...] / l_ref[...]
    od = o[:seq] - lam * o[seq:]
    ms = jnp.mean(od * od, axis=-1, keepdims=True)
    out = od * lax.rsqrt(ms + LN_EPS) * g_ref[...] * (1.0 - LAMBDA_INIT)
    o_ref[...] = out.astype(BF16)


def _attention(h, slopes, lq1, lk1, lq2, lk2, subln_g, *, batch, seq):
    n = h.shape[0]
    vec = lambda: pl.BlockSpec((1, HEAD_DIM), lambda b, hd, *_: (0, 0))
    grid_spec = pltpu.PrefetchScalarGridSpec(
        num_scalar_prefetch=1,
        grid=(batch, N_HEADS),
        in_specs=[
            vec(), vec(), vec(), vec(),
            pl.BlockSpec((1, V_DIM), lambda b, hd, *_: (0, 0)),
            pl.BlockSpec((seq, V_DIM), lambda b, hd, *_: (b, hd)),
            pl.BlockSpec((seq, V_DIM), lambda b, hd, *_: (b, N_HEADS + hd)),
            pl.BlockSpec((seq, V_DIM), lambda b, hd, *_: (b, 2 * N_HEADS + hd)),
        ],
        out_specs=pl.BlockSpec((seq, V_DIM), lambda b, hd, *_: (b, hd)),
        scratch_shapes=[
            pltpu.VMEM((2 * seq, V_DIM), BF16),
            pltpu.VMEM((2 * seq, LANES), F32),
            pltpu.VMEM((2 * seq, LANES), F32),
            pltpu.VMEM((2 * seq, V_DIM), F32),
        ],
    )
    return pl.pallas_call(
        functools.partial(_attn_body, seq=seq),
        grid_spec=grid_spec,
        out_shape=jax.ShapeDtypeStruct((n, ATTN_WIDTH), BF16),
        compiler_params=_cparams(("arbitrary", "arbitrary"), 32),
        name="diff_attn",
    )(slopes, lq1, lk1, lq2, lk2, subln_g, h, h, h)


CONV_PAD = 32


def _conv_body(a_ref, g_ref, w_ref, cb_ref, lg_ref, lb_ref, o_ref, glu_ref, conv_ref, *, seq):
    n_slab = CONV_WIDTH // LANES
    for c in range(n_slab):
        glu_ref[c, 0:CONV_PAD, :] = jnp.zeros((CONV_PAD, LANES), F32)

    def glu(i, _):
        st = pl.multiple_of(i * CONV_GLU_TILE, CONV_GLU_TILE)
        a = a_ref[pl.ds(st, CONV_GLU_TILE), :].astype(F32)
        g = g_ref[pl.ds(st, CONV_GLU_TILE), :].astype(F32)
        val = a * _sigmoid(g)
        for c in range(n_slab):
            glu_ref[c, pl.ds(CONV_PAD + st, CONV_GLU_TILE), :] = val[:, c * LANES:(c + 1) * LANES]
        return 0

    lax.fori_loop(0, seq // CONV_GLU_TILE, glu, 0)

    span = 8 * CONV_ROW_STRIDE
    for c in range(n_slab):
        lanes = slice(c * LANES, (c + 1) * LANES)
        taps = [jnp.broadcast_to(w_ref[j:j + 1, lanes], (8, LANES)) for j in range(CONV_KERNEL)]
        bias = jnp.broadcast_to(cb_ref[:, lanes], (8, LANES))

        def conv(i, _):
            st = i * CONV_TILE
            for grp in range(CONV_TILE // span):
                for p in range(CONV_ROW_STRIDE):
                    t0 = st + grp * span + p
                    acc = bias
                    for j in range(CONV_KERNEL):
                        src = pl.ds(t0 + (CONV_PAD - (CONV_KERNEL - 1) + j), 8, stride=CONV_ROW_STRIDE)
                        acc = acc + glu_ref[c, src, :] * taps[j]
                    conv_ref[c, pl.ds(t0, 8, stride=CONV_ROW_STRIDE), :] = acc
            return 0

        lax.fori_loop(0, seq // CONV_TILE, conv, 0)

    def norm(i, _):
        st = pl.multiple_of(i * CONV_GLU_TILE, CONV_GLU_TILE)
        x = jnp.concatenate([conv_ref[c, pl.ds(st, CONV_GLU_TILE), :] for c in range(n_slab)], axis=1)
        y = _layer_norm(x, lg_ref[...], lb_ref[...])
        o_ref[pl.ds(st, CONV_GLU_TILE), :] = (y * _sigmoid(y)).astype(BF16)
        return 0

    lax.fori_loop(0, seq // CONV_GLU_TILE, norm, 0)


def _conv_module(h, conv_w, conv_b, ln_g, ln_b, *, batch, seq):
    n = h.shape[0]
    u_col = 3 * ATTN_WIDTH // CONV_WIDTH
    row = lambda: pl.BlockSpec((1, CONV_WIDTH), lambda b: (0, 0))
    return pl.pallas_call(
        functools.partial(_conv_body, seq=seq),
        grid=(batch,),
        in_specs=[
            pl.BlockSpec((seq, CONV_WIDTH), lambda b: (b, u_col)),
            pl.BlockSpec((seq, CONV_WIDTH), lambda b: (b, u_col + 1)),
            pl.BlockSpec((CONV_KERNEL, CONV_WIDTH), lambda b: (0, 0)),
            row(), row(), row(),
        ],
        out_specs=pl.BlockSpec((seq, CONV_WIDTH), lambda b: (b, 0)),
        out_shape=jax.ShapeDtypeStruct((n, CONV_WIDTH), BF16),
        scratch_shapes=[
            pltpu.VMEM((CONV_WIDTH // LANES, CONV_PAD + seq, LANES), F32),
            pltpu.VMEM((CONV_WIDTH // LANES, seq, LANES), F32),
        ],
        compiler_params=_cparams(("arbitrary",), 32),
        name="conformer_conv",
    )(h, h, conv_w, conv_b, ln_g, ln_b)


def _mix_body(x_ref, a_ref, c_ref, wo_ref, g0_ref, b0_ref, g1_ref, b1_ref, rwt_ref, rb_ref,
              x1_ref, xs_ref, pos_ref, gate_ref, cc_ref):
    tb = x_ref.shape[0]
    xn = _layer_norm(x_ref[...], g0_ref[...], b0_ref[...])
    mix = (jnp.dot(a_ref[...], wo_ref[0:ATTN_WIDTH, :], preferred_element_type=F32)
           + jnp.dot(c_ref[...], wo_ref[ATTN_WIDTH:, :], preferred_element_type=F32))
    x1 = _layer_norm(DEEPNORM_ALPHA * xn + mix, g1_ref[...], b1_ref[...])
    x1_ref[...] = x1
    x1b = x1.astype(BF16)

    logits = lax.dot_general(rwt_ref[...], x1, (((1,), (1,)), ((), ())),
                             precision=lax.Precision.HIGHEST, preferred_element_type=F32) + rb_ref[...]
    eidx = lax.broadcasted_iota(I32, logits.shape, 0)
    work = logits
    vals, hots = [], []
    for _ in range(TOP_K):
        mx = jnp.max(work, axis=0, keepdims=True)
        sel = jnp.min(jnp.where(work == mx, eidx, N_EXPERTS), axis=0, keepdims=True)
        hot = eidx == sel
        vals.append(mx)
        hots.append(hot)
        work = jnp.where(hot, -jnp.inf, work)
    ex = [jnp.exp(v - vals[0]) for v in vals]
    den = ex[0] + ex[1] + ex[2] + ex[3]
    for k in range(TOP_K):
        gate_ref[k:k + 1, :] = ex[k] / den

    member = jnp.zeros(logits.shape, F32)
    for hot in hots:
        member = member + jnp.where(hot, 1.0, 0.0)
    before = (lax.broadcasted_iota(I32, (tb, tb), 0) < lax.broadcasted_iota(I32, (tb, tb), 1))
    rank = jnp.dot(member.astype(BF16), jnp.where(before, 1.0, 0.0).astype(BF16), preferred_element_type=F32)
    count = jnp.sum(member, axis=1, keepdims=True)
    chunks = jnp.floor((count + (BF16_ROWS - 1.0)) * (1.0 / BF16_ROWS))
    cc_ref[...] = jnp.broadcast_to(chunks, cc_ref.shape).astype(I32)
    lower = (lax.broadcasted_iota(I32, (N_EXPERTS, N_EXPERTS), 1) < lax.broadcasted_iota(I32, (N_EXPERTS, N_EXPERTS), 0))
    seg_start = jnp.dot(jnp.where(lower, 1.0, 0.0).astype(BF16),
                        jnp.broadcast_to(chunks, (N_EXPERTS, LANES)).astype(BF16),
                        preferred_element_type=F32)[:, 0:1]
    slot = seg_start * float(BF16_ROWS) + rank
    pos = [jnp.sum(jnp.where(hot, slot, 0.0), axis=0, keepdims=True).astype(I32) for hot in hots]
    for k in range(TOP_K):
        pos_ref[k:k + 1, :] = pos[k]

    rows = 256
    one, nil = jnp.ones((), BF16), jnp.zeros((), BF16)
    pos16 = [p.astype(jnp.int16) for p in pos]
    for rc in range(xs_ref.shape[0] // rows):
        r = lax.broadcasted_iota(jnp.int16, (rows, tb), 0) + jnp.int16(rc * rows)
        hit = (r == pos16[0]) | (r == pos16[1]) | (r == pos16[2]) | (r == pos16[3])
        sel = jnp.where(hit, one, nil)
        xs_ref[rc * rows:(rc + 1) * rows, :] = jnp.dot(sel, x1b, preferred_element_type=F32).astype(BF16)


def _mix(x2, a, c, wo_bf16, g0, b0, g1, b1, rwt, rb):
    n, d = x2.shape
    tb = TOKEN_TILE
    nb = n // tb
    row = lambda: pl.BlockSpec((1, d), lambda i: (0, 0))
    return pl.pallas_call(
        _mix_body,
        grid=(nb,),
        in_specs=[
            pl.BlockSpec((tb, d), lambda i: (i, 0)),
            pl.BlockSpec((tb, ATTN_WIDTH), lambda i: (i, 0)),
            pl.BlockSpec((tb, CONV_WIDTH), lambda i: (i, 0)),
            pl.BlockSpec((d, d), lambda i: (0, 0)),
            row(), row(), row(), row(),
            pl.BlockSpec((N_EXPERTS, d), lambda i: (0, 0)),
            pl.BlockSpec((N_EXPERTS, 1), lambda i: (0, 0)),
        ],
        out_specs=[
            pl.BlockSpec((tb, d), lambda i: (i, 0)),
            pl.BlockSpec((None, LOCAL_ROWS, d), lambda i: (i, 0, 0)),
            pl.BlockSpec((None, TOP_K, tb), lambda i: (i, 0, 0)),
            pl.BlockSpec((None, TOP_K, tb), lambda i: (i, 0, 0)),
            pl.BlockSpec((None, N_EXPERTS, LANES), lambda i: (i, 0, 0)),
        ],
        out_shape=[
            jax.ShapeDtypeStruct((n, d), F32),
            jax.ShapeDtypeStruct((nb, LOCAL_ROWS, d), BF16),
            jax.ShapeDtypeStruct((nb, TOP_K, tb), I32),
            jax.ShapeDtypeStruct((nb, TOP_K, tb), F32),
            jax.ShapeDtypeStruct((nb, N_EXPERTS, LANES), I32),
        ],
        compiler_params=_cparams(("arbitrary",), 48),
        name="mix_router_sort",
    )(x2, a, c, wo_bf16, g0, b0, g1, b1, rwt, rb)


def _max_tiles(nb):
    return (nb * CHUNKS_PER_BLOCK + CHUNKS_PER_TILE - 1) // CHUNKS_PER_TILE + N_EXPERTS + 1


def _build_schedule(cc):
    nb = cc.shape[0]
    cpt = CHUNKS_PER_TILE
    ne1 = N_EXPERTS + 1
    tail = CHUNKS_PER_BLOCK - jnp.sum(cc, axis=1)
    cce = jnp.concatenate([cc, tail[:, None]], axis=1)
    seg_start = jnp.cumsum(cce, axis=1) - cce
    tot = jnp.sum(cce, axis=0)
    tiles_e = (tot + cpt - 1) // cpt
    tile_end = jnp.cumsum(tiles_e)
    max_tiles = _max_tiles(nb)

    def lookup(passed, table):
        return table[0] + jnp.sum(jnp.where(passed, (table[1:] - table[:-1])[None, :], 0), axis=1)

    t = jnp.arange(max_tiles, dtype=I32)
    passed = tile_end[None, :] <= t[:, None]
    te = jnp.minimum(jnp.sum(passed, axis=1), ne1 - 1).astype(I32)

    pad_len = tiles_e * cpt - tot
    seg_len = jnp.concatenate([cce.T, pad_len[:, None]], axis=1).reshape(-1)
    seg_end = jnp.cumsum(seg_len)
    block = jnp.arange(nb, dtype=I32)[None, :]
    is_real = jnp.concatenate([jnp.ones((ne1, nb), I32), jnp.zeros((ne1, 1), I32)], axis=1).reshape(-1)
    seg_first = (seg_end - seg_len) * is_real
    zero = jnp.zeros((1,), I32)

    base = block * CHUNKS_PER_BLOCK + seg_start.T
    first = jnp.concatenate([jnp.concatenate([base, jnp.zeros((ne1, 1), I32)], axis=1).reshape(-1) - seg_first,
                             zero])
    p = jnp.arange(max_tiles * cpt, dtype=I32)
    passed = seg_end[None, :] <= p[:, None]
    real = lookup(passed, jnp.concatenate([is_real, zero]))
    src = (real * p + lookup(passed, first)).astype(I32)
    dump0 = nb * CHUNKS_PER_BLOCK
    dst = jnp.where(real > 0, src, dump0 + ((p // cpt) % 2) * cpt + p % cpt).astype(I32)
    dummy = dump0 + cpt + jnp.arange(cpt, dtype=I32)
    return te, src, jnp.concatenate([dummy, dst])


def _moe_body(te_ref, src_ref, dst_ref, xs_hbm, wgu_ref, bgu_ref, wd_ref, bd_ref, y_hbm,
              xbuf, ybuf, act, wgu_b, wd_b, in_sem, out_sem):
    t = pl.program_id(0)
    last = pl.num_programs(0) - 1
    cpt = CHUNKS_PER_TILE
    f = wd_ref.shape[0]
    slot = t % 2
    other = 1 - slot
    e = te_ref[t]

    def rows(c):
        return pl.ds(c * BF16_ROWS, BF16_ROWS)

    def in_copy(tile, buf, c):
        return pltpu.make_async_copy(xs_hbm.at[src_ref[tile * cpt + c]], xbuf.at[buf, rows(c)], in_sem.at[buf])

    def out_copy(tile, buf, c):
        return pltpu.make_async_copy(ybuf.at[buf, rows(c)], y_hbm.at[dst_ref[(tile + 1) * cpt + c]],
                                     out_sem.at[buf])

    def start_all(copy, tile, buf):
        for c in range(cpt):
            copy(tile, buf, c).start()

    def wait_all(copy, tile, buf):
        for c in range(cpt):
            copy(tile, buf, c).wait()

    def neighbour_copies():
        nxt = jnp.minimum(t + 1, last)
        return ([in_copy(nxt, other, c) for c in range(cpt)] + [out_copy(t - 1, other, c) for c in range(cpt)])

    @pl.when(t == 0)
    def _():
        start_all(in_copy, 0, 0)
        ybuf[1] = jnp.zeros(ybuf.shape[1:], BF16)
        even_dump = [pltpu.make_async_copy(ybuf.at[1, rows(c)], y_hbm.at[y_hbm.shape[0] - 2 * cpt + c],
                                           out_sem.at[0]) for c in range(cpt)]
        for copy in even_dump:
            copy.start()
        for copy in even_dump:
            copy.wait()

    @pl.when(t >= 1)
    def _():
        wait_all(out_copy, t - 2, slot)

    wait_all(in_copy, t, slot)

    @pl.when((e < N_EXPERTS) & ((t == 0) | (te_ref[jnp.maximum(t - 1, 0)] != e)))
    def _():
        wgu_b[...] = wgu_ref[...].astype(BF16)
        wd_b[...] = wd_ref[...].astype(BF16)

    @pl.when(e < N_EXPERTS)
    def _():
        pending = neighbour_copies()
        nc = 512
        n_up, n_down = f // nc, wd_ref.shape[1] // nc
        per_gap = -(-len(pending) // (3 * n_up + n_down))

        def start_some():
            for _ in range(min(per_gap, len(pending))):
                pending.pop(0).start()

        x = xbuf[slot]
        for j in range(n_up):
            gsl = slice(j * nc, (j + 1) * nc)
            usl = slice(f + j * nc, f + (j + 1) * nc)
            g = jnp.dot(x, wgu_b[:, gsl], preferred_element_type=F32) + bgu_ref[:, gsl]
            start_some()
            u = jnp.dot(x, wgu_b[:, usl], preferred_element_type=F32) + bgu_ref[:, usl]
            start_some()
            g = jnp.minimum(g, SWIGLU_LIMIT)
            u = jnp.clip(u, -SWIGLU_LIMIT, SWIGLU_LIMIT)
            act[:, gsl] = (g * _sigmoid(SWIGLU_ALPHA * g) * (u + 1.0)).astype(BF16)
            start_some()
        a = act[...]
        for j in range(n_down):
            sl = slice(j * nc, (j + 1) * nc)
            y = jnp.dot(a, wd_b[:, sl], preferred_element_type=F32) + bd_ref[:, sl]
            ybuf[slot, :, sl] = y.astype(BF16)
            start_some()
        assert not pending

    @pl.when(e >= N_EXPERTS)
    def _():
        for copy in neighbour_copies():
            copy.start()
        ybuf[slot] = jnp.zeros(ybuf.shape[1:], BF16)

    @pl.when(t == last)
    def _():
        start_all(out_copy, t, slot)
        wait_all(out_copy, t - 1, other)
        wait_all(out_copy, t, slot)
        wait_all(in_copy, t, other)


def _moe(xs_chunks, te, src, dst, w_gate_up, b_gate_up, w_down, b_down):
    nch, _, d = xs_chunks.shape
    f = w_down.shape[1]
    max_tiles = te.shape[0]
    wsel = lambda t, te_r, src_r, dst_r: (jnp.minimum(te_r[t], N_EXPERTS - 1), 0, 0)
    grid_spec = pltpu.PrefetchScalarGridSpec(
        num_scalar_prefetch=3,
        grid=(max_tiles,),
        in_specs=[
            pl.BlockSpec(memory_space=pl.ANY),
            pl.BlockSpec((None, d, 2 * f), wsel),
            pl.BlockSpec((None, 1, 2 * f), wsel),
            pl.BlockSpec((None, f, d), wsel),
            pl.BlockSpec((None, 1, d), wsel),
        ],
        out_specs=pl.BlockSpec(memory_space=pl.ANY),
        scratch_shapes=[
            pltpu.VMEM((2, MOE_TILE, d), BF16),
            pltpu.VMEM((2, MOE_TILE, d), BF16),
            pltpu.VMEM((MOE_TILE, f), BF16),
            pltpu.VMEM((d, 2 * f), BF16),
            pltpu.VMEM((f, d), BF16),
            pltpu.SemaphoreType.DMA((2,)),
            pltpu.SemaphoreType.DMA((2,)),
        ],
    )
    return pl.pallas_call(
        _moe_body,
        grid_spec=grid_spec,
        out_shape=jax.ShapeDtypeStruct((nch + 2 * CHUNKS_PER_TILE, BF16_ROWS, d), BF16),
        compiler_params=_cparams(("arbitrary",), 56),
        name="moe_grouped",
    )(te, src, dst, xs_chunks, w_gate_up, b_gate_up, w_down, b_down)


def _combine_body(y_ref, pos_ref, gate_ref, x1_ref, g_ref, b_ref, o_ref):
    tb = x1_ref.shape[0]
    pos = pos_ref[...]
    gate = gate_ref[...]
    acc = jnp.zeros(x1_ref.shape, F32)
    cols = 512
    for rc in range(y_ref.shape[0] // cols):
        r = lax.broadcasted_iota(I32, (tb, cols), 1) + rc * cols
        w = jnp.zeros((tb, cols), F32)
        for k in range(TOP_K):
            w = w + jnp.where(r == pos[:, k:k + 1], gate[:, k:k + 1], 0.0)
        acc = acc + jnp.dot(w.astype(BF16), y_ref[rc * cols:(rc + 1) * cols, :], preferred_element_type=F32)
    o_ref[...] = _layer_norm(DEEPNORM_ALPHA * x1_ref[...] + acc, g_ref[...], b_ref[...])


def _combine(y_local, pos, gates, x1, g, b):
    n, d = x1.shape
    tb = TOKEN_TILE
    nb = n // tb
    row = lambda: pl.BlockSpec((1, d), lambda i: (0, 0))
    return pl.pallas_call(
        _combine_body,
        grid=(nb,),
        in_specs=[
            pl.BlockSpec((LOCAL_ROWS, d), lambda i: (i, 0)),
            pl.BlockSpec((None, tb, TOP_K), lambda i: (i, 0, 0)),
            pl.BlockSpec((None, tb, TOP_K), lambda i: (i, 0, 0)),
            pl.BlockSpec((tb, d), lambda i: (i, 0)),
            row(), row(),
        ],
        out_specs=pl.BlockSpec((tb, d), lambda i: (i, 0)),
        out_shape=jax.ShapeDtypeStruct((n, d), F32),
        compiler_params=_cparams(("arbitrary",), 40),
        name="combine_ln",
    )(y_local, pos, gates, x1, g, b)


def kernel(x, emb_ln_g, emb_ln_b, w_in, lambda_q1, lambda_k1, lambda_q2, lambda_k2, subln_g, conv_w, conv_b,
           conv_ln_g, conv_ln_b, w_out, ln1_g, ln1_b, router_w, router_b, w_gate_up, b_gate_up, w_down, b_down,
           ln2_g, ln2_b):
    batch, seq, d = x.shape
    n = batch * seq
    assert n % TOKEN_TILE == 0 and seq % ATTN_KV_BLOCK == 0 and seq % CONV_TILE == 0
    assert w_in.shape[0] == DEPTH
    x2 = x.reshape(n, d)
    row = lambda v: v.reshape(1, -1).astype(F32)
    slopes = jnp.array([(2.0 ** (-8.0 / N_HEADS)) ** (i + 1) for i in range(N_HEADS)], dtype=F32)

    h = _inproj(x2, row(emb_ln_g), row(emb_ln_b), w_in[0].astype(BF16))
    a = _attention(h, slopes, row(lambda_q1[0]), row(lambda_k1[0]), row(lambda_q2[0]), row(lambda_k2[0]),
                   row(subln_g[0]), batch=batch, seq=seq)
    c = _conv_module(h, conv_w[0], row(conv_b[0]), row(conv_ln_g[0]), row(conv_ln_b[0]), batch=batch, seq=seq)
    x1, xs_local, pos_t, gates_t, cc = _mix(
        x2, a, c, w_out[0].astype(BF16), row(emb_ln_g), row(emb_ln_b), row(ln1_g[0]), row(ln1_b[0]),
        router_w[0].T, router_b[0].reshape(N_EXPERTS, 1))

    te, src, dst = _build_schedule(cc[:, :, 0])
    nb = n // TOKEN_TILE
    y_chunks = _moe(xs_local.reshape(nb * CHUNKS_PER_BLOCK, BF16_ROWS, d), te, src, dst,
                    w_gate_up[0], b_gate_up[0].reshape(N_EXPERTS, 1, -1), w_down[0],
                    b_down[0].reshape(N_EXPERTS, 1, -1))
    y_local = y_chunks.reshape(-1, d)
    out = _combine(y_local, jnp.transpose(pos_t, (0, 2, 1)), jnp.transpose(gates_t, (0, 2, 1)), x1,
                   row(ln2_g[0]), row(ln2_b[0]))
    return out.reshape(batch, seq, d)
```

```python
import functools

import jax
import jax.numpy as jnp
from jax import lax
from jax.experimental import pallas as pl
from jax.experimental.pallas import tpu as pltpu

F32 = jnp.float32
BF16 = jnp.bfloat16
I32 = jnp.int32

N_HEADS = 4
HEAD_DIM = 64
V_DIM = 128
ATTN_WIDTH = N_HEADS * V_DIM
CONV_WIDTH = 512
CONV_KERNEL = 31
N_EXPERTS = 32
TOP_K = 4
SWIGLU_LIMIT = 7.0
SWIGLU_ALPHA = 1.702
LN_EPS = 1e-5
DEPTH = 1
DEEPNORM_ALPHA = (2.0 * DEPTH) ** 0.25
LAMBDA_INIT = 0.2
LOG2_E = 1.4426950408889634
Q_COLS = N_HEADS * 2 * HEAD_DIM
Q_SCALE = HEAD_DIM ** -0.5 * LOG2_E

LANES = 128
BF16_ROWS = 16

TOKEN_TILE = 512
ATTN_KV_BLOCK = 256
ATTN_SUB = 128
CONV_TILE = 64
CONV_ROW_STRIDE = 4
CONV_GLU_TILE = 128
MOE_TILE = 512
CHUNKS_PER_TILE = MOE_TILE // BF16_ROWS
MOE_DMA_STARTS_PER_GAP = 16
LOCAL_ROWS = TOP_K * TOKEN_TILE + N_EXPERTS * BF16_ROWS
CHUNKS_PER_BLOCK = LOCAL_ROWS // BF16_ROWS


def _cparams(semantics, vmem_mib):
    return pltpu.CompilerParams(dimension_semantics=semantics, vmem_limit_bytes=vmem_mib * 1024 * 1024)


def _layer_norm(x, g, b):
    mu = jnp.mean(x, axis=-1, keepdims=True)
    xc = x - mu
    var = jnp.mean(xc * xc, axis=-1, keepdims=True)
    return xc * lax.rsqrt(var + LN_EPS) * g + b


def _sigmoid(x):
    return 1.0 / (1.0 + jnp.exp(-x))


def _inproj_body(x_ref, g_ref, b_ref, w_ref, h_ref, *, n_chunk):
    xn = _layer_norm(x_ref[...], g_ref[...], b_ref[...]).astype(BF16)
    for j in range(w_ref.shape[1] // n_chunk):
        sl = slice(j * n_chunk, (j + 1) * n_chunk)
        hj = jnp.dot(xn, w_ref[:, sl], preferred_element_type=F32)
        if (j + 1) * n_chunk <= Q_COLS:
            hj = hj * Q_SCALE
        h_ref[:, sl] = hj.astype(BF16)


def _inproj(x2, g, b, w_bf16):
    n, d = x2.shape
    cols = w_bf16.shape[1]
    tm = TOKEN_TILE
    return pl.pallas_call(
        functools.partial(_inproj_body, n_chunk=512),
        grid=(n // tm,),
        in_specs=[
            pl.BlockSpec((tm, d), lambda i: (i, 0)),
            pl.BlockSpec((1, d), lambda i: (0, 0)),
            pl.BlockSpec((1, d), lambda i: (0, 0)),
            pl.BlockSpec((d, cols), lambda i: (0, 0)),
        ],
        out_specs=pl.BlockSpec((tm, cols), lambda i: (i, 0)),
        out_shape=jax.ShapeDtypeStruct((n, cols), BF16),
        compiler_params=_cparams(("arbitrary",), 40),
        name="inproj",
    )(x2, g, b, w_bf16)


def _attn_body(slopes_ref, lq1_ref, lk1_ref, lq2_ref, lk2_ref, g_ref, q_ref, k_ref, v_ref, o_ref,
               q2_ref, m_ref, l_ref, acc_ref, *, seq):
    head = pl.program_id(1)
    slope = slopes_ref[head] * LOG2_E
    lam =(jnp.exp(jnp.sum(lq1_ref[...] * lk1_ref[...], axis=-1, keepdims=True))
           - jnp.exp(jnp.sum(lq2_ref[...] * lk2_ref[...], axis=-1, keepdims=True))
           + LAMBDA_INIT)

    q = q_ref[...]
    lane = lax.broadcasted_iota(I32, q.shape, 1)
    zero = jnp.zeros_like(q)
    q2_ref[0:seq, :] = jnp.where(lane < HEAD_DIM, q, zero)
    q2_ref[seq:, :] = jnp.where(lane >= HEAD_DIM, q, zero)
    m_ref[...] = jnp.full(m_ref.shape, -jnp.inf, F32)
    l_ref[...] = jnp.zeros(l_ref.shape, F32)
    acc_ref[...] = jnp.zeros(acc_ref.shape, F32)
    kb_rows = ATTN_KV_BLOCK
    sub = ATTN_SUB
    ones = jnp.ones((kb_rows, V_DIM), BF16)
    col = lax.broadcasted_iota(I32, (1, kb_rows), 1)

    for d in range(seq // kb_rows):
        k0 = d * kb_rows
        kb = k_ref[k0:k0 + kb_rows, :]
        vext = jnp.concatenate([v_ref[k0:k0 + kb_rows, :], ones], axis=1)
        for half in range(2):
            for i in range(k0 // sub, seq // sub):
                r0 = i * sub
                rows = slice(half * seq + r0, half * seq + r0 + sub)
                ncol = min(kb_rows, -(-(r0 + sub - k0) // LANES) * LANES)
                bias = (col[:, 0:ncol] + (k0 - r0)).astype(F32) * slope
                s = lax.dot_general(q2_ref[rows, :], kb[0:ncol, :], (((1,), (1,)), ((), ())),
                                    preferred_element_type=F32) + bias
                if k0 + ncol - 1 > r0:
                    row = lax.broadcasted_iota(I32, s.shape, 0) + (r0 - k0)
                    s = jnp.where(lax.broadcasted_iota(I32, s.shape, 1) <= row, s, -jnp.inf)
                m_prev = m_ref[rows, :]
                m_new = jnp.maximum(m_prev, jnp.max(s, axis=1, keepdims=True))
                alpha = jnp.exp2(m_prev - m_new)
                p = jnp.exp2(s - jnp.concatenate([m_new] * (ncol // LANES), axis=1))
                pv = jnp.dot(p.astype(BF16), vext[0:ncol, :], preferred_element_type=F32)
                acc_ref[rows, :] = alpha * acc_ref[rows, :] + pv[:, :V_DIM]
                l_ref[rows, :] = alpha * l_ref[rows, :] + pv[:, V_DIM:]
                m_ref[rows, :] = m_new

    o = acc_ref[...] / l_ref[...]
    od = o[:seq] - lam * o[seq:]
    ms = jnp.mean(od * od, axis=-1, keepdims=True)
    out = od * lax.rsqrt(ms + LN_EPS) * g_ref[...] * (1.0 - LAMBDA_INIT)
    o_ref[...] = out.astype(BF16)


def _attention(h, slopes, lq1, lk1, lq2, lk2, subln_g, *, batch, seq):
    n = h.shape[0]
    vec = lambda: pl.BlockSpec((1, HEAD_DIM), lambda b, hd, *_: (0, 0))
    grid_spec = pltpu.PrefetchScalarGridSpec(
        num_scalar_prefetch=1,
        grid=(batch, N_HEADS),
        in_specs=[
            vec(), vec(), vec(), vec(),
            pl.BlockSpec((1, V_DIM), lambda b, hd, *_: (0, 0)),
            pl.BlockSpec((seq, V_DIM), lambda b, hd, *_: (b, hd)),
            pl.BlockSpec((seq, V_DIM), lambda b, hd, *_: (b, N_HEADS + hd)),
            pl.BlockSpec((seq, V_DIM), lambda b, hd, *_: (b, 2 * N_HEADS + hd)),
        ],
        out_specs=pl.BlockSpec((seq, V_DIM), lambda b, hd, *_: (b, hd)),
        scratch_shapes=[
            pltpu.VMEM((2 * seq, V_DIM), BF16),
            pltpu.VMEM((2 * seq, LANES), F32),
            pltpu.VMEM((2 * seq, LANES), F32),
            pltpu.VMEM((2 * seq, V_DIM), F32),
        ],
    )
    return pl.pallas_call(
        functools.partial(_attn_body, seq=seq),
        grid_spec=grid_spec,
        out_shape=jax.ShapeDtypeStruct((n, ATTN_WIDTH), BF16),
        compiler_params=_cparams(("arbitrary", "arbitrary"), 32),
        name="diff_attn",
    )(slopes, lq1, lk1, lq2, lk2, subln_g, h, h, h)


CONV_PAD = 32


def _conv_body(a_ref, g_ref, w_ref, cb_ref, lg_ref, lb_ref, o_ref, glu_ref, conv_ref, *, seq):
    n_slab = CONV_WIDTH // LANES
    for c in range(n_slab):
        glu_ref[c, 0:CONV_PAD, :] = jnp.zeros((CONV_PAD, LANES), F32)

    def glu(i, _):
        st = pl.multiple_of(i * CONV_GLU_TILE, CONV_GLU_TILE)
        a = a_ref[pl.ds(st, CONV_GLU_TILE), :].astype(F32)
        g = g_ref[pl.ds(st, CONV_GLU_TILE), :].astype(F32)
        val = a * _sigmoid(g)
        for c in range(n_slab):
            glu_ref[c, pl.ds(CONV_PAD + st, CONV_GLU_TILE), :] = val[:, c * LANES:(c + 1) * LANES]
        return 0

    lax.fori_loop(0, seq // CONV_GLU_TILE, glu, 0)

    span = 8 * CONV_ROW_STRIDE
    for c in range(n_slab):
        lanes = slice(c * LANES, (c + 1) * LANES)
        taps = [jnp.broadcast_to(w_ref[j:j + 1, lanes], (8, LANES)) for j in range(CONV_KERNEL)]
        bias = jnp.broadcast_to(cb_ref[:, lanes], (8, LANES))

        def conv(i, _):
            st = i * CONV_TILE
            for grp in range(CONV_TILE // span):
                for p in range(CONV_ROW_STRIDE):
                    t0 = st + grp * span + p
                    acc = bias
                    for j in range(CONV_KERNEL):
                        src = pl.ds(t0 + (CONV_PAD - (CONV_KERNEL - 1) + j), 8, stride=CONV_ROW_STRIDE)
                        acc = acc + glu_ref[c, src, :] * taps[j]
                    conv_ref[c, pl.ds(t0, 8, stride=CONV_ROW_STRIDE), :] = acc
            return 0

        lax.fori_loop(0, seq // CONV_TILE, conv, 0)

    def norm(i, _):
        st = pl.multiple_of(i * CONV_GLU_TILE, CONV_GLU_TILE)
        x = jnp.concatenate([conv_ref[c, pl.ds(st, CONV_GLU_TILE), :] for c in range(n_slab)], axis=1)
        y = _layer_norm(x, lg_ref[...], lb_ref[...])
        o_ref[pl.ds(st, CONV_GLU_TILE), :] = (y * _sigmoid(y)).astype(BF16)
        return 0

    lax.fori_loop(0, seq // CONV_GLU_TILE, norm, 0)


def _conv_module(h, conv_w, conv_b, ln_g, ln_b, *, batch, seq):
    n = h.shape[0]
    u_col = 3 * ATTN_WIDTH // CONV_WIDTH
    row = lambda: pl.BlockSpec((1, CONV_WIDTH), lambda b: (0, 0))
    return pl.pallas_call(
        functools.partial(_conv_body, seq=seq),
        grid=(batch,),
        in_specs=[
            pl.BlockSpec((seq, CONV_WIDTH), lambda b: (b, u_col)),
            pl.BlockSpec((seq, CONV_WIDTH), lambda b: (b, u_col + 1)),
            pl.BlockSpec((CONV_KERNEL, CONV_WIDTH), lambda b: (0, 0)),
            row(), row(), row(),
        ],
        out_specs=pl.BlockSpec((seq, CONV_WIDTH), lambda b: (b, 0)),
        out_shape=jax.ShapeDtypeStruct((n, CONV_WIDTH), BF16),
        scratch_shapes=[
            pltpu.VMEM((CONV_WIDTH // LANES, CONV_PAD + seq, LANES), F32),
            pltpu.VMEM((CONV_WIDTH // LANES, seq, LANES), F32),
        ],
        compiler_params=_cparams(("arbitrary",), 32),
        name="conformer_conv",
    )(h, h, conv_w, conv_b, ln_g, ln_b)


def _mix_body(x_ref, a_ref, c_ref, wo_ref, g0_ref, b0_ref, g1_ref, b1_ref, rwt_ref, rb_ref,
              x1_ref, xs_ref, pos_ref, gate_ref, cc_ref):
    tb = x_ref.shape[0]
    xn = _layer_norm(x_ref[...], g0_ref[...], b0_ref[...])
    mix = (jnp.dot(a_ref[...], wo_ref[0:ATTN_WIDTH, :], preferred_element_type=F32)
           + jnp.dot(c_ref[...], wo_ref[ATTN_WIDTH:, :], preferred_element_type=F32))
    x1 = _layer_norm(DEEPNORM_ALPHA * xn + mix, g1_ref[...], b1_ref[...])
    x1_ref[...] = x1
    x1b = x1.astype(BF16)

    logits = lax.dot_general(rwt_ref[...], x1, (((1,), (1,)), ((), ())),
                             precision=lax.Precision.HIGHEST, preferred_element_type=F32) + rb_ref[...]
    eidx = lax.broadcasted_iota(I32, logits.shape, 0)
    work = logits
    vals, hots = [], []
    for _ in range(TOP_K):
        mx = jnp.max(work, axis=0, keepdims=True)
        sel = jnp.min(jnp.where(work == mx, eidx, N_EXPERTS), axis=0, keepdims=True)
        hot = eidx == sel
        vals.append(mx)
        hots.append(hot)
        work = jnp.where(hot, -jnp.inf, work)
    ex = [jnp.exp(v - vals[0]) for v in vals]
    den = ex[0] + ex[1] + ex[2] + ex[3]
    for k in range(TOP_K):
        gate_ref[k:k + 1, :] = ex[k] / den

    member = jnp.zeros(logits.shape, F32)
    for hot in hots:
        member = member + jnp.where(hot, 1.0, 0.0)
    before = (lax.broadcasted_iota(I32, (tb, tb), 0) < lax.broadcasted_iota(I32, (tb, tb), 1))
    rank = jnp.dot(member.astype(BF16), jnp.where(before, 1.0, 0.0).astype(BF16), preferred_element_type=F32)
    count = jnp.sum(member, axis=1, keepdims=True)
    chunks = jnp.floor((count + (BF16_ROWS - 1.0)) * (1.0 / BF16_ROWS))
    cc_ref[...] = jnp.broadcast_to(chunks, cc_ref.shape).astype(I32)
    lower = (lax.broadcasted_iota(I32, (N_EXPERTS, N_EXPERTS), 1) < lax.broadcasted_iota(I32, (N_EXPERTS, N_EXPERTS), 0))
    seg_start = jnp.dot(jnp.where(lower, 1.0, 0.0).astype(BF16),
                        jnp.broadcast_to(chunks, (N_EXPERTS, LANES)).astype(BF16),
                        preferred_element_type=F32)[:, 0:1]
    slot = seg_start * float(BF16_ROWS) + rank
    pos = [jnp.sum(jnp.where(hot, slot, 0.0), axis=0, keepdims=True).astype(I32) for hot in hots]
    for k in range(TOP_K):
        pos_ref[k:k + 1, :] = pos[k]

    rows = 256
    one, nil = jnp.ones((), BF16), jnp.zeros((), BF16)
    pos16 = [p.astype(jnp.int16) for p in pos]
    for rc in range(xs_ref.shape[0] // rows):
        r = lax.broadcasted_iota(jnp.int16, (rows, tb), 0) + jnp.int16(rc * rows)
        hit = (r == pos16[0]) | (r == pos16[1]) | (r == pos16[2]) | (r == pos16[3])
        sel = jnp.where(hit, one, nil)
        xs_ref[rc * rows:(rc + 1) * rows, :] = jnp.dot(sel, x1b, preferred_element_type=F32).astype(BF16)


def _mix(x2, a, c, wo_bf16, g0, b0, g1, b1, rwt, rb):
    n, d = x2.shape
    tb = TOKEN_TILE
    nb = n // tb
    row = lambda: pl.BlockSpec((1, d), lambda i: (0, 0))
    return pl.pallas_call(
        _mix_body,
        grid=(nb,),
        in_specs=[
            pl.BlockSpec((tb, d), lambda i: (i, 0)),
            pl.BlockSpec((tb, ATTN_WIDTH), lambda i: (i, 0)),
            pl.BlockSpec((tb, CONV_WIDTH), lambda i: (i, 0)),
            pl.BlockSpec((d, d), lambda i: (0, 0)),
            row(), row(), row(), row(),
            pl.BlockSpec((N_EXPERTS, d), lambda i: (0, 0)),
            pl.BlockSpec((N_EXPERTS, 1), lambda i: (0, 0)),
        ],
        out_specs=[
            pl.BlockSpec((tb, d), lambda i: (i, 0)),
            pl.BlockSpec((None, LOCAL_ROWS, d), lambda i: (i, 0, 0)),
            pl.BlockSpec((None, TOP_K, tb), lambda i: (i, 0, 0)),
            pl.BlockSpec((None, TOP_K, tb), lambda i: (i, 0, 0)),
            pl.BlockSpec((None, N_EXPERTS, LANES), lambda i: (i, 0, 0)),
        ],
        out_shape=[
            jax.ShapeDtypeStruct((n, d), F32),
            jax.ShapeDtypeStruct((nb, LOCAL_ROWS, d), BF16),
            jax.ShapeDtypeStruct((nb, TOP_K, tb), I32),
            jax.ShapeDtypeStruct((nb, TOP_K, tb), F32),
            jax.ShapeDtypeStruct((nb, N_EXPERTS, LANES), I32),
        ],
        compiler_params=_cparams(("arbitrary",), 48),
        name="mix_router_sort",
    )(x2, a, c, wo_bf16, g0, b0, g1, b1, rwt, rb)


def _max_tiles(nb):
    return (nb * CHUNKS_PER_BLOCK + CHUNKS_PER_TILE - 1) // CHUNKS_PER_TILE + N_EXPERTS + 1


def _build_schedule(cc):
    nb = cc.shape[0]
    cpt = CHUNKS_PER_TILE
    ne1 = N_EXPERTS + 1
    tail = CHUNKS_PER_BLOCK - jnp.sum(cc, axis=1)
    cce = jnp.concatenate([cc, tail[:, None]], axis=1)
    seg_start = jnp.cumsum(cce, axis=1) - cce
    tot = jnp.sum(cce, axis=0)
    tiles_e = (tot + cpt - 1) // cpt
    tile_end = jnp.cumsum(tiles_e)
    max_tiles = _max_tiles(nb)

    def lookup(passed, table):
        return table[0] + jnp.sum(jnp.where(passed, (table[1:] - table[:-1])[None, :], 0), axis=1)

    t = jnp.arange(max_tiles, dtype=I32)
    passed = tile_end[None, :] <= t[:, None]
    te = jnp.minimum(jnp.sum(passed, axis=1), ne1 - 1).astype(I32)

    pad_len = tiles_e * cpt - tot
    seg_len = jnp.concatenate([cce.T, pad_len[:, None]], axis=1).reshape(-1)
    seg_end = jnp.cumsum(seg_len)
    block = jnp.arange(nb, dtype=I32)[None, :]
    is_real = jnp.concatenate([jnp.ones((ne1, nb), I32), jnp.zeros((ne1, 1), I32)], axis=1).reshape(-1)
    seg_first = (seg_end - seg_len) * is_real
    zero = jnp.zeros((1,), I32)

    base = block * CHUNKS_PER_BLOCK + seg_start.T
    first = jnp.concatenate([jnp.concatenate([base, jnp.zeros((ne1, 1), I32)], axis=1).reshape(-1) - seg_first,
                             zero])
    p = jnp.arange(max_tiles * cpt, dtype=I32)
    passed = seg_end[None, :] <= p[:, None]
    real = lookup(passed, jnp.concatenate([is_real, zero]))
    src = (real * p + lookup(passed, first)).astype(I32)
    dump0 = nb * CHUNKS_PER_BLOCK
    dst = jnp.where(real > 0, src, dump0 + ((p // cpt) % 2) * cpt + p % cpt).astype(I32)
    dummy = dump0 + cpt + jnp.arange(cpt, dtype=I32)
    return te, src, jnp.concatenate([dummy, dst])


def _moe_body(te_ref, src_ref, dst_ref, xs_hbm, wgu_ref, bgu_ref, wd_ref, bd_ref, y_hbm,
              xbuf, ybuf, act, wgu_b, wd_b, in_sem, out_sem):
    t = pl.program_id(0)
    last = pl.num_programs(0) - 1
    cpt = CHUNKS_PER_TILE
    f = wd_ref.shape[0]
    slot = t % 2
    other = 1 - slot
    e = te_ref[t]

    def rows(c):
        return pl.ds(c * BF16_ROWS, BF16_ROWS)

    def in_copy(tile, buf, c):
        return pltpu.make_async_copy(xs_hbm.at[src_ref[tile * cpt + c]], xbuf.at[buf, rows(c)], in_sem.at[buf])

    def out_copy(tile, buf, c):
        return pltpu.make_async_copy(ybuf.at[buf, rows(c)], y_hbm.at[dst_ref[(tile + 1) * cpt + c]],
                                     out_sem.at[buf])

    def start_all(copy, tile, buf):
        for c in range(cpt):
            copy(tile, buf, c).start()

    def wait_all(copy, tile, buf):
        for c in range(cpt):
            copy(tile, buf, c).wait()

    def neighbour_copies():
        nxt = jnp.minimum(t + 1, last)
        return ([in_copy(nxt, other, c) for c in range(cpt)] + [out_copy(t - 1, other, c) for c in range(cpt)])

    @pl.when(t == 0)
    def _():
        start_all(in_copy, 0, 0)
        ybuf[1] = jnp.zeros(ybuf.shape[1:], BF16)
        even_dump = [pltpu.make_async_copy(ybuf.at[1, rows(c)], y_hbm.at[y_hbm.shape[0] - 2 * cpt + c],
                                           out_sem.at[0]) for c in range(cpt)]
        for copy in even_dump:
            copy.start()
        for copy in even_dump:
            copy.wait()

    @pl.when(t >= 1)
    def _():
        wait_all(out_copy, t - 2, slot)

    wait_all(in_copy, t, slot)

    @pl.when((e < N_EXPERTS) & ((t == 0) | (te_ref[jnp.maximum(t - 1, 0)] != e)))
    def _():
        wgu_b[...] = wgu_ref[...].astype(BF16)
        wd_b[...] = wd_ref[...].astype(BF16)

    @pl.when(e < N_EXPERTS)
    def _():
        pending = neighbour_copies()
        nc = 512
        n_up, n_down = f // nc, wd_ref.shape[1] // nc
        per_gap = MOE_DMA_STARTS_PER_GAP

        def start_some():
            for _ in range(min(per_gap, len(pending))):
                pending.pop(0).start()

        x = xbuf[slot]
        for j in range(n_up):
            gsl = slice(j * nc, (j + 1) * nc)
            usl = slice(f + j * nc, f + (j + 1) * nc)
            g = jnp.dot(x, wgu_b[:, gsl], preferred_element_type=F32) + bgu_ref[:, gsl]
            start_some()
            u = jnp.dot(x, wgu_b[:, usl], preferred_element_type=F32) + bgu_ref[:, usl]
            start_some()
            g = jnp.minimum(g, SWIGLU_LIMIT)
            u = jnp.clip(u, -SWIGLU_LIMIT, SWIGLU_LIMIT)
            act[:, gsl] = (g * _sigmoid(SWIGLU_ALPHA * g) * (u + 1.0)).astype(BF16)
            start_some()
        a = act[...]
        for j in range(n_down):
            sl = slice(j * nc, (j + 1) * nc)
            y = jnp.dot(a, wd_b[:, sl], preferred_element_type=F32) + bd_ref[:, sl]
            ybuf[slot, :, sl] = y.astype(BF16)
            start_some()
        assert not pending

    @pl.when(e >= N_EXPERTS)
    def _():
        for copy in neighbour_copies():
            copy.start()
        ybuf[slot] = jnp.zeros(ybuf.shape[1:], BF16)

    @pl.when(t == last)
    def _():
        start_all(out_copy, t, slot)
        wait_all(out_copy, t - 1, other)
        wait_all(out_copy, t, slot)
        wait_all(in_copy, t, other)


def _moe(xs_chunks, te, src, dst, w_gate_up, b_gate_up, w_down, b_down):
    nch, _, d = xs_chunks.shape
    f = w_down.shape[1]
    max_tiles = te.shape[0]
    wsel = lambda t, te_r, src_r, dst_r: (jnp.minimum(te_r[t], N_EXPERTS - 1), 0, 0)
    grid_spec = pltpu.PrefetchScalarGridSpec(
        num_scalar_prefetch=3,
        grid=(max_tiles,),
        in_specs=[
            pl.BlockSpec(memory_space=pl.ANY),
            pl.BlockSpec((None, d, 2 * f), wsel),
            pl.BlockSpec((None, 1, 2 * f), wsel),
            pl.BlockSpec((None, f, d), wsel),
            pl.BlockSpec((None, 1, d), wsel),
        ],
        out_specs=pl.BlockSpec(memory_space=pl.ANY),
        scratch_shapes=[
            pltpu.VMEM((2, MOE_TILE, d), BF16),
            pltpu.VMEM((2, MOE_TILE, d), BF16),
            pltpu.VMEM((MOE_TILE, f), BF16),
            pltpu.VMEM((d, 2 * f), BF16),
            pltpu.VMEM((f, d), BF16),
            pltpu.SemaphoreType.DMA((2,)),
            pltpu.SemaphoreType.DMA((2,)),
        ],
    )
    return pl.pallas_call(
        _moe_body,
        grid_spec=grid_spec,
        out_shape=jax.ShapeDtypeStruct((nch + 2 * CHUNKS_PER_TILE, BF16_ROWS, d), BF16),
        compiler_params=_cparams(("arbitrary",), 56),
        name="moe_grouped",
    )(te, src, dst, xs_chunks, w_gate_up, b_gate_up, w_down, b_down)


def _combine_body(y_ref, pos_ref, gate_ref, x1_ref, g_ref, b_ref, o_ref):
    tb = x1_ref.shape[0]
    pos = pos_ref[...]
    gate = gate_ref[...]
    acc = jnp.zeros(x1_ref.shape, F32)
    cols = 512
    for rc in range(y_ref.shape[0] // cols):
        r = lax.broadcasted_iota(I32, (tb, cols), 1) + rc * cols
        w = jnp.zeros((tb, cols), F32)
        for k in range(TOP_K):
            w = w + jnp.where(r == pos[:, k:k + 1], gate[:, k:k + 1], 0.0)
        acc = acc + jnp.dot(w.astype(BF16), y_ref[rc * cols:(rc + 1) * cols, :], preferred_element_type=F32)
    o_ref[...] = _layer_norm(DEEPNORM_ALPHA * x1_ref[...] + acc, g_ref[...], b_ref[...])


def _combine(y_local, pos, gates, x1, g, b):
    n, d = x1.shape
    tb = TOKEN_TILE
    nb = n // tb
    row = lambda: pl.BlockSpec((1, d), lambda i: (0, 0))
    return pl.pallas_call(
        _combine_body,
        grid=(nb,),
        in_specs=[
            pl.BlockSpec((LOCAL_ROWS, d), lambda i: (i, 0)),
            pl.BlockSpec((None, tb, TOP_K), lambda i: (i, 0, 0)),
            pl.BlockSpec((None, tb, TOP_K), lambda i: (i, 0, 0)),
            pl.BlockSpec((tb, d), lambda i: (i, 0)),
            row(), row(),
        ],
        out_specs=pl.BlockSpec((tb, d), lambda i: (i, 0)),
        out_shape=jax.ShapeDtypeStruct((n, d), F32),
        compiler_params=_cparams(("arbitrary",), 40),
        name="combine_ln",
    )(y_local, pos, gates, x1, g, b)


def kernel(x, emb_ln_g, emb_ln_b, w_in, lambda_q1, lambda_k1, lambda_q2, lambda_k2, subln_g, conv_w, conv_b,
           conv_ln_g, conv_ln_b, w_out, ln1_g, ln1_b, router_w, router_b, w_gate_up, b_gate_up, w_down, b_down,
           ln2_g, ln2_b):
    batch, seq, d = x.shape
    n = batch * seq
    assert n % TOKEN_TILE == 0 and seq % ATTN_KV_BLOCK == 0 and seq % CONV_TILE == 0
    assert w_in.shape[0] == DEPTH
    x2 = x.reshape(n, d)
    row = lambda v: v.reshape(1, -1).astype(F32)
    slopes = jnp.array([(2.0 ** (-8.0 / N_HEADS)) ** (i + 1) for i in range(N_HEADS)], dtype=F32)

    h = _inproj(x2, row(emb_ln_g), row(emb_ln_b), w_in[0].astype(BF16))
    a = _attention(h, slopes, row(lambda_q1[0]), row(lambda_k1[0]), row(lambda_q2[0]), row(lambda_k2[0]),
                   row(subln_g[0]), batch=batch, seq=seq)
    c = _conv_module(h, conv_w[0], row(conv_b[0]), row(conv_ln_g[0]), row(conv_ln_b[0]), batch=batch, seq=seq)
    x1, xs_local, pos_t, gates_t, cc = _mix(
        x2, a, c, w_out[0].astype(BF16), row(emb_ln_g), row(emb_ln_b), row(ln1_g[0]), row(ln1_b[0]),
        router_w[0].T, router_b[0].reshape(N_EXPERTS, 1))

    te, src, dst = _build_schedule(cc[:, :, 0])
    nb = n // TOKEN_TILE
    y_chunks = _moe(xs_local.reshape(nb * CHUNKS_PER_BLOCK, BF16_ROWS, d), te, src, dst,
                    w_gate_up[0], b_gate_up[0].reshape(N_EXPERTS, 1, -1), w_down[0],
                    b_down[0].reshape(N_EXPERTS, 1, -1))
    y_local = y_chunks.reshape(-1, d)
    out = _combine(y_local, jnp.transpose(pos_t, (0, 2, 1)), jnp.transpose(gates_t, (0, 2, 1)), x1,
                   row(ln2_g[0]), row(ln2_b[0]))
    return out.reshape(batch, seq, d)
```

```python
import functools

import jax
import jax.numpy as jnp
from jax import lax
from jax.experimental import pallas as pl
from jax.experimental.pallas import tpu as pltpu

F32 = jnp.float32
BF16 = jnp.bfloat16
I32 = jnp.int32

N_HEADS = 4
HEAD_DIM = 64
V_DIM = 128
ATTN_WIDTH = N_HEADS * V_DIM
CONV_WIDTH = 512
CONV_KERNEL = 31
N_EXPERTS = 32
TOP_K = 4
SWIGLU_LIMIT = 7.0
SWIGLU_ALPHA = 1.702
LN_EPS = 1e-5
DEPTH = 1
DEEPNORM_ALPHA = (2.0 * DEPTH) ** 0.25
LAMBDA_INIT = 0.2
LOG2_E = 1.4426950408889634
Q_COLS = N_HEADS * 2 * HEAD_DIM
Q_SCALE = HEAD_DIM ** -0.5 * LOG2_E

LANES = 128
BF16_ROWS = 16

TOKEN_TILE = 512
ATTN_KV_BLOCK = 256
ATTN_SUB = 128
CONV_TILE = 64
CONV_ROW_STRIDE = 4
CONV_GLU_TILE = 128
MOE_TILE = 512
CHUNKS_PER_TILE = MOE_TILE // BF16_ROWS
LOCAL_ROWS = TOP_K * TOKEN_TILE + N_EXPERTS * BF16_ROWS
CHUNKS_PER_BLOCK = LOCAL_ROWS // BF16_ROWS


def _cparams(semantics, vmem_mib):
    return pltpu.CompilerParams(dimension_semantics=semantics, vmem_limit_bytes=vmem_mib * 1024 * 1024)


def _layer_norm(x, g, b):
    mu = jnp.mean(x, axis=-1, keepdims=True)
    xc = x - mu
    var = jnp.mean(xc * xc, axis=-1, keepdims=True)
    return xc * lax.rsqrt(var + LN_EPS) * g + b


def _sigmoid(x):
    return 1.0 / (1.0 + jnp.exp(-x))


def _inproj_body(x_ref, g_ref, b_ref, w_ref, h_ref, *, n_chunk):
    xn = _layer_norm(x_ref[...], g_ref[...], b_ref[...]).astype(BF16)
    for j in range(w_ref.shape[1] // n_chunk):
        sl = slice(j * n_chunk, (j + 1) * n_chunk)
        hj = jnp.dot(xn, w_ref[:, sl], preferred_element_type=F32)
        if (j + 1) * n_chunk <= Q_COLS:
            hj = hj * Q_SCALE
        h_ref[:, sl] = hj.astype(BF16)


def _inproj(x2, g, b, w_bf16):
    n, d = x2.shape
    cols = w_bf16.shape[1]
    tm = TOKEN_TILE
    return pl.pallas_call(
        functools.partial(_inproj_body, n_chunk=512),
        grid=(n // tm,),
        in_specs=[
            pl.BlockSpec((tm, d), lambda i: (i, 0)),
            pl.BlockSpec((1, d), lambda i: (0, 0)),
            pl.BlockSpec((1, d), lambda i: (0, 0)),
            pl.BlockSpec((d, cols), lambda i: (0, 0)),
        ],
        out_specs=pl.BlockSpec((tm, cols), lambda i: (i, 0)),
        out_shape=jax.ShapeDtypeStruct((n, cols), BF16),
        compiler_params=_cparams(("arbitrary",), 40),
        name="inproj",
    )(x2, g, b, w_bf16)


def _attn_body(slopes_ref, lq1_ref, lk1_ref, lq2_ref, lk2_ref, g_ref, q_ref, k_ref, v_ref, o_ref,
               q2_ref, m_ref, l_ref, acc_ref, *, seq):
    head = pl.program_id(1)
    slope = slopes_ref[head] * LOG2_E
    lam =(jnp.exp(jnp.sum(lq1_ref[...] * lk1_ref[...], axis=-1, keepdims=True))
           - jnp.exp(jnp.sum(lq2_ref[...] * lk2_ref[...], axis=-1, keepdims=True))
           + LAMBDA_INIT)

    q = q_ref[...]
    lane = lax.broadcasted_iota(I32, q.shape, 1)
    zero = jnp.zeros_like(q)
    q2_ref[0:seq, :] = jnp.where(lane < HEAD_DIM, q, zero)
    q2_ref[seq:, :] = jnp.where(lane >= HEAD_DIM, q, zero)
    m_ref[...] = jnp.full(m_ref.shape, -jnp.inf, F32)
    l_ref[...] = jnp.zeros(l_ref.shape, F32)
    acc_ref[...] = jnp.zeros(acc_ref.shape, F32)
    kb_rows = ATTN_KV_BLOCK
    sub = ATTN_SUB
    ones = jnp.ones((kb_rows, V_DIM), BF16)
    col = lax.broadcasted_iota(I32, (1, kb_rows), 1)

    for d in range(seq // kb_rows):
        k0 = d * kb_rows
        kb = k_ref[k0:k0 + kb_rows, :]
        vext = jnp.concatenate([v_ref[k0:k0 + kb_rows, :], ones], axis=1)
        for half in range(2):
            for i in range(k0 // sub, seq // sub):
                r0 = i * sub
                rows = slice(half * seq + r0, half * seq + r0 + sub)
                ncol = min(kb_rows, -(-(r0 + sub - k0) // LANES) * LANES)
                bias = (col[:, 0:ncol] + (k0 - r0)).astype(F32) * slope
                s = lax.dot_general(q2_ref[rows, :], kb[0:ncol, :], (((1,), (1,)), ((), ())),
                                    preferred_element_type=F32) + bias
                if k0 + ncol - 1 > r0:
                    row = lax.broadcasted_iota(I32, s.shape, 0) + (r0 - k0)
                    s = jnp.where(lax.broadcasted_iota(I32, s.shape, 1) <= row, s, -jnp.inf)
                m_prev = m_ref[rows, :]
                m_new = jnp.maximum(m_prev, jnp.max(s, axis=1, keepdims=True))
                alpha = jnp.exp2(m_prev - m_new)
                p = jnp.exp2(s - jnp.concatenate([m_new] * (ncol // LANES), axis=1))
                pv = jnp.dot(p.astype(BF16), vext[0:ncol, :], preferred_element_type=F32)
                acc_ref[rows, :] = alpha * acc_ref[rows, :] + pv[:, :V_DIM]
                l_ref[rows, :] = alpha * l_ref[rows, :] + pv[:, V_DIM:]
                m_ref[rows, :] = m_new

    o = acc_ref[...] / l_ref[...]
    od = o[:seq] - lam * o[seq:]
    ms = jnp.mean(od * od, axis=-1, keepdims=True)
    out = od * lax.rsqrt(ms + LN_EPS) * g_ref[...] * (1.0 - LAMBDA_INIT)
    o_ref[...] = out.astype(BF16)


def _attention(h, slopes, lq1, lk1, lq2, lk2, subln_g, *, batch, seq):
    n = h.shape[0]
    vec = lambda: pl.BlockSpec((1, HEAD_DIM), lambda b, hd, *_: (0, 0))
    grid_spec = pltpu.PrefetchScalarGridSpec(
        num_scalar_prefetch=1,
        grid=(batch, N_HEADS),
        in_specs=[
            vec(), vec(), vec(), vec(),
            pl.BlockSpec((1, V_DIM), lambda b, hd, *_: (0, 0)),
            pl.BlockSpec((seq, V_DIM), lambda b, hd, *_: (b, hd)),
            pl.BlockSpec((seq, V_DIM), lambda b, hd, *_: (b, N_HEADS + hd)),
            pl.BlockSpec((seq, V_DIM), lambda b, hd, *_: (b, 2 * N_HEADS + hd)),
        ],
        out_specs=pl.BlockSpec((seq, V_DIM), lambda b, hd, *_: (b, hd)),
        scratch_shapes=[
            pltpu.VMEM((2 * seq, V_DIM), BF16),
            pltpu.VMEM((2 * seq, LANES), F32),
            pltpu.VMEM((2 * seq, LANES), F32),
            pltpu.VMEM((2 * seq, V_DIM), F32),
        ],
    )
    return pl.pallas_call(
        functools.partial(_attn_body, seq=seq),
        grid_spec=grid_spec,
        out_shape=jax.ShapeDtypeStruct((n, ATTN_WIDTH), BF16),
        compiler_params=_cparams(("arbitrary", "arbitrary"), 32),
        name="diff_attn",
    )(slopes, lq1, lk1, lq2, lk2, subln_g, h, h, h)


CONV_PAD = 32


def _conv_body(a_ref, g_ref, w_ref, cb_ref, lg_ref, lb_ref, o_ref, glu_ref, conv_ref, *, seq):
    n_slab = CONV_WIDTH // LANES
    for c in range(n_slab):
        glu_ref[c, 0:CONV_PAD, :] = jnp.zeros((CONV_PAD, LANES), F32)

    def glu(i, _):
        st = pl.multiple_of(i * CONV_GLU_TILE, CONV_GLU_TILE)
        a = a_ref[pl.ds(st, CONV_GLU_TILE), :].astype(F32)
        g = g_ref[pl.ds(st, CONV_GLU_TILE), :].astype(F32)
        val = a * _sigmoid(g)
        for c in range(n_slab):
            glu_ref[c, pl.ds(CONV_PAD + st, CONV_GLU_TILE), :] = val[:, c * LANES:(c + 1) * LANES]
        return 0

    lax.fori_loop(0, seq // CONV_GLU_TILE, glu, 0)

    span = 8 * CONV_ROW_STRIDE
    for c in range(n_slab):
        lanes = slice(c * LANES, (c + 1) * LANES)
        taps = [jnp.broadcast_to(w_ref[j:j + 1, lanes], (8, LANES)) for j in range(CONV_KERNEL)]
        bias = jnp.broadcast_to(cb_ref[:, lanes], (8, LANES))

        def conv(i, _):
            st = i * CONV_TILE
            for grp in range(CONV_TILE // span):
                for p in range(CONV_ROW_STRIDE):
                    t0 = st + grp * span + p
                    acc = bias
                    for j in range(CONV_KERNEL):
                        src = pl.ds(t0 + (CONV_PAD - (CONV_KERNEL - 1) + j), 8, stride=CONV_ROW_STRIDE)
                        acc = acc + glu_ref[c, src, :] * taps[j]
                    conv_ref[c, pl.ds(t0, 8, stride=CONV_ROW_STRIDE), :] = acc
            return 0

        lax.fori_loop(0, seq // CONV_TILE, conv, 0)

    def norm(i, _):
        st = pl.multiple_of(i * CONV_GLU_TILE, CONV_GLU_TILE)
        x = jnp.concatenate([conv_ref[c, pl.ds(st, CONV_GLU_TILE), :] for c in range(n_slab)], axis=1)
        y = _layer_norm(x, lg_ref[...], lb_ref[...])
        o_ref[pl.ds(st, CONV_GLU_TILE), :] = (y * _sigmoid(y)).astype(BF16)
        return 0

    lax.fori_loop(0, seq // CONV_GLU_TILE, norm, 0)


def _conv_module(h, conv_w, conv_b, ln_g, ln_b, *, batch, seq):
    n = h.shape[0]
    u_col = 3 * ATTN_WIDTH // CONV_WIDTH
    row = lambda: pl.BlockSpec((1, CONV_WIDTH), lambda b: (0, 0))
    return pl.pallas_call(
        functools.partial(_conv_body, seq=seq),
        grid=(batch,),
        in_specs=[
            pl.BlockSpec((seq, CONV_WIDTH), lambda b: (b, u_col)),
            pl.BlockSpec((seq, CONV_WIDTH), lambda b: (b, u_col + 1)),
            pl.BlockSpec((CONV_KERNEL, CONV_WIDTH), lambda b: (0, 0)),
            row(), row(), row(),
        ],
        out_specs=pl.BlockSpec((seq, CONV_WIDTH), lambda b: (b, 0)),
        out_shape=jax.ShapeDtypeStruct((n, CONV_WIDTH), BF16),
        scratch_shapes=[
            pltpu.VMEM((CONV_WIDTH // LANES, CONV_PAD + seq, LANES), F32),
            pltpu.VMEM((CONV_WIDTH // LANES, seq, LANES), F32),
        ],
        compiler_params=_cparams(("arbitrary",), 32),
        name="conformer_conv",
    )(h, h, conv_w, conv_b, ln_g, ln_b)


def _mix_body(x_ref, a_ref, c_ref, wo_ref, g0_ref, b0_ref, g1_ref, b1_ref, rwt_ref, rb_ref,
              x1_ref, xs_ref, pos_ref, gate_ref, cc_ref):
    tb = x_ref.shape[0]
    xn = _layer_norm(x_ref[...], g0_ref[...], b0_ref[...])
    mix = (jnp.dot(a_ref[...], wo_ref[0:ATTN_WIDTH, :], preferred_element_type=F32)
           + jnp.dot(c_ref[...], wo_ref[ATTN_WIDTH:, :], preferred_element_type=F32))
    x1 = _layer_norm(DEEPNORM_ALPHA * xn + mix, g1_ref[...], b1_ref[...])
    x1_ref[...] = x1
    x1b = x1.astype(BF16)

    logits = lax.dot_general(rwt_ref[...], x1, (((1,), (1,)), ((), ())),
                             precision=lax.Precision.HIGHEST, preferred_element_type=F32) + rb_ref[...]
    eidx = lax.broadcasted_iota(I32, logits.shape, 0)
    work = logits
    vals, hots = [], []
    for _ in range(TOP_K):
        mx = jnp.max(work, axis=0, keepdims=True)
        sel = jnp.min(jnp.where(work == mx, eidx, N_EXPERTS), axis=0, keepdims=True)
        hot = eidx == sel
        vals.append(mx)
        hots.append(hot)
        work = jnp.where(hot, -jnp.inf, work)
    ex = [jnp.exp(v - vals[0]) for v in vals]
    den = ex[0] + ex[1] + ex[2] + ex[3]
    for k in range(TOP_K):
        gate_ref[k:k + 1, :] = ex[k] / den

    member = jnp.zeros(logits.shape, F32)
    for hot in hots:
        member = member + jnp.where(hot, 1.0, 0.0)
    before = (lax.broadcasted_iota(I32, (tb, tb), 0) < lax.broadcasted_iota(I32, (tb, tb), 1))
    rank = jnp.dot(member.astype(BF16), jnp.where(before, 1.0, 0.0).astype(BF16), preferred_element_type=F32)
    count = jnp.sum(member, axis=1, keepdims=True)
    chunks = jnp.floor((count + (BF16_ROWS - 1.0)) * (1.0 / BF16_ROWS))
    cc_ref[...] = jnp.broadcast_to(chunks, cc_ref.shape).astype(I32)
    lower = (lax.broadcasted_iota(I32, (N_EXPERTS, N_EXPERTS), 1) < lax.broadcasted_iota(I32, (N_EXPERTS, N_EXPERTS), 0))
    seg_start = jnp.dot(jnp.where(lower, 1.0, 0.0).astype(BF16),
                        jnp.broadcast_to(chunks, (N_EXPERTS, LANES)).astype(BF16),
                        preferred_element_type=F32)[:, 0:1]
    slot = seg_start * float(BF16_ROWS) + rank
    pos = [jnp.sum(jnp.where(hot, slot, 0.0), axis=0, keepdims=True).astype(I32) for hot in hots]
    for k in range(TOP_K):
        pos_ref[k:k + 1, :] = pos[k]

    rows = 256
    one, nil = jnp.ones((), BF16), jnp.zeros((), BF16)
    pos16 = [p.astype(jnp.int16) for p in pos]
    for rc in range(xs_ref.shape[0] // rows):
        r = lax.broadcasted_iota(jnp.int16, (rows, tb), 0) + jnp.int16(rc * rows)
        hit = (r == pos16[0]) | (r == pos16[1]) | (r == pos16[2]) | (r == pos16[3])
        sel = jnp.where(hit, one, nil)
        xs_ref[rc * rows:(rc + 1) * rows, :] = jnp.dot(sel, x1b, preferred_element_type=F32).astype(BF16)


def _mix(x2, a, c, wo_bf16, g0, b0, g1, b1, rwt, rb):
    n, d = x2.shape
    tb = TOKEN_TILE
    nb = n // tb
    row = lambda: pl.BlockSpec((1, d), lambda i: (0, 0))
    return pl.pallas_call(
        _mix_body,
        grid=(nb,),
        in_specs=[
            pl.BlockSpec((tb, d), lambda i: (i, 0)),
            pl.BlockSpec((tb, ATTN_WIDTH), lambda i: (i, 0)),
            pl.BlockSpec((tb, CONV_WIDTH), lambda i: (i, 0)),
            pl.BlockSpec((d, d), lambda i: (0, 0)),
            row(), row(), row(), row(),
            pl.BlockSpec((N_EXPERTS, d), lambda i: (0, 0)),
            pl.BlockSpec((N_EXPERTS, 1), lambda i: (0, 0)),
        ],
        out_specs=[
            pl.BlockSpec((tb, d), lambda i: (i, 0)),
            pl.BlockSpec((None, LOCAL_ROWS, d), lambda i: (i, 0, 0)),
            pl.BlockSpec((None, TOP_K, tb), lambda i: (i, 0, 0)),
            pl.BlockSpec((None, TOP_K, tb), lambda i: (i, 0, 0)),
            pl.BlockSpec((None, N_EXPERTS, LANES), lambda i: (i, 0, 0)),
        ],
        out_shape=[
            jax.ShapeDtypeStruct((n, d), F32),
            jax.ShapeDtypeStruct((nb, LOCAL_ROWS, d), BF16),
            jax.ShapeDtypeStruct((nb, TOP_K, tb), I32),
            jax.ShapeDtypeStruct((nb, TOP_K, tb), F32),
            jax.ShapeDtypeStruct((nb, N_EXPERTS, LANES), I32),
        ],
        compiler_params=_cparams(("arbitrary",), 48),
        name="mix_router_sort",
    )(x2, a, c, wo_bf16, g0, b0, g1, b1, rwt, rb)


def _max_tiles(nb):
    return (nb * CHUNKS_PER_BLOCK + CHUNKS_PER_TILE - 1) // CHUNKS_PER_TILE + N_EXPERTS + 1


def _build_schedule(cc):
    nb = cc.shape[0]
    cpt = CHUNKS_PER_TILE
    ne1 = N_EXPERTS + 1
    tail = CHUNKS_PER_BLOCK - jnp.sum(cc, axis=1)
    cce = jnp.concatenate([cc, tail[:, None]], axis=1)
    seg_start = jnp.cumsum(cce, axis=1) - cce
    tot = jnp.sum(cce, axis=0)
    tiles_e = (tot + cpt - 1) // cpt
    tile_end = jnp.cumsum(tiles_e)
    max_tiles = _max_tiles(nb)

    def lookup(passed, table):
        return table[0] + jnp.sum(jnp.where(passed, (table[1:] - table[:-1])[None, :], 0), axis=1)

    t = jnp.arange(max_tiles, dtype=I32)
    passed = tile_end[None, :] <= t[:, None]
    te = jnp.minimum(jnp.sum(passed, axis=1), ne1 - 1).astype(I32)

    pad_len = tiles_e * cpt - tot
    seg_len = jnp.concatenate([cce.T, pad_len[:, None]], axis=1).reshape(-1)
    seg_end = jnp.cumsum(seg_len)
    block = jnp.arange(nb, dtype=I32)[None, :]
    is_real = jnp.concatenate([jnp.ones((ne1, nb), I32), jnp.zeros((ne1, 1), I32)], axis=1).reshape(-1)
    seg_first = (seg_end - seg_len) * is_real
    zero = jnp.zeros((1,), I32)

    base = block * CHUNKS_PER_BLOCK + seg_start.T
    first = jnp.concatenate([jnp.concatenate([base, jnp.zeros((ne1, 1), I32)], axis=1).reshape(-1) - seg_first,
                             zero])
    p = jnp.arange(max_tiles * cpt, dtype=I32)
    passed = seg_end[None, :] <= p[:, None]
    real = lookup(passed, jnp.concatenate([is_real, zero]))
    src = (real * p + lookup(passed, first)).astype(I32)
    dump0 = nb * CHUNKS_PER_BLOCK
    dst = jnp.where(real > 0, src, dump0 + ((p // cpt) % 2) * cpt + p % cpt).astype(I32)
    dummy = dump0 + cpt + jnp.arange(cpt, dtype=I32)
    return te, src, jnp.concatenate([dummy, dst]), jnp.reshape(tile_end[-1], (1,)).astype(I32)


def _moe_body(te_ref, src_ref, dst_ref, nt_ref, xs_hbm, wgu_ref, bgu_ref, wd_ref, bd_ref, y_hbm,
              xbuf, ybuf, act, wgu_b, wd_b, in_sem, out_sem):
    t = pl.program_id(0)
    last = nt_ref[0] - 1
    cpt = CHUNKS_PER_TILE
    f = wd_ref.shape[0]
    slot = t % 2
    other = 1 - slot
    e = te_ref[t]

    def rows(c):
        return pl.ds(c * BF16_ROWS, BF16_ROWS)

    def in_copy(tile, buf, c):
        return pltpu.make_async_copy(xs_hbm.at[src_ref[tile * cpt + c]], xbuf.at[buf, rows(c)], in_sem.at[buf])

    def out_copy(tile, buf, c):
        return pltpu.make_async_copy(ybuf.at[buf, rows(c)], y_hbm.at[dst_ref[(tile + 1) * cpt + c]],
                                     out_sem.at[buf])

    def start_all(copy, tile, buf):
        for c in range(cpt):
            copy(tile, buf, c).start()

    def wait_all(copy, tile, buf):
        for c in range(cpt):
            copy(tile, buf, c).wait()

    @pl.when(t == 0)
    def _():
        start_all(in_copy, 0, 0)
        ybuf[1] = jnp.zeros(ybuf.shape[1:], BF16)
        even_dump = [pltpu.make_async_copy(ybuf.at[1, rows(c)], y_hbm.at[y_hbm.shape[0] - 2 * cpt + c],
                                           out_sem.at[0]) for c in range(cpt)]
        for copy in even_dump:
            copy.start()
        for copy in even_dump:
            copy.wait()

    @pl.when(t <= last)
    def _():
        @pl.when(t >= 1)
        def _():
            wait_all(out_copy, t - 2, slot)

        wait_all(in_copy, t, slot)
        start_all(in_copy, jnp.minimum(t + 1, last), other)
        start_all(out_copy, t - 1, other)

        @pl.when((e < N_EXPERTS) & ((t == 0) | (te_ref[jnp.maximum(t - 1, 0)] != e)))
        def _():
            wgu_b[...] = wgu_ref[...].astype(BF16)
            wd_b[...] = wd_ref[...].astype(BF16)

        @pl.when(e < N_EXPERTS)
        def _():
            x = xbuf[slot]
            nc = 512
            for j in range(f // nc):
                gsl = slice(j * nc, (j + 1) * nc)
                usl = slice(f + j * nc, f + (j + 1) * nc)
                g = jnp.dot(x, wgu_b[:, gsl], preferred_element_type=F32) + bgu_ref[:, gsl]
                u = jnp.dot(x, wgu_b[:, usl], preferred_element_type=F32) + bgu_ref[:, usl]
                g = jnp.minimum(g, SWIGLU_LIMIT)
                u = jnp.clip(u, -SWIGLU_LIMIT, SWIGLU_LIMIT)
                act[:, gsl] = (g * _sigmoid(SWIGLU_ALPHA * g) * (u + 1.0)).astype(BF16)
            a = act[...]
            for j in range(wd_ref.shape[1] // nc):
                sl = slice(j * nc, (j + 1) * nc)
                y = jnp.dot(a, wd_b[:, sl], preferred_element_type=F32) + bd_ref[:, sl]
                ybuf[slot, :, sl] = y.astype(BF16)

        @pl.when(e >= N_EXPERTS)
        def _():
            ybuf[slot] = jnp.zeros(ybuf.shape[1:], BF16)

        @pl.when(t == last)
        def _():
            start_all(out_copy, t, slot)
            wait_all(out_copy, t - 1, other)
            wait_all(out_copy, t, slot)
            wait_all(in_copy, t, other)


def _moe(xs_chunks, te, src, dst, nt, w_gate_up, b_gate_up, w_down, b_down):
    nch, _, d = xs_chunks.shape
    f = w_down.shape[1]
    max_tiles = te.shape[0]
    wsel = lambda t, te_r, src_r, dst_r, nt_r: (jnp.minimum(te_r[t], N_EXPERTS - 1), 0, 0)
    grid_spec = pltpu.PrefetchScalarGridSpec(
        num_scalar_prefetch=4,
        grid=(max_tiles,),
        in_specs=[
            pl.BlockSpec(memory_space=pl.ANY),
            pl.BlockSpec((None, d, 2 * f), wsel),
            pl.BlockSpec((None, 1, 2 * f), wsel),
            pl.BlockSpec((None, f, d), wsel),
            pl.BlockSpec((None, 1, d), wsel),
        ],
        out_specs=pl.BlockSpec(memory_space=pl.ANY),
        scratch_shapes=[
            pltpu.VMEM((2, MOE_TILE, d), BF16),
            pltpu.VMEM((2, MOE_TILE, d), BF16),
            pltpu.VMEM((MOE_TILE, f), BF16),
            pltpu.VMEM((d, 2 * f), BF16),
            pltpu.VMEM((f, d), BF16),
            pltpu.SemaphoreType.DMA((2,)),
            pltpu.SemaphoreType.DMA((2,)),
        ],
    )
    return pl.pallas_call(
        _moe_body,
        grid_spec=grid_spec,
        out_shape=jax.ShapeDtypeStruct((nch + 2 * CHUNKS_PER_TILE, BF16_ROWS, d), BF16),
        compiler_params=_cparams(("arbitrary",), 56),
        name="moe_grouped",
    )(te, src, dst, nt, xs_chunks, w_gate_up, b_gate_up, w_down, b_down)


def _combine_body(y_ref, pos_ref, gate_ref, x1_ref, g_ref, b_ref, o_ref):
    tb = x1_ref.shape[0]
    pos = pos_ref[...]
    gate = gate_ref[...]
    acc = jnp.zeros(x1_ref.shape, F32)
    cols = 512
    for rc in range(y_ref.shape[0] // cols):
        r = lax.broadcasted_iota(I32, (tb, cols), 1) + rc * cols
        w = jnp.zeros((tb, cols), F32)
        for k in range(TOP_K):
            w = w + jnp.where(r == pos[:, k:k + 1], gate[:, k:k + 1], 0.0)
        acc = acc + jnp.dot(w.astype(BF16), y_ref[rc * cols:(rc + 1) * cols, :], preferred_element_type=F32)
    o_ref[...] = _layer_norm(DEEPNORM_ALPHA * x1_ref[...] + acc, g_ref[...], b_ref[...])


def _combine(y_local, pos, gates, x1, g, b):
    n, d = x1.shape
    tb = TOKEN_TILE
    nb = n // tb
    row = lambda: pl.BlockSpec((1, d), lambda i: (0, 0))
    return pl.pallas_call(
        _combine_body,
        grid=(nb,),
        in_specs=[
            pl.BlockSpec((LOCAL_ROWS, d), lambda i: (i, 0)),
            pl.BlockSpec((None, tb, TOP_K), lambda i: (i, 0, 0)),
            pl.BlockSpec((None, tb, TOP_K), lambda i: (i, 0, 0)),
            pl.BlockSpec((tb, d), lambda i: (i, 0)),
            row(), row(),
        ],
        out_specs=pl.BlockSpec((tb, d), lambda i: (i, 0)),
        out_shape=jax.ShapeDtypeStruct((n, d), F32),
        compiler_params=_cparams(("arbitrary",), 40),
        name="combine_ln",
    )(y_local, pos, gates, x1, g, b)


def kernel(x, emb_ln_g, emb_ln_b, w_in, lambda_q1, lambda_k1, lambda_q2, lambda_k2, subln_g, conv_w, conv_b,
           conv_ln_g, conv_ln_b, w_out, ln1_g, ln1_b, router_w, router_b, w_gate_up, b_gate_up, w_down, b_down,
           ln2_g, ln2_b):
    batch, seq, d = x.shape
    n = batch * seq
    assert n % TOKEN_TILE == 0 and seq % ATTN_KV_BLOCK == 0 and seq % CONV_TILE == 0
    assert w_in.shape[0] == DEPTH
    x2 = x.reshape(n, d)
    row = lambda v: v.reshape(1, -1).astype(F32)
    slopes = jnp.array([(2.0 ** (-8.0 / N_HEADS)) ** (i + 1) for i in range(N_HEADS)], dtype=F32)

    h = _inproj(x2, row(emb_ln_g), row(emb_ln_b), w_in[0].astype(BF16))
    a = _attention(h, slopes, row(lambda_q1[0]), row(lambda_k1[0]), row(lambda_q2[0]), row(lambda_k2[0]),
                   row(subln_g[0]), batch=batch, seq=seq)
    c = _conv_module(h, conv_w[0], row(conv_b[0]), row(conv_ln_g[0]), row(conv_ln_b[0]), batch=batch, seq=seq)
    x1, xs_local, pos_t, gates_t, cc = _mix(
        x2, a, c, w_out[0].astype(BF16), row(emb_ln_g), row(emb_ln_b), row(ln1_g[0]), row(ln1_b[0]),
        router_w[0].T, router_b[0].reshape(N_EXPERTS, 1))

    te, src, dst, nt = _build_schedule(cc[:, :, 0])
    nb = n // TOKEN_TILE
    y_chunks = _moe(xs_local.reshape(nb * CHUNKS_PER_BLOCK, BF16_ROWS, d), te, src, dst, nt,
                    w_gate_up[0], b_gate_up[0].reshape(N_EXPERTS, 1, -1), w_down[0],
                    b_down[0].reshape(N_EXPERTS, 1, -1))
    y_local = y_chunks.reshape(-1, d)
    out = _combine(y_local, jnp.transpose(pos_t, (0, 2, 1)), jnp.transpose(gates_t, (0, 2, 1)), x1,
                   row(ln2_g[0]), row(ln2_b[0]))
    return out.reshape(batch, seq, d)
```

```python
import functools

import jax
import jax.numpy as jnp
from jax import lax
from jax.experimental import pallas as pl
from jax.experimental.pallas import tpu as pltpu

F32 = jnp.float32
BF16 = jnp.bfloat16
I32 = jnp.int32

N_HEADS = 4
HEAD_DIM = 64
V_DIM = 128
ATTN_WIDTH = N_HEADS * V_DIM
CONV_WIDTH = 512
CONV_KERNEL = 31
N_EXPERTS = 32
TOP_K = 4
SWIGLU_LIMIT = 7.0
SWIGLU_ALPHA = 1.702
LN_EPS = 1e-5
DEPTH = 1
DEEPNORM_ALPHA = (2.0 * DEPTH) ** 0.25
LAMBDA_INIT = 0.2
LOG2_E = 1.4426950408889634
Q_COLS = N_HEADS * 2 * HEAD_DIM
Q_SCALE = HEAD_DIM ** -0.5 * LOG2_E

LANES = 128
BF16_ROWS = 16

TOKEN_TILE = 512
ATTN_KV_BLOCK = 256
ATTN_SUB = 128
CONV_TILE = 64
CONV_ROW_STRIDE = 4
CONV_GLU_TILE = 128
MOE_TILE = 512
CHUNKS_PER_TILE = MOE_TILE // BF16_ROWS
LOCAL_ROWS = TOP_K * TOKEN_TILE + N_EXPERTS * BF16_ROWS
CHUNKS_PER_BLOCK = LOCAL_ROWS // BF16_ROWS


def _cparams(semantics, vmem_mib):
    return pltpu.CompilerParams(dimension_semantics=semantics, vmem_limit_bytes=vmem_mib * 1024 * 1024)


def _layer_norm(x, g, b):
    mu = jnp.mean(x, axis=-1, keepdims=True)
    xc = x - mu
    var = jnp.mean(xc * xc, axis=-1, keepdims=True)
    return xc * lax.rsqrt(var + LN_EPS) * g + b


def _sigmoid(x):
    return 1.0 / (1.0 + jnp.exp(-x))


def _inproj_body(x_ref, g_ref, b_ref, w_ref, h_ref, *, n_chunk):
    xn = _layer_norm(x_ref[...], g_ref[...], b_ref[...]).astype(BF16)
    for j in range(w_ref.shape[1] // n_chunk):
        sl = slice(j * n_chunk, (j + 1) * n_chunk)
        hj = jnp.dot(xn, w_ref[:, sl], preferred_element_type=F32)
        if (j + 1) * n_chunk <= Q_COLS:
            hj = hj * Q_SCALE
        h_ref[:, sl] = hj.astype(BF16)


def _inproj(x2, g, b, w_bf16):
    n, d = x2.shape
    cols = w_bf16.shape[1]
    tm = TOKEN_TILE
    return pl.pallas_call(
        functools.partial(_inproj_body, n_chunk=512),
        grid=(n // tm,),
        in_specs=[
            pl.BlockSpec((tm, d), lambda i: (i, 0)),
            pl.BlockSpec((1, d), lambda i: (0, 0)),
            pl.BlockSpec((1, d), lambda i: (0, 0)),
            pl.BlockSpec((d, cols), lambda i: (0, 0)),
        ],
        out_specs=pl.BlockSpec((tm, cols), lambda i: (i, 0)),
        out_shape=jax.ShapeDtypeStruct((n, cols), BF16),
        compiler_params=_cparams(("arbitrary",), 40),
        name="inproj",
    )(x2, g, b, w_bf16)


def _attn_body(slopes_ref, lq1_ref, lk1_ref, lq2_ref, lk2_ref, g_ref, q_ref, k_ref, v_ref, o_ref,
               q2_ref, m_ref, l_ref, acc_ref, *, seq):
    head = pl.program_id(1)
    slope = slopes_ref[head] * LOG2_E
    lam =(jnp.exp(jnp.sum(lq1_ref[...] * lk1_ref[...], axis=-1, keepdims=True))
           - jnp.exp(jnp.sum(lq2_ref[...] * lk2_ref[...], axis=-1, keepdims=True))
           + LAMBDA_INIT)

    q = q_ref[...]
    lane = lax.broadcasted_iota(I32, q.shape, 1)
    zero = jnp.zeros_like(q)
    q2_ref[0:seq, :] = jnp.where(lane < HEAD_DIM, q, zero)
    q2_ref[seq:, :] = jnp.where(lane >= HEAD_DIM, q, zero)
    m_ref[...] = jnp.full(m_ref.shape, -jnp.inf, F32)
    l_ref[...] = jnp.zeros(l_ref.shape, F32)
    acc_ref[...] = jnp.zeros(acc_ref.shape, F32)
    kb_rows = ATTN_KV_BLOCK
    sub = ATTN_SUB
    ones = jnp.ones((kb_rows, V_DIM), BF16)
    col = lax.broadcasted_iota(I32, (1, kb_rows), 1)

    for d in range(seq // kb_rows):
        k0 = d * kb_rows
        kb = k_ref[k0:k0 + kb_rows, :]
        vext = jnp.concatenate([v_ref[k0:k0 + kb_rows, :], ones], axis=1)
        for half in range(2):
            for i in range(k0 // sub, seq // sub):
                r0 = i * sub
                rows = slice(half * seq + r0, half * seq + r0 + sub)
                ncol = min(kb_rows, -(-(r0 + sub - k0) // LANES) * LANES)
                bias = (col[:, 0:ncol] + (k0 - r0)).astype(F32) * slope
                s = lax.dot_general(q2_ref[rows, :], kb[0:ncol, :], (((1,), (1,)), ((), ())),
                                    preferred_element_type=F32) + bias
                if k0 + ncol - 1 > r0:
                    row = lax.broadcasted_iota(I32, s.shape, 0) + (r0 - k0)
                    s = jnp.where(lax.broadcasted_iota(I32, s.shape, 1) <= row, s, -jnp.inf)
                m_prev = m_ref[rows, :]
                m_new = jnp.maximum(m_prev, jnp.max(s, axis=1, keepdims=True))
                alpha = jnp.exp2(m_prev - m_new)
                p = jnp.exp2(s - jnp.concatenate([m_new] * (ncol // LANES), axis=1))
                pv = jnp.dot(p.astype(BF16), vext[0:ncol, :], preferred_element_type=F32)
                acc_ref[rows, :] = alpha * acc_ref[rows, :] + pv[:, :V_DIM]
                l_ref[rows, :] = alpha * l_ref[rows, :] + pv[:, V_DIM:]
                m_ref[rows, :] = m_new

    o = acc_ref[...] / l_ref[...]
    od = o[:seq] - lam * o[seq:]
    ms = jnp.mean(od * od, axis=-1, keepdims=True)
    out = od * lax.rsqrt(ms + LN_EPS) * g_ref[...] * (1.0 - LAMBDA_INIT)
    o_ref[...] = out.astype(BF16)


def _attention(h, slopes, lq1, lk1, lq2, lk2, subln_g, *, batch, seq):
    n = h.shape[0]
    vec = lambda: pl.BlockSpec((1, HEAD_DIM), lambda b, hd, *_: (0, 0))
    grid_spec = pltpu.PrefetchScalarGridSpec(
        num_scalar_prefetch=1,
        grid=(batch, N_HEADS),
        in_specs=[
            vec(), vec(), vec(), vec(),
            pl.BlockSpec((1, V_DIM), lambda b, hd, *_: (0, 0)),
            pl.BlockSpec((seq, V_DIM), lambda b, hd, *_: (b, hd)),
            pl.BlockSpec((seq, V_DIM), lambda b, hd, *_: (b, N_HEADS + hd)),
            pl.BlockSpec((seq, V_DIM), lambda b, hd, *_: (b, 2 * N_HEADS + hd)),
        ],
        out_specs=pl.BlockSpec((seq, V_DIM), lambda b, hd, *_: (b, hd)),
        scratch_shapes=[
            pltpu.VMEM((2 * seq, V_DIM), BF16),
            pltpu.VMEM((2 * seq, LANES), F32),
            pltpu.VMEM((2 * seq, LANES), F32),
            pltpu.VMEM((2 * seq, V_DIM), F32),
        ],
    )
    return pl.pallas_call(
        functools.partial(_attn_body, seq=seq),
        grid_spec=grid_spec,
        out_shape=jax.ShapeDtypeStruct((n, ATTN_WIDTH), BF16),
        compiler_params=_cparams(("arbitrary", "arbitrary"), 32),
        name="diff_attn",
    )(slopes, lq1, lk1, lq2, lk2, subln_g, h, h, h)


CONV_PAD = 32


def _conv_body(a_ref, g_ref, w_ref, cb_ref, lg_ref, lb_ref, o_ref, glu_ref, conv_ref, *, seq):
    n_slab = CONV_WIDTH // LANES
    for c in range(n_slab):
        glu_ref[c, 0:CONV_PAD, :] = jnp.zeros((CONV_PAD, LANES), F32)

    def glu(i, _):
        st = pl.multiple_of(i * CONV_GLU_TILE, CONV_GLU_TILE)
        a = a_ref[pl.ds(st, CONV_GLU_TILE), :].astype(F32)
        g = g_ref[pl.ds(st, CONV_GLU_TILE), :].astype(F32)
        val = a * _sigmoid(g)
        for c in range(n_slab):
            glu_ref[c, pl.ds(CONV_PAD + st, CONV_GLU_TILE), :] = val[:, c * LANES:(c + 1) * LANES]
        return 0

    lax.fori_loop(0, seq // CONV_GLU_TILE, glu, 0)

    span = 8 * CONV_ROW_STRIDE
    for c in range(n_slab):
        lanes = slice(c * LANES, (c + 1) * LANES)
        taps = [jnp.broadcast_to(w_ref[j:j + 1, lanes], (8, LANES)) for j in range(CONV_KERNEL)]
        bias = jnp.broadcast_to(cb_ref[:, lanes], (8, LANES))

        def conv(i, _):
            st = i * CONV_TILE
            for grp in range(CONV_TILE // span):
                for p in range(CONV_ROW_STRIDE):
                    t0 = st + grp * span + p
                    acc = bias
                    for j in range(CONV_KERNEL):
                        src = pl.ds(t0 + (CONV_PAD - (CONV_KERNEL - 1) + j), 8, stride=CONV_ROW_STRIDE)
                        acc = acc + glu_ref[c, src, :] * taps[j]
                    conv_ref[c, pl.ds(t0, 8, stride=CONV_ROW_STRIDE), :] = acc
            return 0

        lax.fori_loop(0, seq // CONV_TILE, conv, 0)

    def norm(i, _):
        st = pl.multiple_of(i * CONV_GLU_TILE, CONV_GLU_TILE)
        x = jnp.concatenate([conv_ref[c, pl.ds(st, CONV_GLU_TILE), :] for c in range(n_slab)], axis=1)
        y = _layer_norm(x, lg_ref[...], lb_ref[...])
        o_ref[pl.ds(st, CONV_GLU_TILE), :] = (y * _sigmoid(y)).astype(BF16)
        return 0

    lax.fori_loop(0, seq // CONV_GLU_TILE, norm, 0, unroll=2)


def _conv_module(h, conv_w, conv_b, ln_g, ln_b, *, batch, seq):
    n = h.shape[0]
    u_col = 3 * ATTN_WIDTH // CONV_WIDTH
    row = lambda: pl.BlockSpec((1, CONV_WIDTH), lambda b: (0, 0))
    return pl.pallas_call(
        functools.partial(_conv_body, seq=seq),
        grid=(batch,),
        in_specs=[
            pl.BlockSpec((seq, CONV_WIDTH), lambda b: (b, u_col)),
            pl.BlockSpec((seq, CONV_WIDTH), lambda b: (b, u_col + 1)),
            pl.BlockSpec((CONV_KERNEL, CONV_WIDTH), lambda b: (0, 0)),
            row(), row(), row(),
        ],
        out_specs=pl.BlockSpec((seq, CONV_WIDTH), lambda b: (b, 0)),
        out_shape=jax.ShapeDtypeStruct((n, CONV_WIDTH), BF16),
        scratch_shapes=[
            pltpu.VMEM((CONV_WIDTH // LANES, CONV_PAD + seq, LANES), F32),
            pltpu.VMEM((CONV_WIDTH // LANES, seq, LANES), F32),
        ],
        compiler_params=_cparams(("arbitrary",), 32),
        name="conformer_conv",
    )(h, h, conv_w, conv_b, ln_g, ln_b)


def _mix_body(x_ref, a_ref, c_ref, wo_ref, g0_ref, b0_ref, g1_ref, b1_ref, rwt_ref, rb_ref,
              x1_ref, xs_ref, pos_ref, gate_ref, cc_ref):
    tb = x_ref.shape[0]
    xn = _layer_norm(x_ref[...], g0_ref[...], b0_ref[...])
    mix = (jnp.dot(a_ref[...], wo_ref[0:ATTN_WIDTH, :], preferred_element_type=F32)
           + jnp.dot(c_ref[...], wo_ref[ATTN_WIDTH:, :], preferred_element_type=F32))
    x1 = _layer_norm(DEEPNORM_ALPHA * xn + mix, g1_ref[...], b1_ref[...])
    x1_ref[...] = x1
    x1b = x1.astype(BF16)

    x_lo = (x1 - x1b.astype(F32)).astype(BF16)
    rw = rwt_ref[...]
    rw_hi = rw.astype(BF16)
    rw_lo = (rw - rw_hi.astype(F32)).astype(BF16)
    nt_dims = (((1,), (1,)), ((), ()))
    both = lax.dot_general(jnp.concatenate([rw_hi, rw_lo], axis=0), x1b, nt_dims, preferred_element_type=F32)
    logits = ((lax.dot_general(rw_hi, x_lo, nt_dims, preferred_element_type=F32) + both[N_EXPERTS:])
              + both[:N_EXPERTS]) + rb_ref[...]
    eidx = lax.broadcasted_iota(I32, logits.shape, 0)
    work = logits
    vals, hots = [], []
    for _ in range(TOP_K):
        mx = jnp.max(work, axis=0, keepdims=True)
        sel = jnp.min(jnp.where(work == mx, eidx, N_EXPERTS), axis=0, keepdims=True)
        hot = eidx == sel
        vals.append(mx)
        hots.append(hot)
        work = jnp.where(hot, -jnp.inf, work)
    ex = [jnp.exp(v - vals[0]) for v in vals]
    den = ex[0] + ex[1] + ex[2] + ex[3]
    for k in range(TOP_K):
        gate_ref[k:k + 1, :] = ex[k] / den

    member = jnp.zeros(logits.shape, F32)
    for hot in hots:
        member = member + jnp.where(hot, 1.0, 0.0)
    before = (lax.broadcasted_iota(I32, (tb, tb), 0) < lax.broadcasted_iota(I32, (tb, tb), 1))
    rank = jnp.dot(member.astype(BF16), jnp.where(before, 1.0, 0.0).astype(BF16), preferred_element_type=F32)
    count = jnp.sum(member, axis=1, keepdims=True)
    chunks = jnp.floor((count + (BF16_ROWS - 1.0)) * (1.0 / BF16_ROWS))
    cc_ref[...] = jnp.broadcast_to(chunks, cc_ref.shape).astype(I32)
    lower = (lax.broadcasted_iota(I32, (N_EXPERTS, N_EXPERTS), 1) < lax.broadcasted_iota(I32, (N_EXPERTS, N_EXPERTS), 0))
    seg_start = jnp.dot(jnp.where(lower, 1.0, 0.0).astype(BF16),
                        jnp.broadcast_to(chunks, (N_EXPERTS, LANES)).astype(BF16),
                        preferred_element_type=F32)[:, 0:1]
    slot = seg_start * float(BF16_ROWS) + rank
    pos = [jnp.sum(jnp.where(hot, slot, 0.0), axis=0, keepdims=True).astype(I32) for hot in hots]
    for k in range(TOP_K):
        pos_ref[k:k + 1, :] = pos[k]

    rows = 256
    one, nil = jnp.ones((), BF16), jnp.zeros((), BF16)
    pos16 = [p.astype(jnp.int16) for p in pos]
    for rc in range(xs_ref.shape[0] // rows):
        r = lax.broadcasted_iota(jnp.int16, (rows, tb), 0) + jnp.int16(rc * rows)
        hit = (r == pos16[0]) | (r == pos16[1]) | (r == pos16[2]) | (r == pos16[3])
        sel = jnp.where(hit, one, nil)
        xs_ref[rc * rows:(rc + 1) * rows, :] = jnp.dot(sel, x1b, preferred_element_type=F32).astype(BF16)


def _mix(x2, a, c, wo_bf16, g0, b0, g1, b1, rwt, rb):
    n, d = x2.shape
    tb = TOKEN_TILE
    nb = n // tb
    row = lambda: pl.BlockSpec((1, d), lambda i: (0, 0))
    return pl.pallas_call(
        _mix_body,
        grid=(nb,),
        in_specs=[
            pl.BlockSpec((tb, d), lambda i: (i, 0)),
            pl.BlockSpec((tb, ATTN_WIDTH), lambda i: (i, 0)),
            pl.BlockSpec((tb, CONV_WIDTH), lambda i: (i, 0)),
            pl.BlockSpec((d, d), lambda i: (0, 0)),
            row(), row(), row(), row(),
            pl.BlockSpec((N_EXPERTS, d), lambda i: (0, 0)),
            pl.BlockSpec((N_EXPERTS, 1), lambda i: (0, 0)),
        ],
        out_specs=[
            pl.BlockSpec((tb, d), lambda i: (i, 0)),
            pl.BlockSpec((None, LOCAL_ROWS, d), lambda i: (i, 0, 0)),
            pl.BlockSpec((None, TOP_K, tb), lambda i: (i, 0, 0)),
            pl.BlockSpec((None, TOP_K, tb), lambda i: (i, 0, 0)),
            pl.BlockSpec((None, N_EXPERTS, LANES), lambda i: (i, 0, 0)),
        ],
        out_shape=[
            jax.ShapeDtypeStruct((n, d), F32),
            jax.ShapeDtypeStruct((nb, LOCAL_ROWS, d), BF16),
            jax.ShapeDtypeStruct((nb, TOP_K, tb), I32),
            jax.ShapeDtypeStruct((nb, TOP_K, tb), F32),
            jax.ShapeDtypeStruct((nb, N_EXPERTS, LANES), I32),
        ],
        compiler_params=_cparams(("arbitrary",), 48),
        name="mix_router_sort",
    )(x2, a, c, wo_bf16, g0, b0, g1, b1, rwt, rb)


def _max_tiles(nb):
    return (nb * CHUNKS_PER_BLOCK + CHUNKS_PER_TILE - 1) // CHUNKS_PER_TILE + N_EXPERTS + 1


def _build_schedule(cc):
    nb = cc.shape[0]
    cpt = CHUNKS_PER_TILE
    ne1 = N_EXPERTS + 1
    tail = CHUNKS_PER_BLOCK - jnp.sum(cc, axis=1)
    cce = jnp.concatenate([cc, tail[:, None]], axis=1)
    seg_start = jnp.cumsum(cce, axis=1) - cce
    tot = jnp.sum(cce, axis=0)
    tiles_e = (tot + cpt - 1) // cpt
    tile_end = jnp.cumsum(tiles_e)
    max_tiles = _max_tiles(nb)

    def lookup(passed, table):
        return table[0] + jnp.sum(jnp.where(passed, (table[1:] - table[:-1])[None, :], 0), axis=1)

    t = jnp.arange(max_tiles, dtype=I32)
    passed = tile_end[None, :] <= t[:, None]
    te = jnp.minimum(jnp.sum(passed, axis=1), ne1 - 1).astype(I32)

    pad_len = tiles_e * cpt - tot
    seg_len = jnp.concatenate([cce.T, pad_len[:, None]], axis=1).reshape(-1)
    seg_end = jnp.cumsum(seg_len)
    block = jnp.arange(nb, dtype=I32)[None, :]
    is_real = jnp.concatenate([jnp.ones((ne1, nb), I32), jnp.zeros((ne1, 1), I32)], axis=1).reshape(-1)
    seg_first = (seg_end - seg_len) * is_real
    zero = jnp.zeros((1,), I32)

    base = block * CHUNKS_PER_BLOCK + seg_start.T
    first = jnp.concatenate([jnp.concatenate([base, jnp.zeros((ne1, 1), I32)], axis=1).reshape(-1) - seg_first,
                             zero])
    p = jnp.arange(max_tiles * cpt, dtype=I32)
    passed = seg_end[None, :] <= p[:, None]
    real = lookup(passed, jnp.concatenate([is_real, zero]))
    src = (real * p + lookup(passed, first)).astype(I32)
    dump0 = nb * CHUNKS_PER_BLOCK
    dst = jnp.where(real > 0, src, dump0 + ((p // cpt) % 2) * cpt + p % cpt).astype(I32)
    dummy = dump0 + cpt + jnp.arange(cpt, dtype=I32)
    nv = jnp.sum(real.reshape(max_tiles, cpt), axis=1).astype(I32)
    return te, nv, src, jnp.concatenate([dummy, dst]), jnp.reshape(tile_end[-1], (1,)).astype(I32)


def _moe_body(te_ref, nv_ref, src_ref, dst_ref, nt_ref, xs_hbm, wgu_ref, bgu_ref, wd_ref, bd_ref, y_hbm,
              xbuf, ybuf, act, wgu_b, wd_b, in_sem, out_sem):
    t = pl.program_id(0)
    last = nt_ref[0] - 1
    cpt = CHUNKS_PER_TILE
    f = wd_ref.shape[0]
    slot = t % 2
    other = 1 - slot
    e = te_ref[t]

    def rows(c):
        return pl.ds(c * BF16_ROWS, BF16_ROWS)

    def in_copy(tile, buf, c):
        return pltpu.make_async_copy(xs_hbm.at[src_ref[tile * cpt + c]], xbuf.at[buf, rows(c)], in_sem.at[buf])

    def out_copy(tile, buf, c):
        return pltpu.make_async_copy(ybuf.at[buf, rows(c)], y_hbm.at[dst_ref[(tile + 1) * cpt + c]],
                                     out_sem.at[buf])

    def start_all(copy, tile, buf):
        for c in range(cpt):
            copy(tile, buf, c).start()

    def wait_all(copy, tile, buf):
        for c in range(cpt):
            copy(tile, buf, c).wait()

    @pl.when(t == 0)
    def _():
        start_all(in_copy, 0, 0)
        ybuf[...] = jnp.zeros(ybuf.shape, BF16)
        even_dump = [pltpu.make_async_copy(ybuf.at[1, rows(c)], y_hbm.at[y_hbm.shape[0] - 2 * cpt + c],
                                           out_sem.at[0]) for c in range(cpt)]
        for copy in even_dump:
            copy.start()
        for copy in even_dump:
            copy.wait()

    @pl.when(t <= last)
    def _():
        @pl.when(t >= 1)
        def _():
            wait_all(out_copy, t - 2, slot)

        wait_all(in_copy, t, slot)
        start_all(in_copy, jnp.minimum(t + 1, last), other)
        start_all(out_copy, t - 1, other)

        @pl.when((e < N_EXPERTS) & ((t == 0) | (te_ref[jnp.maximum(t - 1, 0)] != e)))
        def _():
            wgu_b[...] = wgu_ref[...].astype(BF16)
            wd_b[...] = wd_ref[...].astype(BF16)

        def expert_mlp(m):
            x = xbuf[slot, 0:m, :]
            nc = 512
            for j in range(f // nc):
                gsl = slice(j * nc, (j + 1) * nc)
                usl = slice(f + j * nc, f + (j + 1) * nc)
                g = jnp.dot(x, wgu_b[:, gsl], preferred_element_type=F32) + bgu_ref[:, gsl]
                u = jnp.dot(x, wgu_b[:, usl], preferred_element_type=F32) + bgu_ref[:, usl]
                g = jnp.minimum(g, SWIGLU_LIMIT)
                u = jnp.clip(u, -SWIGLU_LIMIT, SWIGLU_LIMIT)
                act[0:m, gsl] = (g * _sigmoid(SWIGLU_ALPHA * g) * (u + 1.0)).astype(BF16)
            a = act[0:m, :]
            for j in range(wd_ref.shape[1] // nc):
                sl = slice(j * nc, (j + 1) * nc)
                y = jnp.dot(a, wd_b[:, sl], preferred_element_type=F32) + bd_ref[:, sl]
                ybuf[slot, 0:m, sl] = y.astype(BF16)

        few = nv_ref[t] * BF16_ROWS <= MOE_TILE // 2

        @pl.when((e < N_EXPERTS) & jnp.logical_not(few))
        def _():
            expert_mlp(MOE_TILE)

        @pl.when((e < N_EXPERTS) & few)
        def _():
            expert_mlp(MOE_TILE // 2)

        @pl.when(e >= N_EXPERTS)
        def _():
            ybuf[slot] = jnp.zeros(ybuf.shape[1:], BF16)

        @pl.when(t == last)
        def _():
            start_all(out_copy, t, slot)
            wait_all(out_copy, t - 1, other)
            wait_all(out_copy, t, slot)
            wait_all(in_copy, t, other)


def _moe(xs_chunks, te, nv, src, dst, nt, w_gate_up, b_gate_up, w_down, b_down):
    nch, _, d = xs_chunks.shape
    f = w_down.shape[1]
    max_tiles = te.shape[0]
    wsel = lambda t, te_r, nv_r, src_r, dst_r, nt_r: (jnp.minimum(te_r[t], N_EXPERTS - 1), 0, 0)
    grid_spec = pltpu.PrefetchScalarGridSpec(
        num_scalar_prefetch=5,
        grid=(max_tiles,),
        in_specs=[
            pl.BlockSpec(memory_space=pl.ANY),
            pl.BlockSpec((None, d, 2 * f), wsel),
            pl.BlockSpec((None, 1, 2 * f), wsel),
            pl.BlockSpec((None, f, d), wsel),
            pl.BlockSpec((None, 1, d), wsel),
        ],
        out_specs=pl.BlockSpec(memory_space=pl.ANY),
        scratch_shapes=[
            pltpu.VMEM((2, MOE_TILE, d), BF16),
            pltpu.VMEM((2, MOE_TILE, d), BF16),
            pltpu.VMEM((MOE_TILE, f), BF16),
            pltpu.VMEM((d, 2 * f), BF16),
            pltpu.VMEM((f, d), BF16),
            pltpu.SemaphoreType.DMA((2,)),
            pltpu.SemaphoreType.DMA((2,)),
        ],
    )
    return pl.pallas_call(
        _moe_body,
        grid_spec=grid_spec,
        out_shape=jax.ShapeDtypeStruct((nch + 2 * CHUNKS_PER_TILE, BF16_ROWS, d), BF16),
        compiler_params=_cparams(("arbitrary",), 56),
        name="moe_grouped",
    )(te, nv, src, dst, nt, xs_chunks, w_gate_up, b_gate_up, w_down, b_down)


def _combine_body(y_ref, pos_ref, gate_ref, x1_ref, g_ref, b_ref, o_ref):
    tb = x1_ref.shape[0]
    pos = pos_ref[...]
    gate = gate_ref[...]
    cols = 512
    acc = jnp.zeros(x1_ref.shape, F32)
    for rc in range(y_ref.shape[0] // cols):
        r = lax.broadcasted_iota(I32, (tb, cols), 1) + rc * cols
        w = jnp.zeros((tb, cols), F32)
        for k in range(TOP_K):
            w = w + jnp.where(r == pos[:, k:k + 1], gate[:, k:k + 1], 0.0)
        acc = acc + jnp.dot(w.astype(BF16), y_ref[rc * cols:(rc + 1) * cols, :], preferred_element_type=F32)
    o_ref[...] = _layer_norm(DEEPNORM_ALPHA * x1_ref[...] + acc, g_ref[...], b_ref[...])


def _combine(y_local, pos, gates, x1, g, b):
    n, d = x1.shape
    tb = TOKEN_TILE
    nb = n // tb
    row = lambda: pl.BlockSpec((1, d), lambda i: (0, 0))
    return pl.pallas_call(
        _combine_body,
        grid=(nb,),
        in_specs=[
            pl.BlockSpec((LOCAL_ROWS, d), lambda i: (i, 0)),
            pl.BlockSpec((None, tb, TOP_K), lambda i: (i, 0, 0)),
            pl.BlockSpec((None, tb, TOP_K), lambda i: (i, 0, 0)),
            pl.BlockSpec((tb, d), lambda i: (i, 0)),
            row(), row(),
        ],
        out_specs=pl.BlockSpec((tb, d), lambda i: (i, 0)),
        out_shape=jax.ShapeDtypeStruct((n, d), F32),
        compiler_params=_cparams(("arbitrary",), 40),
        name="combine_ln",
    )(y_local, pos, gates, x1, g, b)


def kernel(x, emb_ln_g, emb_ln_b, w_in, lambda_q1, lambda_k1, lambda_q2, lambda_k2, subln_g, conv_w, conv_b,
           conv_ln_g, conv_ln_b, w_out, ln1_g, ln1_b, router_w, router_b, w_gate_up, b_gate_up, w_down, b_down,
           ln2_g, ln2_b):
    batch, seq, d = x.shape
    n = batch * seq
    assert n % TOKEN_TILE == 0 and seq % ATTN_KV_BLOCK == 0 and seq % CONV_TILE == 0
    assert w_in.shape[0] == DEPTH
    x2 = x.reshape(n, d)
    row = lambda v: v.reshape(1, -1).astype(F32)
    slopes = jnp.array([(2.0 ** (-8.0 / N_HEADS)) ** (i + 1) for i in range(N_HEADS)], dtype=F32)

    h = _inproj(x2, row(emb_ln_g), row(emb_ln_b), w_in[0].astype(BF16))
    a = _attention(h, slopes, row(lambda_q1[0]), row(lambda_k1[0]), row(lambda_q2[0]), row(lambda_k2[0]),
                   row(subln_g[0]), batch=batch, seq=seq)
    c = _conv_module(h, conv_w[0], row(conv_b[0]), row(conv_ln_g[0]), row(conv_ln_b[0]), batch=batch, seq=seq)
    x1, xs_local, pos_t, gates_t, cc = _mix(
        x2, a, c, w_out[0].astype(BF16), row(emb_ln_g), row(emb_ln_b), row(ln1_g[0]), row(ln1_b[0]),
        router_w[0].T, router_b[0].reshape(N_EXPERTS, 1))

    te, nv, src, dst, nt = _build_schedule(cc[:, :, 0])
    nb = n // TOKEN_TILE
    y_chunks = _moe(xs_local.reshape(nb * CHUNKS_PER_BLOCK, BF16_ROWS, d), te, nv, src, dst, nt,
                    w_gate_up[0], b_gate_up[0].reshape(N_EXPERTS, 1, -1), w_down[0],
                    b_down[0].reshape(N_EXPERTS, 1, -1))
    y_local = y_chunks.reshape(-1, d)
    out = _combine(y_local, jnp.transpose(pos_t, (0, 2, 1)), jnp.transpose(gates_t, (0, 2, 1)), x1,
                   row(ln2_g[0]), row(ln2_b[0]))
    return out.reshape(batch, seq, d)
```

```python
import functools

import jax
import jax.numpy as jnp
from jax import lax
from jax.experimental import pallas as pl
from jax.experimental.pallas import tpu as pltpu

F32 = jnp.float32
BF16 = jnp.bfloat16
I32 = jnp.int32

N_HEADS = 4
HEAD_DIM = 64
V_DIM = 128
ATTN_WIDTH = N_HEADS * V_DIM
CONV_WIDTH = 512
CONV_KERNEL = 31
N_EXPERTS = 32
TOP_K = 4
SWIGLU_LIMIT = 7.0
SWIGLU_ALPHA = 1.702
LN_EPS = 1e-5
DEPTH = 1
DEEPNORM_ALPHA = (2.0 * DEPTH) ** 0.25
LAMBDA_INIT = 0.2
LOG2_E = 1.4426950408889634
Q_COLS = N_HEADS * 2 * HEAD_DIM
Q_SCALE = HEAD_DIM ** -0.5 * LOG2_E

LANES = 128
BF16_ROWS = 16

TOKEN_TILE = 512
ATTN_KV_BLOCK = 256
ATTN_SUB = 256
CONV_TILE = 64
CONV_ROW_STRIDE = 4
CONV_GLU_TILE = 128
MOE_TILE = 512
CHUNKS_PER_TILE = MOE_TILE // BF16_ROWS
LOCAL_ROWS = TOP_K * TOKEN_TILE + N_EXPERTS * BF16_ROWS
CHUNKS_PER_BLOCK = LOCAL_ROWS // BF16_ROWS


def _cparams(semantics, vmem_mib):
    return pltpu.CompilerParams(dimension_semantics=semantics, vmem_limit_bytes=vmem_mib * 1024 * 1024)


def _layer_norm(x, g, b):
    mu = jnp.mean(x, axis=-1, keepdims=True)
    xc = x - mu
    var = jnp.mean(xc * xc, axis=-1, keepdims=True)
    return xc * lax.rsqrt(var + LN_EPS) * g + b


def _sigmoid(x):
    return 1.0 / (1.0 + jnp.exp(-x))


def _inproj_body(x_ref, g_ref, b_ref, w_ref, h_ref, *, n_chunk):
    xn = _layer_norm(x_ref[...], g_ref[...], b_ref[...]).astype(BF16)
    for j in range(w_ref.shape[1] // n_chunk):
        sl = slice(j * n_chunk, (j + 1) * n_chunk)
        hj = jnp.dot(xn, w_ref[:, sl], preferred_element_type=F32)
        if (j + 1) * n_chunk <= Q_COLS:
            hj = hj * Q_SCALE
        h_ref[:, sl] = hj.astype(BF16)


def _inproj(x2, g, b, w_bf16):
    n, d = x2.shape
    cols = w_bf16.shape[1]
    tm = TOKEN_TILE
    return pl.pallas_call(
        functools.partial(_inproj_body, n_chunk=512),
        grid=(n // tm,),
        in_specs=[
            pl.BlockSpec((tm, d), lambda i: (i, 0)),
            pl.BlockSpec((1, d), lambda i: (0, 0)),
            pl.BlockSpec((1, d), lambda i: (0, 0)),
            pl.BlockSpec((d, cols), lambda i: (0, 0)),
        ],
        out_specs=pl.BlockSpec((tm, cols), lambda i: (i, 0)),
        out_shape=jax.ShapeDtypeStruct((n, cols), BF16),
        compiler_params=_cparams(("arbitrary",), 40),
        name="inproj",
    )(x2, g, b, w_bf16)


def _attn_body(slopes_ref, lq1_ref, lk1_ref, lq2_ref, lk2_ref, g_ref, q_ref, k_ref, v_ref, o_ref,
               q2_ref, m_ref, l_ref, acc_ref, *, seq):
    head = pl.program_id(1)
    slope = slopes_ref[head] * LOG2_E
    lam =(jnp.exp(jnp.sum(lq1_ref[...] * lk1_ref[...], axis=-1, keepdims=True))
           - jnp.exp(jnp.sum(lq2_ref[...] * lk2_ref[...], axis=-1, keepdims=True))
           + LAMBDA_INIT)

    q = q_ref[...]
    lane = lax.broadcasted_iota(I32, q.shape, 1)
    zero = jnp.zeros_like(q)
    q2_ref[0:seq, :] = jnp.where(lane < HEAD_DIM, q, zero)
    q2_ref[seq:, :] = jnp.where(lane >= HEAD_DIM, q, zero)
    m_ref[...] = jnp.full(m_ref.shape, -jnp.inf, F32)
    l_ref[...] = jnp.zeros(l_ref.shape, F32)
    acc_ref[...] = jnp.zeros(acc_ref.shape, F32)
    kb_rows = ATTN_KV_BLOCK
    sub = ATTN_SUB
    ones = jnp.ones((kb_rows, V_DIM), BF16)
    col = lax.broadcasted_iota(I32, (1, kb_rows), 1)

    for d in range(seq // kb_rows):
        k0 = d * kb_rows
        kb = k_ref[k0:k0 + kb_rows, :]
        vext = jnp.concatenate([v_ref[k0:k0 + kb_rows, :], ones], axis=1)
        for half in range(2):
            for i in range(k0 // sub, seq // sub):
                r0 = i * sub
                rows = slice(half * seq + r0, half * seq + r0 + sub)
                ncol = min(kb_rows, -(-(r0 + sub - k0) // LANES) * LANES)
                bias = (col[:, 0:ncol] + (k0 - r0)).astype(F32) * slope
                s = lax.dot_general(q2_ref[rows, :], kb[0:ncol, :], (((1,), (1,)), ((), ())),
                                    preferred_element_type=F32) + bias
                if k0 + ncol - 1 > r0:
                    row = lax.broadcasted_iota(I32, s.shape, 0) + (r0 - k0)
                    s = jnp.where(lax.broadcasted_iota(I32, s.shape, 1) <= row, s, -jnp.inf)
                m_prev = m_ref[rows, :]
                m_new = jnp.maximum(m_prev, jnp.max(s, axis=1, keepdims=True))
                alpha = jnp.exp2(m_prev - m_new)
                p = jnp.exp2(s - jnp.concatenate([m_new] * (ncol // LANES), axis=1))
                pv = jnp.dot(p.astype(BF16), vext[0:ncol, :], preferred_element_type=F32)
                acc_ref[rows, :] = alpha * acc_ref[rows, :] + pv[:, :V_DIM]
                l_ref[rows, :] = alpha * l_ref[rows, :] + pv[:, V_DIM:]
                m_ref[rows, :] = m_new

    o = acc_ref[...] / l_ref[...]
    od = o[:seq] - lam * o[seq:]
    ms = jnp.mean(od * od, axis=-1, keepdims=True)
    out = od * lax.rsqrt(ms + LN_EPS) * g_ref[...] * (1.0 - LAMBDA_INIT)
    o_ref[...] = out.astype(BF16)


def _attention(h, slopes, lq1, lk1, lq2, lk2, subln_g, *, batch, seq):
    n = h.shape[0]
    vec = lambda: pl.BlockSpec((1, HEAD_DIM), lambda b, hd, *_: (0, 0))
    grid_spec = pltpu.PrefetchScalarGridSpec(
        num_scalar_prefetch=1,
        grid=(batch, N_HEADS),
        in_specs=[
            vec(), vec(), vec(), vec(),
            pl.BlockSpec((1, V_DIM), lambda b, hd, *_: (0, 0)),
            pl.BlockSpec((seq, V_DIM), lambda b, hd, *_: (b, hd)),
            pl.BlockSpec((seq, V_DIM), lambda b, hd, *_: (b, N_HEADS + hd)),
            pl.BlockSpec((seq, V_DIM), lambda b, hd, *_: (b, 2 * N_HEADS + hd)),
        ],
        out_specs=pl.BlockSpec((seq, V_DIM), lambda b, hd, *_: (b, hd)),
        scratch_shapes=[
            pltpu.VMEM((2 * seq, V_DIM), BF16),
            pltpu.VMEM((2 * seq, LANES), F32),
            pltpu.VMEM((2 * seq, LANES), F32),
            pltpu.VMEM((2 * seq, V_DIM), F32),
        ],
    )
    return pl.pallas_call(
        functools.partial(_attn_body, seq=seq),
        grid_spec=grid_spec,
        out_shape=jax.ShapeDtypeStruct((n, ATTN_WIDTH), BF16),
        compiler_params=_cparams(("arbitrary", "arbitrary"), 32),
        name="diff_attn",
    )(slopes, lq1, lk1, lq2, lk2, subln_g, h, h, h)


CONV_PAD = 32


def _conv_body(a_ref, g_ref, w_ref, cb_ref, lg_ref, lb_ref, o_ref, glu_ref, conv_ref, *, seq):
    n_slab = CONV_WIDTH // LANES
    for c in range(n_slab):
        glu_ref[c, 0:CONV_PAD, :] = jnp.zeros((CONV_PAD, LANES), F32)

    def glu(i, _):
        st = pl.multiple_of(i * CONV_GLU_TILE, CONV_GLU_TILE)
        a = a_ref[pl.ds(st, CONV_GLU_TILE), :].astype(F32)
        g = g_ref[pl.ds(st, CONV_GLU_TILE), :].astype(F32)
        val = a * _sigmoid(g)
        for c in range(n_slab):
            glu_ref[c, pl.ds(CONV_PAD + st, CONV_GLU_TILE), :] = val[:, c * LANES:(c + 1) * LANES]
        return 0

    lax.fori_loop(0, seq // CONV_GLU_TILE, glu, 0)

    span = 8 * CONV_ROW_STRIDE
    for c in range(n_slab):
        lanes = slice(c * LANES, (c + 1) * LANES)
        taps = [jnp.broadcast_to(w_ref[j:j + 1, lanes], (8, LANES)) for j in range(CONV_KERNEL)]
        bias = jnp.broadcast_to(cb_ref[:, lanes], (8, LANES))

        def conv(i, _):
            st = i * CONV_TILE
            for grp in range(CONV_TILE // span):
                for p in range(CONV_ROW_STRIDE):
                    t0 = st + grp * span + p
                    acc = bias
                    for j in range(CONV_KERNEL):
                        src = pl.ds(t0 + (CONV_PAD - (CONV_KERNEL - 1) + j), 8, stride=CONV_ROW_STRIDE)
                        acc = acc + glu_ref[c, src, :] * taps[j]
                    conv_ref[c, pl.ds(t0, 8, stride=CONV_ROW_STRIDE), :] = acc
            return 0

        lax.fori_loop(0, seq // CONV_TILE, conv, 0)

    def norm(i, _):
        st = pl.multiple_of(i * CONV_GLU_TILE, CONV_GLU_TILE)
        x = jnp.concatenate([conv_ref[c, pl.ds(st, CONV_GLU_TILE), :] for c in range(n_slab)], axis=1)
        y = _layer_norm(x, lg_ref[...], lb_ref[...])
        o_ref[pl.ds(st, CONV_GLU_TILE), :] = (y * _sigmoid(y)).astype(BF16)
        return 0

    lax.fori_loop(0, seq // CONV_GLU_TILE, norm, 0, unroll=2)


def _conv_module(h, conv_w, conv_b, ln_g, ln_b, *, batch, seq):
    n = h.shape[0]
    u_col = 3 * ATTN_WIDTH // CONV_WIDTH
    row = lambda: pl.BlockSpec((1, CONV_WIDTH), lambda b: (0, 0))
    return pl.pallas_call(
        functools.partial(_conv_body, seq=seq),
        grid=(batch,),
        in_specs=[
            pl.BlockSpec((seq, CONV_WIDTH), lambda b: (b, u_col)),
            pl.BlockSpec((seq, CONV_WIDTH), lambda b: (b, u_col + 1)),
            pl.BlockSpec((CONV_KERNEL, CONV_WIDTH), lambda b: (0, 0)),
            row(), row(), row(),
        ],
        out_specs=pl.BlockSpec((seq, CONV_WIDTH), lambda b: (b, 0)),
        out_shape=jax.ShapeDtypeStruct((n, CONV_WIDTH), BF16),
        scratch_shapes=[
            pltpu.VMEM((CONV_WIDTH // LANES, CONV_PAD + seq, LANES), F32),
            pltpu.VMEM((CONV_WIDTH // LANES, seq, LANES), F32),
        ],
        compiler_params=_cparams(("arbitrary",), 32),
        name="conformer_conv",
    )(h, h, conv_w, conv_b, ln_g, ln_b)


def _mix_body(x_ref, a_ref, c_ref, wo_ref, g0_ref, b0_ref, g1_ref, b1_ref, rwt_ref, rb_ref,
              x1_ref, xs_ref, pos_ref, gate_ref, cc_ref):
    tb = x_ref.shape[0]
    xn = _layer_norm(x_ref[...], g0_ref[...], b0_ref[...])
    mix = (jnp.dot(a_ref[...], wo_ref[0:ATTN_WIDTH, :], preferred_element_type=F32)
           + jnp.dot(c_ref[...], wo_ref[ATTN_WIDTH:, :], preferred_element_type=F32))
    x1 = _layer_norm(DEEPNORM_ALPHA * xn + mix, g1_ref[...], b1_ref[...])
    x1_ref[...] = x1
    x1b = x1.astype(BF16)

    x_lo = (x1 - x1b.astype(F32)).astype(BF16)
    rw = rwt_ref[...]
    rw_hi = rw.astype(BF16)
    rw_lo = (rw - rw_hi.astype(F32)).astype(BF16)
    nt_dims = (((1,), (1,)), ((), ()))
    both = lax.dot_general(jnp.concatenate([rw_hi, rw_lo], axis=0), x1b, nt_dims, preferred_element_type=F32)
    logits = ((lax.dot_general(rw_hi, x_lo, nt_dims, preferred_element_type=F32) + both[N_EXPERTS:])
              + both[:N_EXPERTS]) + rb_ref[...]
    eidx = lax.broadcasted_iota(I32, logits.shape, 0)
    work = logits
    vals, hots = [], []
    for _ in range(TOP_K):
        mx = jnp.max(work, axis=0, keepdims=True)
        sel = jnp.min(jnp.where(work == mx, eidx, N_EXPERTS), axis=0, keepdims=True)
        hot = eidx == sel
        vals.append(mx)
        hots.append(hot)
        work = jnp.where(hot, -jnp.inf, work)
    ex = [jnp.exp(v - vals[0]) for v in vals]
    den = ex[0] + ex[1] + ex[2] + ex[3]
    for k in range(TOP_K):
        gate_ref[k:k + 1, :] = ex[k] / den

    member = jnp.zeros(logits.shape, F32)
    for hot in hots:
        member = member + jnp.where(hot, 1.0, 0.0)
    before = (lax.broadcasted_iota(I32, (tb, tb), 0) < lax.broadcasted_iota(I32, (tb, tb), 1))
    rank = jnp.dot(member.astype(BF16), jnp.where(before, 1.0, 0.0).astype(BF16), preferred_element_type=F32)
    count = jnp.sum(member, axis=1, keepdims=True)
    chunks = jnp.floor((count + (BF16_ROWS - 1.0)) * (1.0 / BF16_ROWS))
    cc_ref[...] = jnp.broadcast_to(chunks, cc_ref.shape).astype(I32)
    lower = (lax.broadcasted_iota(I32, (N_EXPERTS, N_EXPERTS), 1) < lax.broadcasted_iota(I32, (N_EXPERTS, N_EXPERTS), 0))
    seg_start = jnp.dot(jnp.where(lower, 1.0, 0.0).astype(BF16),
                        jnp.broadcast_to(chunks, (N_EXPERTS, LANES)).astype(BF16),
                        preferred_element_type=F32)[:, 0:1]
    slot = seg_start * float(BF16_ROWS) + rank
    pos = [jnp.sum(jnp.where(hot, slot, 0.0), axis=0, keepdims=True).astype(I32) for hot in hots]
    for k in range(TOP_K):
        pos_ref[k:k + 1, :] = pos[k]

    rows = 256
    one, nil = jnp.ones((), BF16), jnp.zeros((), BF16)
    pos16 = [p.astype(jnp.int16) for p in pos]
    for rc in range(xs_ref.shape[0] // rows):
        r = lax.broadcasted_iota(jnp.int16, (rows, tb), 0) + jnp.int16(rc * rows)
        hit = (r == pos16[0]) | (r == pos16[1]) | (r == pos16[2]) | (r == pos16[3])
        sel = jnp.where(hit, one, nil)
        xs_ref[rc * rows:(rc + 1) * rows, :] = jnp.dot(sel, x1b, preferred_element_type=F32).astype(BF16)


def _mix(x2, a, c, wo_bf16, g0, b0, g1, b1, rwt, rb):
    n, d = x2.shape
    tb = TOKEN_TILE
    nb = n // tb
    row = lambda: pl.BlockSpec((1, d), lambda i: (0, 0))
    return pl.pallas_call(
        _mix_body,
        grid=(nb,),
        in_specs=[
            pl.BlockSpec((tb, d), lambda i: (i, 0)),
            pl.BlockSpec((tb, ATTN_WIDTH), lambda i: (i, 0)),
            pl.BlockSpec((tb, CONV_WIDTH), lambda i: (i, 0)),
            pl.BlockSpec((d, d), lambda i: (0, 0)),
            row(), row(), row(), row(),
            pl.BlockSpec((N_EXPERTS, d), lambda i: (0, 0)),
            pl.BlockSpec((N_EXPERTS, 1), lambda i: (0, 0)),
        ],
        out_specs=[
            pl.BlockSpec((tb, d), lambda i: (i, 0)),
            pl.BlockSpec((None, LOCAL_ROWS, d), lambda i: (i, 0, 0)),
            pl.BlockSpec((None, TOP_K, tb), lambda i: (i, 0, 0)),
            pl.BlockSpec((None, TOP_K, tb), lambda i: (i, 0, 0)),
            pl.BlockSpec((None, N_EXPERTS, LANES), lambda i: (i, 0, 0)),
        ],
        out_shape=[
            jax.ShapeDtypeStruct((n, d), F32),
            jax.ShapeDtypeStruct((nb, LOCAL_ROWS, d), BF16),
            jax.ShapeDtypeStruct((nb, TOP_K, tb), I32),
            jax.ShapeDtypeStruct((nb, TOP_K, tb), F32),
            jax.ShapeDtypeStruct((nb, N_EXPERTS, LANES), I32),
        ],
        compiler_params=_cparams(("arbitrary",), 48),
        name="mix_router_sort",
    )(x2, a, c, wo_bf16, g0, b0, g1, b1, rwt, rb)


def _max_tiles(nb):
    return (nb * CHUNKS_PER_BLOCK + CHUNKS_PER_TILE - 1) // CHUNKS_PER_TILE + N_EXPERTS + 1


def _build_schedule(cc):
    nb = cc.shape[0]
    cpt = CHUNKS_PER_TILE
    ne1 = N_EXPERTS + 1
    tail = CHUNKS_PER_BLOCK - jnp.sum(cc, axis=1)
    cce = jnp.concatenate([cc, tail[:, None]], axis=1)
    seg_start = jnp.cumsum(cce, axis=1) - cce
    tot = jnp.sum(cce, axis=0)
    tiles_e = (tot + cpt - 1) // cpt
    tile_end = jnp.cumsum(tiles_e)
    max_tiles = _max_tiles(nb)

    def lookup(passed, table):
        return table[0] + jnp.sum(jnp.where(passed, (table[1:] - table[:-1])[None, :], 0), axis=1)

    t = jnp.arange(max_tiles, dtype=I32)
    passed = tile_end[None, :] <= t[:, None]
    te = jnp.minimum(jnp.sum(passed, axis=1), ne1 - 1).astype(I32)
    experts = jnp.arange(ne1, dtype=I32)
    has_tiles = (tiles_e > 0) & (experts < N_EXPERTS)
    later = jnp.where(has_tiles[None, :] & (experts[None, :] > te[:, None]), experts[None, :], ne1)
    nxt = jnp.min(later, axis=1)
    own = jnp.where(te < N_EXPERTS, te, jnp.max(jnp.where(has_tiles, experts, 0)))
    is_first = jnp.any(((tile_end - tiles_e)[None, :] == t[:, None]) & (tiles_e > 0)[None, :], axis=1)
    wsel = jnp.where(is_first | (nxt >= N_EXPERTS), own, nxt).astype(I32)

    pad_len = tiles_e * cpt - tot
    seg_len = jnp.concatenate([cce.T, pad_len[:, None]], axis=1).reshape(-1)
    seg_end = jnp.cumsum(seg_len)
    block = jnp.arange(nb, dtype=I32)[None, :]
    is_real = jnp.concatenate([jnp.ones((ne1, nb), I32), jnp.zeros((ne1, 1), I32)], axis=1).reshape(-1)
    seg_first = (seg_end - seg_len) * is_real
    zero = jnp.zeros((1,), I32)

    base = block * CHUNKS_PER_BLOCK + seg_start.T
    first = jnp.concatenate([jnp.concatenate([base, jnp.zeros((ne1, 1), I32)], axis=1).reshape(-1) - seg_first,
                             zero])
    p = jnp.arange(max_tiles * cpt, dtype=I32)
    passed = seg_end[None, :] <= p[:, None]
    real = lookup(passed, jnp.concatenate([is_real, zero]))
    src = (real * p + lookup(passed, first)).astype(I32)
    dump0 = nb * CHUNKS_PER_BLOCK
    dst = jnp.where(real > 0, src, dump0 + ((p // cpt) % 2) * cpt + p % cpt).astype(I32)
    dummy = dump0 + cpt + jnp.arange(cpt, dtype=I32)
    nv = jnp.sum(real.reshape(max_tiles, cpt), axis=1).astype(I32)
    return te, wsel, nv, src, jnp.concatenate([dummy, dst]), jnp.reshape(tile_end[-1], (1,)).astype(I32)


def _moe_body(te_ref, wsel_ref, nv_ref, src_ref, dst_ref, nt_ref, xs_hbm, wgu_ref, bgu_ref, wd_ref, bd_ref, y_hbm,
              xbuf, ybuf, act, wgu_b, wd_b, in_sem, out_sem):
    t = pl.program_id(0)
    last = nt_ref[0] - 1
    cpt = CHUNKS_PER_TILE
    f = wd_ref.shape[0]
    slot = t % 2
    other = 1 - slot
    e = te_ref[t]

    def rows(c):
        return pl.ds(c * BF16_ROWS, BF16_ROWS)

    def in_copy(tile, buf, c):
        return pltpu.make_async_copy(xs_hbm.at[src_ref[tile * cpt + c]], xbuf.at[buf, rows(c)], in_sem.at[buf])

    def out_copy(tile, buf, c):
        return pltpu.make_async_copy(ybuf.at[buf, rows(c)], y_hbm.at[dst_ref[(tile + 1) * cpt + c]],
                                     out_sem.at[buf])

    def start_all(copy, tile, buf):
        for c in range(cpt):
            copy(tile, buf, c).start()

    def wait_all(copy, tile, buf):
        for c in range(cpt):
            copy(tile, buf, c).wait()

    @pl.when(t == 0)
    def _():
        start_all(in_copy, 0, 0)
        ybuf[...] = jnp.zeros(ybuf.shape, BF16)
        even_dump = [pltpu.make_async_copy(ybuf.at[1, rows(c)], y_hbm.at[y_hbm.shape[0] - 2 * cpt + c],
                                           out_sem.at[0]) for c in range(cpt)]
        for copy in even_dump:
            copy.start()
        for copy in even_dump:
            copy.wait()

    @pl.when(t <= last)
    def _():
        @pl.when(t >= 1)
        def _():
            wait_all(out_copy, t - 2, slot)

        wait_all(in_copy, t, slot)
        start_all(in_copy, jnp.minimum(t + 1, last), other)
        start_all(out_copy, t - 1, other)

        @pl.when((e < N_EXPERTS) & ((t == 0) | (te_ref[jnp.maximum(t - 1, 0)] != e)))
        def _():
            wgu_b[...] = wgu_ref[...].astype(BF16)
            wd_b[...] = wd_ref[...].astype(BF16)

        def expert_mlp(m):
            x = xbuf[slot, 0:m, :]
            nc = 512
            for j in range(f // nc):
                gsl = slice(j * nc, (j + 1) * nc)
                usl = slice(f + j * nc, f + (j + 1) * nc)
                g = jnp.dot(x, wgu_b[:, gsl], preferred_element_type=F32) + bgu_ref[:, gsl]
                u = jnp.dot(x, wgu_b[:, usl], preferred_element_type=F32) + bgu_ref[:, usl]
                g = jnp.minimum(g, SWIGLU_LIMIT)
                u = jnp.clip(u, -SWIGLU_LIMIT, SWIGLU_LIMIT)
                act[0:m, gsl] = (g * _sigmoid(SWIGLU_ALPHA * g) * (u + 1.0)).astype(BF16)
            a = act[0:m, :]
            for j in range(wd_ref.shape[1] // nc):
                sl = slice(j * nc, (j + 1) * nc)
                y = jnp.dot(a, wd_b[:, sl], preferred_element_type=F32) + bd_ref[:, sl]
                ybuf[slot, 0:m, sl] = y.astype(BF16)

        few = nv_ref[t] * BF16_ROWS <= MOE_TILE // 2

        @pl.when((e < N_EXPERTS) & jnp.logical_not(few))
        def _():
            expert_mlp(MOE_TILE)

        @pl.when((e < N_EXPERTS) & few)
        def _():
            expert_mlp(MOE_TILE // 2)

        @pl.when(e >= N_EXPERTS)
        def _():
            ybuf[slot] = jnp.zeros(ybuf.shape[1:], BF16)

        @pl.when(t == last)
        def _():
            start_all(out_copy, t, slot)
            wait_all(out_copy, t - 1, other)
            wait_all(out_copy, t, slot)
            wait_all(in_copy, t, other)


def _moe(xs_chunks, te, wsel, nv, src, dst, nt, w_gate_up, b_gate_up, w_down, b_down):
    nch, _, d = xs_chunks.shape
    f = w_down.shape[1]
    max_tiles = te.shape[0]
    by_wsel = lambda t, te_r, wsel_r, nv_r, src_r, dst_r, nt_r: (wsel_r[t], 0, 0)
    by_tile = lambda t, te_r, wsel_r, nv_r, src_r, dst_r, nt_r: (jnp.minimum(te_r[t], N_EXPERTS - 1), 0, 0)
    grid_spec = pltpu.PrefetchScalarGridSpec(
        num_scalar_prefetch=6,
        grid=(max_tiles,),
        in_specs=[
            pl.BlockSpec(memory_space=pl.ANY),
            pl.BlockSpec((None, d, 2 * f), by_wsel),
            pl.BlockSpec((None, 1, 2 * f), by_tile),
            pl.BlockSpec((None, f, d), by_wsel),
            pl.BlockSpec((None, 1, d), by_tile),
        ],
        out_specs=pl.BlockSpec(memory_space=pl.ANY),
        scratch_shapes=[
            pltpu.VMEM((2, MOE_TILE, d), BF16),
            pltpu.VMEM((2, MOE_TILE, d), BF16),
            pltpu.VMEM((MOE_TILE, f), BF16),
            pltpu.VMEM((d, 2 * f), BF16),
            pltpu.VMEM((f, d), BF16),
            pltpu.SemaphoreType.DMA((2,)),
            pltpu.SemaphoreType.DMA((2,)),
        ],
    )
    return pl.pallas_call(
        _moe_body,
        grid_spec=grid_spec,
        out_shape=jax.ShapeDtypeStruct((nch + 2 * CHUNKS_PER_TILE, BF16_ROWS, d), BF16),
        compiler_params=_cparams(("arbitrary",), 56),
        name="moe_grouped",
    )(te, wsel, nv, src, dst, nt, xs_chunks, w_gate_up, b_gate_up, w_down, b_down)


def _combine_body(y_ref, pos_ref, gate_ref, x1_ref, g_ref, b_ref, o_ref):
    tb = x1_ref.shape[0]
    pos = pos_ref[...]
    gate = gate_ref[...]
    cols = 512
    acc = jnp.zeros(x1_ref.shape, F32)
    for rc in range(y_ref.shape[0] // cols):
        r = lax.broadcasted_iota(I32, (tb, cols), 1) + rc * cols
        w = jnp.zeros((tb, cols), F32)
        for k in range(TOP_K):
            w = w + jnp.where(r == pos[:, k:k + 1], gate[:, k:k + 1], 0.0)
        acc = acc + jnp.dot(w.astype(BF16), y_ref[rc * cols:(rc + 1) * cols, :], preferred_element_type=F32)
    o_ref[...] = _layer_norm(DEEPNORM_ALPHA * x1_ref[...] + acc, g_ref[...], b_ref[...])


def _combine(y_local, pos, gates, x1, g, b):
    n, d = x1.shape
    tb = TOKEN_TILE
    nb = n // tb
    row = lambda: pl.BlockSpec((1, d), lambda i: (0, 0))
    return pl.pallas_call(
        _combine_body,
        grid=(nb,),
        in_specs=[
            pl.BlockSpec((LOCAL_ROWS, d), lambda i: (i, 0)),
            pl.BlockSpec((None, tb, TOP_K), lambda i: (i, 0, 0)),
            pl.BlockSpec((None, tb, TOP_K), lambda i: (i, 0, 0)),
            pl.BlockSpec((tb, d), lambda i: (i, 0)),
            row(), row(),
        ],
        out_specs=pl.BlockSpec((tb, d), lambda i: (i, 0)),
        out_shape=jax.ShapeDtypeStruct((n, d), F32),
        compiler_params=_cparams(("arbitrary",), 40),
        name="combine_ln",
    )(y_local, pos, gates, x1, g, b)


def kernel(x, emb_ln_g, emb_ln_b, w_in, lambda_q1, lambda_k1, lambda_q2, lambda_k2, subln_g, conv_w, conv_b,
           conv_ln_g, conv_ln_b, w_out, ln1_g, ln1_b, router_w, router_b, w_gate_up, b_gate_up, w_down, b_down,
           ln2_g, ln2_b):
    batch, seq, d = x.shape
    n = batch * seq
    assert n % TOKEN_TILE == 0 and seq % ATTN_KV_BLOCK == 0 and seq % CONV_TILE == 0
    assert w_in.shape[0] == DEPTH
    x2 = x.reshape(n, d)
    row = lambda v: v.reshape(1, -1).astype(F32)
    slopes = jnp.array([(2.0 ** (-8.0 / N_HEADS)) ** (i + 1) for i in range(N_HEADS)], dtype=F32)

    h = _inproj(x2, row(emb_ln_g), row(emb_ln_b), w_in[0].astype(BF16))
    a = _attention(h, slopes, row(lambda_q1[0]), row(lambda_k1[0]), row(lambda_q2[0]), row(lambda_k2[0]),
                   row(subln_g[0]), batch=batch, seq=seq)
    c = _conv_module(h, conv_w[0], row(conv_b[0]), row(conv_ln_g[0]), row(conv_ln_b[0]), batch=batch, seq=seq)
    x1, xs_local, pos_t, gates_t, cc = _mix(
        x2, a, c, w_out[0].astype(BF16), row(emb_ln_g), row(emb_ln_b), row(ln1_g[0]), row(ln1_b[0]),
        router_w[0].T, router_b[0].reshape(N_EXPERTS, 1))

    te, wsel, nv, src, dst, nt = _build_schedule(cc[:, :, 0])
    nb = n // TOKEN_TILE
    y_chunks = _moe(xs_local.reshape(nb * CHUNKS_PER_BLOCK, BF16_ROWS, d), te, wsel, nv, src, dst, nt,
                    w_gate_up[0], b_gate_up[0].reshape(N_EXPERTS, 1, -1), w_down[0],
                    b_down[0].reshape(N_EXPERTS, 1, -1))
    y_local = y_chunks.reshape(-1, d)
    out = _combine(y_local, jnp.transpose(pos_t, (0, 2, 1)), jnp.transpose(gates_t, (0, 2, 1)), x1,
                   row(ln2_g[0]), row(ln2_b[0]))
    return out.reshape(batch, seq, d)
```

```python
import functools

import jax
import jax.numpy as jnp
from jax import lax
from jax.experimental import pallas as pl
from jax.experimental.pallas import tpu as pltpu

F32 = jnp.float32
BF16 = jnp.bfloat16
I32 = jnp.int32

N_HEADS = 4
HEAD_DIM = 64
V_DIM = 128
ATTN_WIDTH = N_HEADS * V_DIM
CONV_WIDTH = 512
CONV_KERNEL = 31
N_EXPERTS = 32
TOP_K = 4
SWIGLU_LIMIT = 7.0
SWIGLU_ALPHA = 1.702
LN_EPS = 1e-5
DEPTH = 1
DEEPNORM_ALPHA = (2.0 * DEPTH) ** 0.25
LAMBDA_INIT = 0.2
LOG2_E = 1.4426950408889634
Q_COLS = N_HEADS * 2 * HEAD_DIM
Q_SCALE = HEAD_DIM ** -0.5 * LOG2_E

LANES = 128
BF16_ROWS = 16

TOKEN_TILE = 512
ATTN_KV_BLOCK = 256
ATTN_SUB = 256
CONV_TILE = 64
CONV_ROW_STRIDE = 4
CONV_GLU_TILE = 128
MOE_TILE = 1024
MOE_ROW_QUANTUM = 256
CHUNKS_PER_TILE = MOE_TILE // BF16_ROWS
LOCAL_ROWS = TOP_K * TOKEN_TILE + N_EXPERTS * BF16_ROWS
CHUNKS_PER_BLOCK = LOCAL_ROWS // BF16_ROWS


def _cparams(semantics, vmem_mib):
    return pltpu.CompilerParams(dimension_semantics=semantics, vmem_limit_bytes=vmem_mib * 1024 * 1024)


def _layer_norm(x, g, b):
    mu = jnp.mean(x, axis=-1, keepdims=True)
    xc = x - mu
    var = jnp.mean(xc * xc, axis=-1, keepdims=True)
    return xc * lax.rsqrt(var + LN_EPS) * g + b


def _sigmoid(x):
    return 1.0 / (1.0 + jnp.exp(-x))


def _inproj_body(x_ref, g_ref, b_ref, w_ref, h_ref, *, n_chunk):
    xn = _layer_norm(x_ref[...], g_ref[...], b_ref[...]).astype(BF16)
    for j in range(w_ref.shape[1] // n_chunk):
        sl = slice(j * n_chunk, (j + 1) * n_chunk)
        hj = jnp.dot(xn, w_ref[:, sl], preferred_element_type=F32)
        if (j + 1) * n_chunk <= Q_COLS:
            hj = hj * Q_SCALE
        h_ref[:, sl] = hj.astype(BF16)


def _inproj(x2, g, b, w_bf16):
    n, d = x2.shape
    cols = w_bf16.shape[1]
    tm = TOKEN_TILE
    return pl.pallas_call(
        functools.partial(_inproj_body, n_chunk=512),
        grid=(n // tm,),
        in_specs=[
            pl.BlockSpec((tm, d), lambda i: (i, 0)),
            pl.BlockSpec((1, d), lambda i: (0, 0)),
            pl.BlockSpec((1, d), lambda i: (0, 0)),
            pl.BlockSpec((d, cols), lambda i: (0, 0)),
        ],
        out_specs=pl.BlockSpec((tm, cols), lambda i: (i, 0)),
        out_shape=jax.ShapeDtypeStruct((n, cols), BF16),
        compiler_params=_cparams(("arbitrary",), 40),
        name="inproj",
    )(x2, g, b, w_bf16)


def _attn_body(slopes_ref, lq1_ref, lk1_ref, lq2_ref, lk2_ref, g_ref, q_ref, k_ref, v_ref, o_ref,
               q2_ref, m_ref, l_ref, acc_ref, *, seq):
    head = pl.program_id(1)
    slope = slopes_ref[head] * LOG2_E
    lam =(jnp.exp(jnp.sum(lq1_ref[...] * lk1_ref[...], axis=-1, keepdims=True))
           - jnp.exp(jnp.sum(lq2_ref[...] * lk2_ref[...], axis=-1, keepdims=True))
           + LAMBDA_INIT)

    q = q_ref[...]
    lane = lax.broadcasted_iota(I32, q.shape, 1)
    zero = jnp.zeros_like(q)
    q2_ref[0:seq, :] = jnp.where(lane < HEAD_DIM, q, zero)
    q2_ref[seq:, :] = jnp.where(lane >= HEAD_DIM, q, zero)
    m_ref[...] = jnp.full(m_ref.shape, -jnp.inf, F32)
    l_ref[...] = jnp.zeros(l_ref.shape, F32)
    acc_ref[...] = jnp.zeros(acc_ref.shape, F32)
    kb_rows = ATTN_KV_BLOCK
    sub = ATTN_SUB
    ones = jnp.ones((kb_rows, V_DIM), BF16)
    col = lax.broadcasted_iota(I32, (1, kb_rows), 1)

    for d in range(seq // kb_rows):
        k0 = d * kb_rows
        kb = k_ref[k0:k0 + kb_rows, :]
        vext = jnp.concatenate([v_ref[k0:k0 + kb_rows, :], ones], axis=1)
        for half in range(2):
            for i in range(k0 // sub, seq // sub):
                r0 = i * sub
                rows = slice(half * seq + r0, half * seq + r0 + sub)
                ncol = min(kb_rows, -(-(r0 + sub - k0) // LANES) * LANES)
                bias = (col[:, 0:ncol] + (k0 - r0)).astype(F32) * slope
                s = lax.dot_general(q2_ref[rows, :], kb[0:ncol, :], (((1,), (1,)), ((), ())),
                                    preferred_element_type=F32) + bias
                if k0 + ncol - 1 > r0:
                    row = lax.broadcasted_iota(I32, s.shape, 0) + (r0 - k0)
                    s = jnp.where(lax.broadcasted_iota(I32, s.shape, 1) <= row, s, -jnp.inf)
                m_prev = m_ref[rows, :]
                m_new = jnp.maximum(m_prev, jnp.max(s, axis=1, keepdims=True))
                alpha = jnp.exp2(m_prev - m_new)
                p = jnp.exp2(s - jnp.concatenate([m_new] * (ncol // LANES), axis=1))
                pv = jnp.dot(p.astype(BF16), vext[0:ncol, :], preferred_element_type=F32)
                acc_ref[rows, :] = alpha * acc_ref[rows, :] + pv[:, :V_DIM]
                l_ref[rows, :] = alpha * l_ref[rows, :] + pv[:, V_DIM:]
                m_ref[rows, :] = m_new

    o = acc_ref[...] / l_ref[...]
    od = o[:seq] - lam * o[seq:]
    ms = jnp.mean(od * od, axis=-1, keepdims=True)
    out = od * lax.rsqrt(ms + LN_EPS) * g_ref[...] * (1.0 - LAMBDA_INIT)
    o_ref[...] = out.astype(BF16)


def _attention(h, slopes, lq1, lk1, lq2, lk2, subln_g, *, batch, seq):
    n = h.shape[0]
    vec = lambda: pl.BlockSpec((1, HEAD_DIM), lambda b, hd, *_: (0, 0))
    grid_spec = pltpu.PrefetchScalarGridSpec(
        num_scalar_prefetch=1,
        grid=(batch, N_HEADS),
        in_specs=[
            vec(), vec(), vec(), vec(),
            pl.BlockSpec((1, V_DIM), lambda b, hd, *_: (0, 0)),
            pl.BlockSpec((seq, V_DIM), lambda b, hd, *_: (b, hd)),
            pl.BlockSpec((seq, V_DIM), lambda b, hd, *_: (b, N_HEADS + hd)),
            pl.BlockSpec((seq, V_DIM), lambda b, hd, *_: (b, 2 * N_HEADS + hd)),
        ],
        out_specs=pl.BlockSpec((seq, V_DIM), lambda b, hd, *_: (b, hd)),
        scratch_shapes=[
            pltpu.VMEM((2 * seq, V_DIM), BF16),
            pltpu.VMEM((2 * seq, LANES), F32),
            pltpu.VMEM((2 * seq, LANES), F32),
            pltpu.VMEM((2 * seq, V_DIM), F32),
        ],
    )
    return pl.pallas_call(
        functools.partial(_attn_body, seq=seq),
        grid_spec=grid_spec,
        out_shape=jax.ShapeDtypeStruct((n, ATTN_WIDTH), BF16),
        compiler_params=_cparams(("arbitrary", "arbitrary"), 32),
        name="diff_attn",
    )(slopes, lq1, lk1, lq2, lk2, subln_g, h, h, h)


CONV_PAD = 32


def _conv_body(a_ref, g_ref, w_ref, cb_ref, lg_ref, lb_ref, o_ref, glu_ref, conv_ref, *, seq):
    n_slab = CONV_WIDTH // LANES
    for c in range(n_slab):
        glu_ref[c, 0:CONV_PAD, :] = jnp.zeros((CONV_PAD, LANES), F32)

    def glu(i, _):
        st = pl.multiple_of(i * CONV_GLU_TILE, CONV_GLU_TILE)
        a = a_ref[pl.ds(st, CONV_GLU_TILE), :].astype(F32)
        g = g_ref[pl.ds(st, CONV_GLU_TILE), :].astype(F32)
        val = a * _sigmoid(g)
        for c in range(n_slab):
            glu_ref[c, pl.ds(CONV_PAD + st, CONV_GLU_TILE), :] = val[:, c * LANES:(c + 1) * LANES]
        return 0

    lax.fori_loop(0, seq // CONV_GLU_TILE, glu, 0)

    span = 8 * CONV_ROW_STRIDE
    for c in range(n_slab):
        lanes = slice(c * LANES, (c + 1) * LANES)
        taps = [jnp.broadcast_to(w_ref[j:j + 1, lanes], (8, LANES)) for j in range(CONV_KERNEL)]
        bias = jnp.broadcast_to(cb_ref[:, lanes], (8, LANES))

        def conv(i, _):
            st = i * CONV_TILE
            for grp in range(CONV_TILE // span):
                for p in range(CONV_ROW_STRIDE):
                    t0 = st + grp * span + p
                    acc = bias
                    for j in range(CONV_KERNEL):
                        src = pl.ds(t0 + (CONV_PAD - (CONV_KERNEL - 1) + j), 8, stride=CONV_ROW_STRIDE)
                        acc = acc + glu_ref[c, src, :] * taps[j]
                    conv_ref[c, pl.ds(t0, 8, stride=CONV_ROW_STRIDE), :] = acc
            return 0

        lax.fori_loop(0, seq // CONV_TILE, conv, 0)

    def norm(i, _):
        st = pl.multiple_of(i * CONV_GLU_TILE, CONV_GLU_TILE)
        x = jnp.concatenate([conv_ref[c, pl.ds(st, CONV_GLU_TILE), :] for c in range(n_slab)], axis=1)
        y = _layer_norm(x, lg_ref[...], lb_ref[...])
        o_ref[pl.ds(st, CONV_GLU_TILE), :] = (y * _sigmoid(y)).astype(BF16)
        return 0

    lax.fori_loop(0, seq // CONV_GLU_TILE, norm, 0, unroll=2)


def _conv_module(h, conv_w, conv_b, ln_g, ln_b, *, batch, seq):
    n = h.shape[0]
    u_col = 3 * ATTN_WIDTH // CONV_WIDTH
    row = lambda: pl.BlockSpec((1, CONV_WIDTH), lambda b: (0, 0))
    return pl.pallas_call(
        functools.partial(_conv_body, seq=seq),
        grid=(batch,),
        in_specs=[
            pl.BlockSpec((seq, CONV_WIDTH), lambda b: (b, u_col)),
            pl.BlockSpec((seq, CONV_WIDTH), lambda b: (b, u_col + 1)),
            pl.BlockSpec((CONV_KERNEL, CONV_WIDTH), lambda b: (0, 0)),
            row(), row(), row(),
        ],
        out_specs=pl.BlockSpec((seq, CONV_WIDTH), lambda b: (b, 0)),
        out_shape=jax.ShapeDtypeStruct((n, CONV_WIDTH), BF16),
        scratch_shapes=[
            pltpu.VMEM((CONV_WIDTH // LANES, CONV_PAD + seq, LANES), F32),
            pltpu.VMEM((CONV_WIDTH // LANES, seq, LANES), F32),
        ],
        compiler_params=_cparams(("arbitrary",), 32),
        name="conformer_conv",
    )(h, h, conv_w, conv_b, ln_g, ln_b)


def _mix_body(x_ref, a_ref, c_ref, wo_ref, g0_ref, b0_ref, g1_ref, b1_ref, rw2_ref, rb_ref,
              x1_ref, xs_ref, pos_ref, gate_ref, cc_ref):
    tb = x_ref.shape[0]
    xn = _layer_norm(x_ref[...], g0_ref[...], b0_ref[...])
    mix = (jnp.dot(a_ref[...], wo_ref[0:ATTN_WIDTH, :], preferred_element_type=F32)
           + jnp.dot(c_ref[...], wo_ref[ATTN_WIDTH:, :], preferred_element_type=F32))
    x1 = _layer_norm(DEEPNORM_ALPHA * xn + mix, g1_ref[...], b1_ref[...])
    x1_ref[...] = x1
    x1b = x1.astype(BF16)

    x_lo = (x1 - x1b.astype(F32)).astype(BF16)
    rw2 = rw2_ref[...]
    nt_dims = (((1,), (1,)), ((), ()))
    both = lax.dot_general(rw2, x1b, nt_dims, preferred_element_type=F32)
    logits = ((lax.dot_general(rw2[:N_EXPERTS], x_lo, nt_dims, preferred_element_type=F32) + both[N_EXPERTS:])
              + both[:N_EXPERTS]) + rb_ref[...]
    eidx = lax.broadcasted_iota(I32, logits.shape, 0)
    work = logits
    vals, hots = [], []
    for _ in range(TOP_K):
        mx = jnp.max(work, axis=0, keepdims=True)
        sel = jnp.min(jnp.where(work == mx, eidx, N_EXPERTS), axis=0, keepdims=True)
        hot = eidx == sel
        vals.append(mx)
        hots.append(hot)
        work = jnp.where(hot, -jnp.inf, work)
    ex = [jnp.exp(v - vals[0]) for v in vals]
    den = ex[0] + ex[1] + ex[2] + ex[3]
    for k in range(TOP_K):
        gate_ref[k:k + 1, :] = ex[k] / den

    member = jnp.zeros(logits.shape, F32)
    for hot in hots:
        member = member + jnp.where(hot, 1.0, 0.0)
    before = (lax.broadcasted_iota(I32, (tb, tb), 0) < lax.broadcasted_iota(I32, (tb, tb), 1))
    rank = jnp.dot(member.astype(BF16), jnp.where(before, 1.0, 0.0).astype(BF16), preferred_element_type=F32)
    count = jnp.sum(member, axis=1, keepdims=True)
    chunks = jnp.floor((count + (BF16_ROWS - 1.0)) * (1.0 / BF16_ROWS))
    cc_ref[...] = jnp.broadcast_to(chunks, cc_ref.shape).astype(I32)
    lower = (lax.broadcasted_iota(I32, (N_EXPERTS, N_EXPERTS), 1) < lax.broadcasted_iota(I32, (N_EXPERTS, N_EXPERTS), 0))
    seg_start = jnp.dot(jnp.where(lower, 1.0, 0.0).astype(BF16),
                        jnp.broadcast_to(chunks, (N_EXPERTS, LANES)).astype(BF16),
                        preferred_element_type=F32)[:, 0:1]
    slot = seg_start * float(BF16_ROWS) + rank
    pos = [jnp.sum(jnp.where(hot, slot, 0.0), axis=0, keepdims=True).astype(I32) for hot in hots]
    for k in range(TOP_K):
        pos_ref[k:k + 1, :] = pos[k]

    rows = 256
    one, nil = jnp.ones((), BF16), jnp.zeros((), BF16)
    pos16 = [p.astype(jnp.int16) for p in pos]
    for rc in range(xs_ref.shape[0] // rows):
        r = lax.broadcasted_iota(jnp.int16, (rows, tb), 0) + jnp.int16(rc * rows)
        hit = (r == pos16[0]) | (r == pos16[1]) | (r == pos16[2]) | (r == pos16[3])
        sel = jnp.where(hit, one, nil)
        xs_ref[rc * rows:(rc + 1) * rows, :] = jnp.dot(sel, x1b, preferred_element_type=F32).astype(BF16)


def _mix(x2, a, c, wo_bf16, g0, b0, g1, b1, rw2, rb):
    n, d = x2.shape
    tb = TOKEN_TILE
    nb = n // tb
    row = lambda: pl.BlockSpec((1, d), lambda i: (0, 0))
    return pl.pallas_call(
        _mix_body,
        grid=(nb,),
        in_specs=[
            pl.BlockSpec((tb, d), lambda i: (i, 0)),
            pl.BlockSpec((tb, ATTN_WIDTH), lambda i: (i, 0)),
            pl.BlockSpec((tb, CONV_WIDTH), lambda i: (i, 0)),
            pl.BlockSpec((d, d), lambda i: (0, 0)),
            row(), row(), row(), row(),
            pl.BlockSpec((2 * N_EXPERTS, d), lambda i: (0, 0)),
            pl.BlockSpec((N_EXPERTS, 1), lambda i: (0, 0)),
        ],
        out_specs=[
            pl.BlockSpec((tb, d), lambda i: (i, 0)),
            pl.BlockSpec((None, LOCAL_ROWS, d), lambda i: (i, 0, 0)),
            pl.BlockSpec((None, TOP_K, tb), lambda i: (i, 0, 0)),
            pl.BlockSpec((None, TOP_K, tb), lambda i: (i, 0, 0)),
            pl.BlockSpec((None, N_EXPERTS, LANES), lambda i: (i, 0, 0)),
        ],
        out_shape=[
            jax.ShapeDtypeStruct((n, d), F32),
            jax.ShapeDtypeStruct((nb, LOCAL_ROWS, d), BF16),
            jax.ShapeDtypeStruct((nb, TOP_K, tb), I32),
            jax.ShapeDtypeStruct((nb, TOP_K, tb), F32),
            jax.ShapeDtypeStruct((nb, N_EXPERTS, LANES), I32),
        ],
        compiler_params=_cparams(("arbitrary",), 48),
        name="mix_router_sort",
    )(x2, a, c, wo_bf16, g0, b0, g1, b1, rw2, rb)


def _max_tiles(nb):
    return (nb * CHUNKS_PER_BLOCK + CHUNKS_PER_TILE - 1) // CHUNKS_PER_TILE + N_EXPERTS + 1


def _build_schedule(cc):
    nb = cc.shape[0]
    cpt = CHUNKS_PER_TILE
    ne1 = N_EXPERTS + 1
    tail = CHUNKS_PER_BLOCK - jnp.sum(cc, axis=1)
    cce = jnp.concatenate([cc, tail[:, None]], axis=1)
    seg_start = jnp.cumsum(cce, axis=1) - cce
    tot = jnp.sum(cce, axis=0)
    tiles_e = (tot + cpt - 1) // cpt
    tile_end = jnp.cumsum(tiles_e)
    max_tiles = _max_tiles(nb)

    def lookup(passed, table):
        return table[0] + jnp.sum(jnp.where(passed, (table[1:] - table[:-1])[None, :], 0), axis=1)

    t = jnp.arange(max_tiles, dtype=I32)
    passed = tile_end[None, :] <= t[:, None]
    te = jnp.minimum(jnp.sum(passed, axis=1), ne1 - 1).astype(I32)
    experts = jnp.arange(ne1, dtype=I32)
    has_tiles = (tiles_e > 0) & (experts < N_EXPERTS)
    later = jnp.where(has_tiles[None, :] & (experts[None, :] > te[:, None]), experts[None, :], ne1)
    nxt = jnp.min(later, axis=1)
    own = jnp.where(te < N_EXPERTS, te, jnp.max(jnp.where(has_tiles, experts, 0)))
    is_first = jnp.any(((tile_end - tiles_e)[None, :] == t[:, None]) & (tiles_e > 0)[None, :], axis=1)
    wsel = jnp.where(is_first | (nxt >= N_EXPERTS), own, nxt).astype(I32)

    pad_len = tiles_e * cpt - tot
    seg_len = jnp.concatenate([cce.T, pad_len[:, None]], axis=1).reshape(-1)
    seg_end = jnp.cumsum(seg_len)
    block = jnp.arange(nb, dtype=I32)[None, :]
    is_real = jnp.concatenate([jnp.ones((ne1, nb), I32), jnp.zeros((ne1, 1), I32)], axis=1).reshape(-1)
    seg_first = (seg_end - seg_len) * is_real
    zero = jnp.zeros((1,), I32)

    base = block * CHUNKS_PER_BLOCK + seg_start.T
    first = jnp.concatenate([jnp.concatenate([base, jnp.zeros((ne1, 1), I32)], axis=1).reshape(-1) - seg_first,
                             zero])
    p = jnp.arange(max_tiles * cpt, dtype=I32)
    passed = seg_end[None, :] <= p[:, None]
    real = lookup(passed, jnp.concatenate([is_real, zero]))
    src = (real * p + lookup(passed, first)).astype(I32)
    dump0 = nb * CHUNKS_PER_BLOCK
    dst = jnp.where(real > 0, src, dump0 + ((p // cpt) % 2) * cpt + p % cpt).astype(I32)
    dummy = dump0 + cpt + jnp.arange(cpt, dtype=I32)
    nv = jnp.sum(real.reshape(max_tiles, cpt), axis=1).astype(I32)
    return te, wsel, nv, src, jnp.concatenate([dummy, dst]), jnp.reshape(tile_end[-1], (1,)).astype(I32)


def _moe_body(te_ref, wsel_ref, nv_ref, src_ref, dst_ref, nt_ref, xs_hbm, wgu_ref, bgu_ref, wd_ref, bd_ref, y_hbm,
              xbuf, ybuf, act, wgu_b, wd_b, in_sem, out_sem):
    t = pl.program_id(0)
    last = nt_ref[0] - 1
    cpt = CHUNKS_PER_TILE
    f = wd_ref.shape[0]
    slot = t % 2
    other = 1 - slot
    e = te_ref[t]

    def rows(c):
        return pl.ds(c * BF16_ROWS, BF16_ROWS)

    def in_copy(tile, buf, c):
        return pltpu.make_async_copy(xs_hbm.at[src_ref[tile * cpt + c]], xbuf.at[buf, rows(c)], in_sem.at[buf])

    def out_copy(tile, buf, c):
        return pltpu.make_async_copy(ybuf.at[buf, rows(c)], y_hbm.at[dst_ref[(tile + 1) * cpt + c]],
                                     out_sem.at[buf])

    def start_all(copy, tile, buf):
        for c in range(cpt):
            copy(tile, buf, c).start()

    def wait_all(copy, tile, buf):
        for c in range(cpt):
            copy(tile, buf, c).wait()

    @pl.when(t == 0)
    def _():
        start_all(in_copy, 0, 0)
        ybuf[...] = jnp.zeros(ybuf.shape, BF16)
        even_dump = [pltpu.make_async_copy(ybuf.at[1, rows(c)], y_hbm.at[y_hbm.shape[0] - 2 * cpt + c],
                                           out_sem.at[0]) for c in range(cpt)]
        for copy in even_dump:
            copy.start()
        for copy in even_dump:
            copy.wait()

    @pl.when(t <= last)
    def _():
        @pl.when(t >= 1)
        def _():
            wait_all(out_copy, t - 2, slot)

        wait_all(in_copy, t, slot)
        start_all(in_copy, jnp.minimum(t + 1, last), other)
        start_all(out_copy, t - 1, other)

        @pl.when((e < N_EXPERTS) & ((t == 0) | (te_ref[jnp.maximum(t - 1, 0)] != e)))
        def _():
            wgu_b[...] = wgu_ref[...].astype(BF16)
            wd_b[...] = wd_ref[...].astype(BF16)

        def expert_mlp(m):
            x = xbuf[slot, 0:m, :]
            nc = 512
            for j in range(f // nc):
                gsl = slice(j * nc, (j + 1) * nc)
                usl = slice(f + j * nc, f + (j + 1) * nc)
                g = jnp.dot(x, wgu_b[:, gsl], preferred_element_type=F32) + bgu_ref[:, gsl]
                u = jnp.dot(x, wgu_b[:, usl], preferred_element_type=F32) + bgu_ref[:, usl]
                g = jnp.minimum(g, SWIGLU_LIMIT)
                u = jnp.clip(u, -SWIGLU_LIMIT, SWIGLU_LIMIT)
                act[0:m, gsl] = (g * _sigmoid(SWIGLU_ALPHA * g) * (u + 1.0)).astype(BF16)
            a = act[0:m, :]
            for j in range(wd_ref.shape[1] // nc):
                sl = slice(j * nc, (j + 1) * nc)
                y = jnp.dot(a, wd_b[:, sl], preferred_element_type=F32) + bd_ref[:, sl]
                ybuf[slot, 0:m, sl] = y.astype(BF16)

        quanta = (nv_ref[t] * BF16_ROWS + (MOE_ROW_QUANTUM - 1)) // MOE_ROW_QUANTUM
        for k in range(1, MOE_TILE // MOE_ROW_QUANTUM + 1):
            pl.when((e < N_EXPERTS) & (quanta == k))(functools.partial(expert_mlp, k * MOE_ROW_QUANTUM))

        @pl.when(e >= N_EXPERTS)
        def _():
            ybuf[slot] = jnp.zeros(ybuf.shape[1:], BF16)

        @pl.when(t == last)
        def _():
            start_all(out_copy, t, slot)
            wait_all(out_copy, t - 1, other)
            wait_all(out_copy, t, slot)
            wait_all(in_copy, t, other)


def _moe(xs_chunks, te, wsel, nv, src, dst, nt, w_gate_up, b_gate_up, w_down, b_down):
    nch, _, d = xs_chunks.shape
    f = w_down.shape[1]
    max_tiles = te.shape[0]
    by_wsel = lambda t, te_r, wsel_r, nv_r, src_r, dst_r, nt_r: (wsel_r[t], 0, 0)
    by_tile = lambda t, te_r, wsel_r, nv_r, src_r, dst_r, nt_r: (jnp.minimum(te_r[t], N_EXPERTS - 1), 0, 0)
    grid_spec = pltpu.PrefetchScalarGridSpec(
        num_scalar_prefetch=6,
        grid=(max_tiles,),
        in_specs=[
            pl.BlockSpec(memory_space=pl.ANY),
            pl.BlockSpec((None, d, 2 * f), by_wsel),
            pl.BlockSpec((None, 1, 2 * f), by_tile),
            pl.BlockSpec((None, f, d), by_wsel),
            pl.BlockSpec((None, 1, d), by_tile),
        ],
        out_specs=pl.BlockSpec(memory_space=pl.ANY),
        scratch_shapes=[
            pltpu.VMEM((2, MOE_TILE, d), BF16),
            pltpu.VMEM((2, MOE_TILE, d), BF16),
            pltpu.VMEM((MOE_TILE, f), BF16),
            pltpu.VMEM((d, 2 * f), BF16),
            pltpu.VMEM((f, d), BF16),
            pltpu.SemaphoreType.DMA((2,)),
            pltpu.SemaphoreType.DMA((2,)),
        ],
    )
    return pl.pallas_call(
        _moe_body,
        grid_spec=grid_spec,
        out_shape=jax.ShapeDtypeStruct((nch + 2 * CHUNKS_PER_TILE, BF16_ROWS, d), BF16),
        compiler_params=_cparams(("arbitrary",), 56),
        name="moe_grouped",
    )(te, wsel, nv, src, dst, nt, xs_chunks, w_gate_up, b_gate_up, w_down, b_down)


def _combine_body(y_ref, pos_ref, gate_ref, x1_ref, g_ref, b_ref, o_ref):
    tb = x1_ref.shape[0]
    pos = pos_ref[...]
    gate = gate_ref[...]
    cols = 512
    acc = jnp.zeros(x1_ref.shape, F32)
    for rc in range(y_ref.shape[0] // cols):
        r = lax.broadcasted_iota(I32, (tb, cols), 1) + rc * cols
        w = jnp.zeros((tb, cols), F32)
        for k in range(TOP_K):
            w = w + jnp.where(r == pos[:, k:k + 1], gate[:, k:k + 1], 0.0)
        acc = acc + jnp.dot(w.astype(BF16), y_ref[rc * cols:(rc + 1) * cols, :], preferred_element_type=F32)
    o_ref[...] = _layer_norm(DEEPNORM_ALPHA * x1_ref[...] + acc, g_ref[...], b_ref[...])


def _combine(y_local, pos, gates, x1, g, b):
    n, d = x1.shape
    tb = TOKEN_TILE
    nb = n // tb
    row = lambda: pl.BlockSpec((1, d), lambda i: (0, 0))
    return pl.pallas_call(
        _combine_body,
        grid=(nb,),
        in_specs=[
            pl.BlockSpec((LOCAL_ROWS, d), lambda i: (i, 0)),
            pl.BlockSpec((None, tb, TOP_K), lambda i: (i, 0, 0)),
            pl.BlockSpec((None, tb, TOP_K), lambda i: (i, 0, 0)),
            pl.BlockSpec((tb, d), lambda i: (i, 0)),
            row(), row(),
        ],
        out_specs=pl.BlockSpec((tb, d), lambda i: (i, 0)),
        out_shape=jax.ShapeDtypeStruct((n, d), F32),
        compiler_params=_cparams(("arbitrary",), 40),
        name="combine_ln",
    )(y_local, pos, gates, x1, g, b)


def kernel(x, emb_ln_g, emb_ln_b, w_in, lambda_q1, lambda_k1, lambda_q2, lambda_k2, subln_g, conv_w, conv_b,
           conv_ln_g, conv_ln_b, w_out, ln1_g, ln1_b, router_w, router_b, w_gate_up, b_gate_up, w_down, b_down,
           ln2_g, ln2_b):
    batch, seq, d = x.shape
    n = batch * seq
    assert n % TOKEN_TILE == 0 and seq % ATTN_KV_BLOCK == 0 and seq % CONV_TILE == 0
    assert w_in.shape[0] == DEPTH
    x2 = x.reshape(n, d)
    row = lambda v: v.reshape(1, -1).astype(F32)
    slopes = jnp.array([(2.0 ** (-8.0 / N_HEADS)) ** (i + 1) for i in range(N_HEADS)], dtype=F32)

    h = _inproj(x2, row(emb_ln_g), row(emb_ln_b), w_in[0].astype(BF16))
    a = _attention(h, slopes, row(lambda_q1[0]), row(lambda_k1[0]), row(lambda_q2[0]), row(lambda_k2[0]),
                   row(subln_g[0]), batch=batch, seq=seq)
    c = _conv_module(h, conv_w[0], row(conv_b[0]), row(conv_ln_g[0]), row(conv_ln_b[0]), batch=batch, seq=seq)
    rw_t = router_w[0].T
    rw_hi = rw_t.astype(BF16)
    rw_lo = (rw_t - rw_hi.astype(F32)).astype(BF16)
    x1, xs_local, pos_t, gates_t, cc = _mix(
        x2, a, c, w_out[0].astype(BF16), row(emb_ln_g), row(emb_ln_b), row(ln1_g[0]), row(ln1_b[0]),
        jnp.concatenate([rw_hi, rw_lo], axis=0), router_b[0].reshape(N_EXPERTS, 1))

    te, wsel, nv, src, dst, nt = _build_schedule(cc[:, :, 0])
    nb = n // TOKEN_TILE
    y_chunks = _moe(xs_local.reshape(nb * CHUNKS_PER_BLOCK, BF16_ROWS, d), te, wsel, nv, src, dst, nt,
                    w_gate_up[0], b_gate_up[0].reshape(N_EXPERTS, 1, -1), w_down[0],
                    b_down[0].reshape(N_EXPERTS, 1, -1))
    y_local = y_chunks.reshape(-1, d)
    out = _combine(y_local, jnp.transpose(pos_t, (0, 2, 1)), jnp.transpose(gates_t, (0, 2, 1)), x1,
                   row(ln2_g[0]), row(ln2_b[0]))
    return out.reshape(batch, seq, d)
```

```python
import functools

import jax
import jax.numpy as jnp
from jax import lax
from jax.experimental import pallas as pl
from jax.experimental.pallas import tpu as pltpu

F32 = jnp.float32
BF16 = jnp.bfloat16
I32 = jnp.int32

N_HEADS = 4
HEAD_DIM = 64
V_DIM = 128
ATTN_WIDTH = N_HEADS * V_DIM
CONV_WIDTH = 512
CONV_KERNEL = 31
N_EXPERTS = 32
TOP_K = 4
SWIGLU_LIMIT = 7.0
SWIGLU_ALPHA = 1.702
LN_EPS = 1e-5
DEPTH = 1
DEEPNORM_ALPHA = (2.0 * DEPTH) ** 0.25
LAMBDA_INIT = 0.2
LOG2_E = 1.4426950408889634
Q_COLS = N_HEADS * 2 * HEAD_DIM
Q_SCALE = HEAD_DIM ** -0.5 * LOG2_E

LANES = 128
BF16_ROWS = 16

TOKEN_TILE = 512
INPROJ_TILE = 1024
ATTN_KV_BLOCK = 256
ATTN_SUB = 256
CONV_TILE = 128
CONV_ROW_STRIDE = 4
CONV_GLU_TILE = 128
MOE_TILE = 1024
MOE_ROW_QUANTUM = 256
CHUNKS_PER_TILE = MOE_TILE // BF16_ROWS
LOCAL_ROWS = TOP_K * TOKEN_TILE + N_EXPERTS * BF16_ROWS
CHUNKS_PER_BLOCK = LOCAL_ROWS // BF16_ROWS


def _cparams(semantics, vmem_mib):
    return pltpu.CompilerParams(dimension_semantics=semantics, vmem_limit_bytes=vmem_mib * 1024 * 1024)


def _layer_norm(x, g, b):
    mu = jnp.mean(x, axis=-1, keepdims=True)
    xc = x - mu
    var = jnp.mean(xc * xc, axis=-1, keepdims=True)
    return xc * lax.rsqrt(var + LN_EPS) * g + b


def _sigmoid(x):
    return 1.0 / (1.0 + jnp.exp(-x))


def _inproj_body(x_ref, g_ref, b_ref, w_ref, h_ref, *, n_chunk):
    xn = _layer_norm(x_ref[...], g_ref[...], b_ref[...]).astype(BF16)
    for j in range(w_ref.shape[1] // n_chunk):
        sl = slice(j * n_chunk, (j + 1) * n_chunk)
        hj = jnp.dot(xn, w_ref[:, sl], preferred_element_type=F32)
        if (j + 1) * n_chunk <= Q_COLS:
            hj = hj * Q_SCALE
        h_ref[:, sl] = hj.astype(BF16)


def _inproj(x2, g, b, w_bf16):
    n, d = x2.shape
    cols = w_bf16.shape[1]
    tm = INPROJ_TILE
    return pl.pallas_call(
        functools.partial(_inproj_body, n_chunk=512),
        grid=(n // tm,),
        in_specs=[
            pl.BlockSpec((tm, d), lambda i: (i, 0)),
            pl.BlockSpec((1, d), lambda i: (0, 0)),
            pl.BlockSpec((1, d), lambda i: (0, 0)),
            pl.BlockSpec((d, cols), lambda i: (0, 0)),
        ],
        out_specs=pl.BlockSpec((tm, cols), lambda i: (i, 0)),
        out_shape=jax.ShapeDtypeStruct((n, cols), BF16),
        compiler_params=_cparams(("arbitrary",), 40),
        name="inproj",
    )(x2, g, b, w_bf16)


def _attn_body(slopes_ref, lq1_ref, lk1_ref, lq2_ref, lk2_ref, g_ref, q_ref, k_ref, v_ref, o_ref,
               q2_ref, m_ref, l_ref, acc_ref, *, seq):
    head = pl.program_id(1)
    slope = slopes_ref[head] * LOG2_E
    lam =(jnp.exp(jnp.sum(lq1_ref[...] * lk1_ref[...], axis=-1, keepdims=True))
           - jnp.exp(jnp.sum(lq2_ref[...] * lk2_ref[...], axis=-1, keepdims=True))
           + LAMBDA_INIT)

    q = q_ref[...]
    lane = lax.broadcasted_iota(I32, q.shape, 1)
    zero = jnp.zeros_like(q)
    q2_ref[0:seq, :] = jnp.where(lane < HEAD_DIM, q, zero)
    q2_ref[seq:, :] = jnp.where(lane >= HEAD_DIM, q, zero)
    m_ref[...] = jnp.full(m_ref.shape, -jnp.inf, F32)
    l_ref[...] = jnp.zeros(l_ref.shape, F32)
    acc_ref[...] = jnp.zeros(acc_ref.shape, F32)
    kb_rows = ATTN_KV_BLOCK
    sub = ATTN_SUB
    ones = jnp.ones((kb_rows, V_DIM), BF16)
    col = lax.broadcasted_iota(I32, (1, kb_rows), 1)

    for d in range(seq // kb_rows):
        k0 = d * kb_rows
        kb = k_ref[k0:k0 + kb_rows, :]
        vext = jnp.concatenate([v_ref[k0:k0 + kb_rows, :], ones], axis=1)
        for half in range(2):
            for i in range(k0 // sub, seq // sub):
                r0 = i * sub
                rows = slice(half * seq + r0, half * seq + r0 + sub)
                ncol = min(kb_rows, -(-(r0 + sub - k0) // LANES) * LANES)
                bias = (col[:, 0:ncol] + (k0 - r0)).astype(F32) * slope
                s = lax.dot_general(q2_ref[rows, :], kb[0:ncol, :], (((1,), (1,)), ((), ())),
                                    preferred_element_type=F32) + bias
                if k0 + ncol - 1 > r0:
                    row = lax.broadcasted_iota(I32, s.shape, 0) + (r0 - k0)
                    s = jnp.where(lax.broadcasted_iota(I32, s.shape, 1) <= row, s, -jnp.inf)
                m_prev = m_ref[rows, :]
                m_new = jnp.maximum(m_prev, jnp.max(s, axis=1, keepdims=True))
                alpha = jnp.exp2(m_prev - m_new)
                p = jnp.exp2(s - jnp.concatenate([m_new] * (ncol // LANES), axis=1))
                pv = jnp.dot(p.astype(BF16), vext[0:ncol, :], preferred_element_type=F32)
                acc_ref[rows, :] = alpha * acc_ref[rows, :] + pv[:, :V_DIM]
                l_ref[rows, :] = alpha * l_ref[rows, :] + pv[:, V_DIM:]
                m_ref[rows, :] = m_new

    o = acc_ref[...] / l_ref[...]
    od = o[:seq] - lam * o[seq:]
    ms = jnp.mean(od * od, axis=-1, keepdims=True)
    out = od * lax.rsqrt(ms + LN_EPS) * g_ref[...] * (1.0 - LAMBDA_INIT)
    o_ref[...] = out.astype(BF16)


def _attention(h, slopes, lq1, lk1, lq2, lk2, subln_g, *, batch, seq):
    n = h.shape[0]
    vec = lambda: pl.BlockSpec((1, HEAD_DIM), lambda b, hd, *_: (0, 0))
    grid_spec = pltpu.PrefetchScalarGridSpec(
        num_scalar_prefetch=1,
        grid=(batch, N_HEADS),
        in_specs=[
            vec(), vec(), vec(), vec(),
            pl.BlockSpec((1, V_DIM), lambda b, hd, *_: (0, 0)),
            pl.BlockSpec((seq, V_DIM), lambda b, hd, *_: (b, hd)),
            pl.BlockSpec((seq, V_DIM), lambda b, hd, *_: (b, N_HEADS + hd)),
            pl.BlockSpec((seq, V_DIM), lambda b, hd, *_: (b, 2 * N_HEADS + hd)),
        ],
        out_specs=pl.BlockSpec((seq, V_DIM), lambda b, hd, *_: (b, hd)),
        scratch_shapes=[
            pltpu.VMEM((2 * seq, V_DIM), BF16),
            pltpu.VMEM((2 * seq, LANES), F32),
            pltpu.VMEM((2 * seq, LANES), F32),
            pltpu.VMEM((2 * seq, V_DIM), F32),
        ],
    )
    return pl.pallas_call(
        functools.partial(_attn_body, seq=seq),
        grid_spec=grid_spec,
        out_shape=jax.ShapeDtypeStruct((n, ATTN_WIDTH), BF16),
        compiler_params=_cparams(("arbitrary", "arbitrary"), 32),
        name="diff_attn",
    )(slopes, lq1, lk1, lq2, lk2, subln_g, h, h, h)


CONV_PAD = 32


def _conv_body(a_ref, g_ref, w_ref, cb_ref, lg_ref, lb_ref, o_ref, glu_ref, conv_ref, *, seq):
    n_slab = CONV_WIDTH // LANES
    for c in range(n_slab):
        glu_ref[c, 0:CONV_PAD, :] = jnp.zeros((CONV_PAD, LANES), F32)

    def glu(i, _):
        st = pl.multiple_of(i * CONV_GLU_TILE, CONV_GLU_TILE)
        a = a_ref[pl.ds(st, CONV_GLU_TILE), :].astype(F32)
        g = g_ref[pl.ds(st, CONV_GLU_TILE), :].astype(F32)
        val = a * _sigmoid(g)
        for c in range(n_slab):
            glu_ref[c, pl.ds(CONV_PAD + st, CONV_GLU_TILE), :] = val[:, c * LANES:(c + 1) * LANES]
        return 0

    lax.fori_loop(0, seq // CONV_GLU_TILE, glu, 0, unroll=2)

    span = 8 * CONV_ROW_STRIDE
    for c in range(n_slab):
        lanes = slice(c * LANES, (c + 1) * LANES)
        taps = [jnp.broadcast_to(w_ref[j:j + 1, lanes], (8, LANES)) for j in range(CONV_KERNEL)]
        bias = jnp.broadcast_to(cb_ref[:, lanes], (8, LANES))

        def conv(i, _):
            st = i * CONV_TILE
            for grp in range(CONV_TILE // span):
                for p in range(CONV_ROW_STRIDE):
                    t0 = st + grp * span + p
                    acc = bias
                    for j in range(CONV_KERNEL):
                        src = pl.ds(t0 + (CONV_PAD - (CONV_KERNEL - 1) + j), 8, stride=CONV_ROW_STRIDE)
                        acc = acc + glu_ref[c, src, :] * taps[j]
                    conv_ref[c, pl.ds(t0, 8, stride=CONV_ROW_STRIDE), :] = acc
            return 0

        lax.fori_loop(0, seq // CONV_TILE, conv, 0)

    def norm(i, _):
        st = pl.multiple_of(i * CONV_GLU_TILE, CONV_GLU_TILE)
        x = jnp.concatenate([conv_ref[c, pl.ds(st, CONV_GLU_TILE), :] for c in range(n_slab)], axis=1)
        y = _layer_norm(x, lg_ref[...], lb_ref[...])
        o_ref[pl.ds(st, CONV_GLU_TILE), :] = (y * _sigmoid(y)).astype(BF16)
        return 0

    lax.fori_loop(0, seq // CONV_GLU_TILE, norm, 0, unroll=2)


def _conv_module(h, conv_w, conv_b, ln_g, ln_b, *, batch, seq):
    n = h.shape[0]
    u_col = 3 * ATTN_WIDTH // CONV_WIDTH
    row = lambda: pl.BlockSpec((1, CONV_WIDTH), lambda b: (0, 0))
    return pl.pallas_call(
        functools.partial(_conv_body, seq=seq),
        grid=(batch,),
        in_specs=[
            pl.BlockSpec((seq, CONV_WIDTH), lambda b: (b, u_col)),
            pl.BlockSpec((seq, CONV_WIDTH), lambda b: (b, u_col + 1)),
            pl.BlockSpec((CONV_KERNEL, CONV_WIDTH), lambda b: (0, 0)),
            row(), row(), row(),
        ],
        out_specs=pl.BlockSpec((seq, CONV_WIDTH), lambda b: (b, 0)),
        out_shape=jax.ShapeDtypeStruct((n, CONV_WIDTH), BF16),
        scratch_shapes=[
            pltpu.VMEM((CONV_WIDTH // LANES, CONV_PAD + seq, LANES), F32),
            pltpu.VMEM((CONV_WIDTH // LANES, seq, LANES), F32),
        ],
        compiler_params=_cparams(("arbitrary",), 32),
        name="conformer_conv",
    )(h, h, conv_w, conv_b, ln_g, ln_b)


def _mix_body(x_ref, a_ref, c_ref, wo_ref, g0_ref, b0_ref, g1_ref, b1_ref, rw2_ref, rb_ref,
              x1_ref, xs_ref, pos_ref, gate_ref, cc_ref):
    tb = x_ref.shape[0]
    xn = _layer_norm(x_ref[...], g0_ref[...], b0_ref[...])
    mix = (jnp.dot(a_ref[...], wo_ref[0:ATTN_WIDTH, :], preferred_element_type=F32)
           + jnp.dot(c_ref[...], wo_ref[ATTN_WIDTH:, :], preferred_element_type=F32))
    x1 = _layer_norm(DEEPNORM_ALPHA * xn + mix, g1_ref[...], b1_ref[...])
    x1_ref[...] = x1
    x1b = x1.astype(BF16)

    x_lo = (x1 - x1b.astype(F32)).astype(BF16)
    rw2 = rw2_ref[...]
    nt_dims = (((1,), (1,)), ((), ()))
    both = lax.dot_general(rw2, x1b, nt_dims, preferred_element_type=F32)
    logits = ((lax.dot_general(rw2[:N_EXPERTS], x_lo, nt_dims, preferred_element_type=F32) + both[N_EXPERTS:])
              + both[:N_EXPERTS]) + rb_ref[...]
    eidx = lax.broadcasted_iota(I32, logits.shape, 0)
    work = logits
    vals, hots = [], []
    for _ in range(TOP_K):
        mx = jnp.max(work, axis=0, keepdims=True)
        sel = jnp.min(jnp.where(work == mx, eidx, N_EXPERTS), axis=0, keepdims=True)
        hot = eidx == sel
        vals.append(mx)
        hots.append(hot)
        work = jnp.where(hot, -jnp.inf, work)
    ex = [jnp.exp(v - vals[0]) for v in vals]
    den = ex[0] + ex[1] + ex[2] + ex[3]
    for k in range(TOP_K):
        gate_ref[k:k + 1, :] = ex[k] / den

    member = jnp.zeros(logits.shape, F32)
    for hot in hots:
        member = member + jnp.where(hot, 1.0, 0.0)
    before = (lax.broadcasted_iota(I32, (tb, tb), 0) < lax.broadcasted_iota(I32, (tb, tb), 1))
    rank = jnp.dot(member.astype(BF16), jnp.where(before, 1.0, 0.0).astype(BF16), preferred_element_type=F32)
    count = jnp.sum(member, axis=1, keepdims=True)
    chunks = jnp.floor((count + (BF16_ROWS - 1.0)) * (1.0 / BF16_ROWS))
    cc_ref[...] = jnp.broadcast_to(chunks, cc_ref.shape).astype(I32)
    lower = (lax.broadcasted_iota(I32, (N_EXPERTS, N_EXPERTS), 1) < lax.broadcasted_iota(I32, (N_EXPERTS, N_EXPERTS), 0))
    seg_start = jnp.dot(jnp.where(lower, 1.0, 0.0).astype(BF16),
                        jnp.broadcast_to(chunks, (N_EXPERTS, LANES)).astype(BF16),
                        preferred_element_type=F32)[:, 0:1]
    slot = seg_start * float(BF16_ROWS) + rank
    pos = [jnp.sum(jnp.where(hot, slot, 0.0), axis=0, keepdims=True).astype(I32) for hot in hots]
    for k in range(TOP_K):
        pos_ref[k:k + 1, :] = pos[k]

    rows = 256
    one, nil = jnp.ones((), BF16), jnp.zeros((), BF16)
    pos16 = [p.astype(jnp.int16) for p in pos]
    for rc in range(xs_ref.shape[0] // rows):
        r = lax.broadcasted_iota(jnp.int16, (rows, tb), 0) + jnp.int16(rc * rows)
        hit = (r == pos16[0]) | (r == pos16[1]) | (r == pos16[2]) | (r == pos16[3])
        sel = jnp.where(hit, one, nil)
        xs_ref[rc * rows:(rc + 1) * rows, :] = jnp.dot(sel, x1b, preferred_element_type=F32).astype(BF16)


def _mix(x2, a, c, wo_bf16, g0, b0, g1, b1, rw2, rb):
    n, d = x2.shape
    tb = TOKEN_TILE
    nb = n // tb
    row = lambda: pl.BlockSpec((1, d), lambda i: (0, 0))
    return pl.pallas_call(
        _mix_body,
        grid=(nb,),
        in_specs=[
            pl.BlockSpec((tb, d), lambda i: (i, 0)),
            pl.BlockSpec((tb, ATTN_WIDTH), lambda i: (i, 0)),
            pl.BlockSpec((tb, CONV_WIDTH), lambda i: (i, 0)),
            pl.BlockSpec((d, d), lambda i: (0, 0)),
            row(), row(), row(), row(),
            pl.BlockSpec((2 * N_EXPERTS, d), lambda i: (0, 0)),
            pl.BlockSpec((N_EXPERTS, 1), lambda i: (0, 0)),
        ],
        out_specs=[
            pl.BlockSpec((tb, d), lambda i: (i, 0)),
            pl.BlockSpec((None, LOCAL_ROWS, d), lambda i: (i, 0, 0)),
            pl.BlockSpec((None, TOP_K, tb), lambda i: (i, 0, 0)),
            pl.BlockSpec((None, TOP_K, tb), lambda i: (i, 0, 0)),
            pl.BlockSpec((None, N_EXPERTS, LANES), lambda i: (i, 0, 0)),
        ],
        out_shape=[
            jax.ShapeDtypeStruct((n, d), F32),
            jax.ShapeDtypeStruct((nb, LOCAL_ROWS, d), BF16),
            jax.ShapeDtypeStruct((nb, TOP_K, tb), I32),
            jax.ShapeDtypeStruct((nb, TOP_K, tb), F32),
            jax.ShapeDtypeStruct((nb, N_EXPERTS, LANES), I32),
        ],
        compiler_params=_cparams(("arbitrary",), 48),
        name="mix_router_sort",
    )(x2, a, c, wo_bf16, g0, b0, g1, b1, rw2, rb)


def _max_tiles(nb):
    return (nb * CHUNKS_PER_BLOCK + CHUNKS_PER_TILE - 1) // CHUNKS_PER_TILE + N_EXPERTS + 1


def _build_schedule(cc):
    nb = cc.shape[0]
    cpt = CHUNKS_PER_TILE
    ne1 = N_EXPERTS + 1
    tail = CHUNKS_PER_BLOCK - jnp.sum(cc, axis=1)
    cce = jnp.concatenate([cc, tail[:, None]], axis=1)
    seg_start = jnp.cumsum(cce, axis=1) - cce
    tot = jnp.sum(cce, axis=0)
    tiles_e = (tot + cpt - 1) // cpt
    tile_end = jnp.cumsum(tiles_e)
    max_tiles = _max_tiles(nb)

    def lookup(passed, table):
        return table[0] + jnp.sum(jnp.where(passed, (table[1:] - table[:-1])[None, :], 0), axis=1)

    t = jnp.arange(max_tiles, dtype=I32)
    passed = tile_end[None, :] <= t[:, None]
    te = jnp.minimum(jnp.sum(passed, axis=1), ne1 - 1).astype(I32)
    experts = jnp.arange(ne1, dtype=I32)
    has_tiles = (tiles_e > 0) & (experts < N_EXPERTS)
    later = jnp.where(has_tiles[None, :] & (experts[None, :] > te[:, None]), experts[None, :], ne1)
    nxt = jnp.min(later, axis=1)
    own = jnp.where(te < N_EXPERTS, te, jnp.max(jnp.where(has_tiles, experts, 0)))
    is_first = jnp.any(((tile_end - tiles_e)[None, :] == t[:, None]) & (tiles_e > 0)[None, :], axis=1)
    wsel = jnp.where(is_first | (nxt >= N_EXPERTS), own, nxt).astype(I32)

    pad_len = tiles_e * cpt - tot
    seg_len = jnp.concatenate([cce.T, pad_len[:, None]], axis=1).reshape(-1)
    seg_end = jnp.cumsum(seg_len)
    block = jnp.arange(nb, dtype=I32)[None, :]
    is_real = jnp.concatenate([jnp.ones((ne1, nb), I32), jnp.zeros((ne1, 1), I32)], axis=1).reshape(-1)
    seg_first = (seg_end - seg_len) * is_real
    zero = jnp.zeros((1,), I32)

    base = block * CHUNKS_PER_BLOCK + seg_start.T
    first = jnp.concatenate([jnp.concatenate([base, jnp.zeros((ne1, 1), I32)], axis=1).reshape(-1) - seg_first,
                             zero])
    p = jnp.arange(max_tiles * cpt, dtype=I32)
    passed = seg_end[None, :] <= p[:, None]
    real = lookup(passed, jnp.concatenate([is_real, zero]))
    src = (real * p + lookup(passed, first)).astype(I32)
    dump0 = nb * CHUNKS_PER_BLOCK
    dst = jnp.where(real > 0, src, dump0 + ((p // cpt) % 2) * cpt + p % cpt).astype(I32)
    dummy = dump0 + cpt + jnp.arange(cpt, dtype=I32)
    nv = jnp.sum(real.reshape(max_tiles, cpt), axis=1).astype(I32)
    return te, wsel, nv, src, jnp.concatenate([dummy, dst]), jnp.reshape(tile_end[-1], (1,)).astype(I32)


def _moe_body(te_ref, wsel_ref, nv_ref, src_ref, dst_ref, nt_ref, xs_hbm, wgu_ref, bgu_ref, wd_ref, bd_ref, y_hbm,
              xbuf, ybuf, act, wgu_b, wd_b, in_sem, out_sem):
    t = pl.program_id(0)
    last = nt_ref[0] - 1
    cpt = CHUNKS_PER_TILE
    f = wd_ref.shape[0]
    slot = t % 2
    other = 1 - slot
    e = te_ref[t]

    def rows(c):
        return pl.ds(c * BF16_ROWS, BF16_ROWS)

    def in_copy(tile, buf, c):
        return pltpu.make_async_copy(xs_hbm.at[src_ref[tile * cpt + c]], xbuf.at[buf, rows(c)], in_sem.at[buf])

    def out_copy(tile, buf, c):
        return pltpu.make_async_copy(ybuf.at[buf, rows(c)], y_hbm.at[dst_ref[(tile + 1) * cpt + c]],
                                     out_sem.at[buf])

    def start_all(copy, tile, buf):
        for c in range(cpt):
            copy(tile, buf, c).start()

    def wait_all(copy, tile, buf):
        for c in range(cpt):
            copy(tile, buf, c).wait()

    @pl.when(t == 0)
    def _():
        start_all(in_copy, 0, 0)
        ybuf[...] = jnp.zeros(ybuf.shape, BF16)
        even_dump = [pltpu.make_async_copy(ybuf.at[1, rows(c)], y_hbm.at[y_hbm.shape[0] - 2 * cpt + c],
                                           out_sem.at[0]) for c in range(cpt)]
        for copy in even_dump:
            copy.start()
        for copy in even_dump:
            copy.wait()

    @pl.when(t <= last)
    def _():
        @pl.when(t >= 1)
        def _():
            wait_all(out_copy, t - 2, slot)

        wait_all(in_copy, t, slot)
        start_all(in_copy, jnp.minimum(t + 1, last), other)
        start_all(out_copy, t - 1, other)

        @pl.when((e < N_EXPERTS) & ((t == 0) | (te_ref[jnp.maximum(t - 1, 0)] != e)))
        def _():
            wgu_b[...] = wgu_ref[...].astype(BF16)
            wd_b[...] = wd_ref[...].astype(BF16)

        def expert_mlp(m):
            x = xbuf[slot, 0:m, :]
            nc = 512
            for j in range(f // nc):
                gsl = slice(j * nc, (j + 1) * nc)
                usl = slice(f + j * nc, f + (j + 1) * nc)
                g = jnp.dot(x, wgu_b[:, gsl], preferred_element_type=F32) + bgu_ref[:, gsl]
                u = jnp.dot(x, wgu_b[:, usl], preferred_element_type=F32) + bgu_ref[:, usl]
                g = jnp.minimum(g, SWIGLU_LIMIT)
                u = jnp.clip(u, -SWIGLU_LIMIT, SWIGLU_LIMIT)
                act[0:m, gsl] = (g * _sigmoid(SWIGLU_ALPHA * g) * (u + 1.0)).astype(BF16)
            a = act[0:m, :]
            for j in range(wd_ref.shape[1] // nc):
                sl = slice(j * nc, (j + 1) * nc)
                y = jnp.dot(a, wd_b[:, sl], preferred_element_type=F32) + bd_ref[:, sl]
                ybuf[slot, 0:m, sl] = y.astype(BF16)

        quanta = (nv_ref[t] * BF16_ROWS + (MOE_ROW_QUANTUM - 1)) // MOE_ROW_QUANTUM
        for k in range(1, MOE_TILE // MOE_ROW_QUANTUM + 1):
            pl.when((e < N_EXPERTS) & (quanta == k))(functools.partial(expert_mlp, k * MOE_ROW_QUANTUM))

        @pl.when(e >= N_EXPERTS)
        def _():
            ybuf[slot] = jnp.zeros(ybuf.shape[1:], BF16)

        @pl.when(t == last)
        def _():
            start_all(out_copy, t, slot)
            wait_all(out_copy, t - 1, other)
            wait_all(out_copy, t, slot)
            wait_all(in_copy, t, other)


def _moe(xs_chunks, te, wsel, nv, src, dst, nt, w_gate_up, b_gate_up, w_down, b_down):
    nch, _, d = xs_chunks.shape
    f = w_down.shape[1]
    max_tiles = te.shape[0]
    by_wsel = lambda t, te_r, wsel_r, nv_r, src_r, dst_r, nt_r: (wsel_r[t], 0, 0)
    by_tile = lambda t, te_r, wsel_r, nv_r, src_r, dst_r, nt_r: (jnp.minimum(te_r[t], N_EXPERTS - 1), 0, 0)
    grid_spec = pltpu.PrefetchScalarGridSpec(
        num_scalar_prefetch=6,
        grid=(max_tiles,),
        in_specs=[
            pl.BlockSpec(memory_space=pl.ANY),
            pl.BlockSpec((None, d, 2 * f), by_wsel),
            pl.BlockSpec((None, 1, 2 * f), by_tile),
            pl.BlockSpec((None, f, d), by_wsel),
            pl.BlockSpec((None, 1, d), by_tile),
        ],
        out_specs=pl.BlockSpec(memory_space=pl.ANY),
        scratch_shapes=[
            pltpu.VMEM((2, MOE_TILE, d), BF16),
            pltpu.VMEM((2, MOE_TILE, d), BF16),
            pltpu.VMEM((MOE_TILE, f), BF16),
            pltpu.VMEM((d, 2 * f), BF16),
            pltpu.VMEM((f, d), BF16),
            pltpu.SemaphoreType.DMA((2,)),
            pltpu.SemaphoreType.DMA((2,)),
        ],
    )
    return pl.pallas_call(
        _moe_body,
        grid_spec=grid_spec,
        out_shape=jax.ShapeDtypeStruct((nch + 2 * CHUNKS_PER_TILE, BF16_ROWS, d), BF16),
        compiler_params=_cparams(("arbitrary",), 56),
        name="moe_grouped",
    )(te, wsel, nv, src, dst, nt, xs_chunks, w_gate_up, b_gate_up, w_down, b_down)


def _combine_body(y_ref, pos_ref, gate_ref, x1_ref, g_ref, b_ref, o_ref):
    tb = x1_ref.shape[0]
    pos = pos_ref[...]
    gate = gate_ref[...]
    cols = 512
    acc = jnp.zeros(x1_ref.shape, F32)
    for rc in range(y_ref.shape[0] // cols):
        r = lax.broadcasted_iota(I32, (tb, cols), 1) + rc * cols
        w = jnp.zeros((tb, cols), F32)
        for k in range(TOP_K):
            w = w + jnp.where(r == pos[:, k:k + 1], gate[:, k:k + 1], 0.0)
        acc = acc + jnp.dot(w.astype(BF16), y_ref[rc * cols:(rc + 1) * cols, :], preferred_element_type=F32)
    o_ref[...] = _layer_norm(DEEPNORM_ALPHA * x1_ref[...] + acc, g_ref[...], b_ref[...])


def _combine(y_local, pos, gates, x1, g, b):
    n, d = x1.shape
    tb = TOKEN_TILE
    nb = n // tb
    row = lambda: pl.BlockSpec((1, d), lambda i: (0, 0))
    return pl.pallas_call(
        _combine_body,
        grid=(nb,),
        in_specs=[
            pl.BlockSpec((LOCAL_ROWS, d), lambda i: (i, 0)),
            pl.BlockSpec((None, tb, TOP_K), lambda i: (i, 0, 0)),
            pl.BlockSpec((None, tb, TOP_K), lambda i: (i, 0, 0)),
            pl.BlockSpec((tb, d), lambda i: (i, 0)),
            row(), row(),
        ],
        out_specs=pl.BlockSpec((tb, d), lambda i: (i, 0)),
        out_shape=jax.ShapeDtypeStruct((n, d), F32),
        compiler_params=_cparams(("arbitrary",), 40),
        name="combine_ln",
    )(y_local, pos, gates, x1, g, b)


def kernel(x, emb_ln_g, emb_ln_b, w_in, lambda_q1, lambda_k1, lambda_q2, lambda_k2, subln_g, conv_w, conv_b,
           conv_ln_g, conv_ln_b, w_out, ln1_g, ln1_b, router_w, router_b, w_gate_up, b_gate_up, w_down, b_down,
           ln2_g, ln2_b):
    batch, seq, d = x.shape
    n = batch * seq
    assert n % TOKEN_TILE == 0 and seq % ATTN_KV_BLOCK == 0 and seq % CONV_TILE == 0
    assert w_in.shape[0] == DEPTH
    x2 = x.reshape(n, d)
    row = lambda v: v.reshape(1, -1).astype(F32)
    slopes = jnp.array([(2.0 ** (-8.0 / N_HEADS)) ** (i + 1) for i in range(N_HEADS)], dtype=F32)

    h = _inproj(x2, row(emb_ln_g), row(emb_ln_b), w_in[0].astype(BF16))
    a = _attention(h, slopes, row(lambda_q1[0]), row(lambda_k1[0]), row(lambda_q2[0]), row(lambda_k2[0]),
                   row(subln_g[0]), batch=batch, seq=seq)
    c = _conv_module(h, conv_w[0], row(conv_b[0]), row(conv_ln_g[0]), row(conv_ln_b[0]), batch=batch, seq=seq)
    rw_t = router_w[0].T
    rw_hi = rw_t.astype(BF16)
    rw_lo = (rw_t - rw_hi.astype(F32)).astype(BF16)
    x1, xs_local, pos_t, gates_t, cc = _mix(
        x2, a, c, w_out[0].astype(BF16), row(emb_ln_g), row(emb_ln_b), row(ln1_g[0]), row(ln1_b[0]),
        jnp.concatenate([rw_hi, rw_lo], axis=0), router_b[0].reshape(N_EXPERTS, 1))

    te, wsel, nv, src, dst, nt = _build_schedule(cc[:, :, 0])
    nb = n // TOKEN_TILE
    y_chunks = _moe(xs_local.reshape(nb * CHUNKS_PER_BLOCK, BF16_ROWS, d), te, wsel, nv, src, dst, nt,
                    w_gate_up[0], b_gate_up[0].reshape(N_EXPERTS, 1, -1), w_down[0],
                    b_down[0].reshape(N_EXPERTS, 1, -1))
    y_local = y_chunks.reshape(-1, d)
    out = _combine(y_local, jnp.transpose(pos_t, (0, 2, 1)), jnp.transpose(gates_t, (0, 2, 1)), x1,
                   row(ln2_g[0]), row(ln2_b[0]))
    return out.reshape(batch, seq, d)
```

```python
import functools

import jax
import jax.numpy as jnp
from jax import lax
from jax.experimental import pallas as pl
from jax.experimental.pallas import tpu as pltpu

F32 = jnp.float32
BF16 = jnp.bfloat16
I32 = jnp.int32

N_HEADS = 4
HEAD_DIM = 64
V_DIM = 128
ATTN_WIDTH = N_HEADS * V_DIM
CONV_WIDTH = 512
CONV_KERNEL = 31
N_EXPERTS = 32
TOP_K = 4
SWIGLU_LIMIT = 7.0
SWIGLU_ALPHA = 1.702
LN_EPS = 1e-5
DEPTH = 1
DEEPNORM_ALPHA = (2.0 * DEPTH) ** 0.25
LAMBDA_INIT = 0.2
LOG2_E = 1.4426950408889634
Q_COLS = N_HEADS * 2 * HEAD_DIM
Q_SCALE = HEAD_DIM ** -0.5 * LOG2_E

LANES = 128
BF16_ROWS = 16

TOKEN_TILE = 512
INPROJ_TILE = 1024
ATTN_KV_BLOCK = 256
ATTN_SUB = 256
CONV_TILE = 128
CONV_ROW_STRIDE = 4
CONV_GLU_TILE = 128
MOE_TILE = 1024
MOE_ROW_QUANTUM = 256
CHUNKS_PER_TILE = MOE_TILE // BF16_ROWS
LOCAL_ROWS = TOP_K * TOKEN_TILE + N_EXPERTS * BF16_ROWS
CHUNKS_PER_BLOCK = LOCAL_ROWS // BF16_ROWS
COMBINE_BLOCKS = 2


def _cparams(semantics, vmem_mib):
    return pltpu.CompilerParams(dimension_semantics=semantics, vmem_limit_bytes=vmem_mib * 1024 * 1024)


def _layer_norm(x, g, b):
    mu = jnp.mean(x, axis=-1, keepdims=True)
    xc = x - mu
    var = jnp.mean(xc * xc, axis=-1, keepdims=True)
    return xc * lax.rsqrt(var + LN_EPS) * g + b


def _sigmoid(x):
    return 1.0 / (1.0 + jnp.exp(-x))


def _inproj_body(x_ref, g_ref, b_ref, w_ref, h_ref, *, n_chunk):
    half = x_ref.shape[0] // 2
    for r0 in (0, half):
        xn = _layer_norm(x_ref[r0:r0 + half, :], g_ref[...], b_ref[...]).astype(BF16)
        for j in range(w_ref.shape[1] // n_chunk):
            sl = slice(j * n_chunk, (j + 1) * n_chunk)
            hj = jnp.dot(xn, w_ref[:, sl], preferred_element_type=F32)
            if (j + 1) * n_chunk <= Q_COLS:
                hj = hj * Q_SCALE
            h_ref[r0:r0 + half, sl] = hj.astype(BF16)


def _inproj(x2, g, b, w_bf16):
    n, d = x2.shape
    cols = w_bf16.shape[1]
    tm = INPROJ_TILE
    return pl.pallas_call(
        functools.partial(_inproj_body, n_chunk=512),
        grid=(n // tm,),
        in_specs=[
            pl.BlockSpec((tm, d), lambda i: (i, 0)),
            pl.BlockSpec((1, d), lambda i: (0, 0)),
            pl.BlockSpec((1, d), lambda i: (0, 0)),
            pl.BlockSpec((d, cols), lambda i: (0, 0)),
        ],
        out_specs=pl.BlockSpec((tm, cols), lambda i: (i, 0)),
        out_shape=jax.ShapeDtypeStruct((n, cols), BF16),
        compiler_params=_cparams(("arbitrary",), 40),
        name="inproj",
    )(x2, g, b, w_bf16)


def _attn_body(slopes_ref, lq1_ref, lk1_ref, lq2_ref, lk2_ref, g_ref, q_ref, k_ref, v_ref, o_ref,
               q2_ref, m_ref, l_ref, acc_ref, *, seq):
    head = pl.program_id(1)
    slope = slopes_ref[head] * LOG2_E
    lam =(jnp.exp(jnp.sum(lq1_ref[...] * lk1_ref[...], axis=-1, keepdims=True))
           - jnp.exp(jnp.sum(lq2_ref[...] * lk2_ref[...], axis=-1, keepdims=True))
           + LAMBDA_INIT)

    q = q_ref[...]
    lane = lax.broadcasted_iota(I32, q.shape, 1)
    zero = jnp.zeros_like(q)
    q2_ref[0:seq, :] = jnp.where(lane < HEAD_DIM, q, zero)
    q2_ref[seq:, :] = jnp.where(lane >= HEAD_DIM, q, zero)
    m_ref[...] = jnp.full(m_ref.shape, -jnp.inf, F32)
    l_ref[...] = jnp.zeros(l_ref.shape, F32)
    acc_ref[...] = jnp.zeros(acc_ref.shape, F32)
    kb_rows = ATTN_KV_BLOCK
    sub = ATTN_SUB
    ones = jnp.ones((kb_rows, V_DIM), BF16)
    col = lax.broadcasted_iota(I32, (1, kb_rows), 1)

    for d in range(seq // kb_rows):
        k0 = d * kb_rows
        kb = k_ref[k0:k0 + kb_rows, :]
        vext = jnp.concatenate([v_ref[k0:k0 + kb_rows, :], ones], axis=1)
        for half in range(2):
            for i in range(k0 // sub, seq // sub):
                r0 = i * sub
                rows = slice(half * seq + r0, half * seq + r0 + sub)
                ncol = min(kb_rows, -(-(r0 + sub - k0) // LANES) * LANES)
                bias = (col[:, 0:ncol] + (k0 - r0)).astype(F32) * slope
                s = lax.dot_general(q2_ref[rows, :], kb[0:ncol, :], (((1,), (1,)), ((), ())),
                                    preferred_element_type=F32) + bias
                if k0 + ncol - 1 > r0:
                    row = lax.broadcasted_iota(I32, s.shape, 0) + (r0 - k0)
                    s = jnp.where(lax.broadcasted_iota(I32, s.shape, 1) <= row, s, -jnp.inf)
                m_prev = m_ref[rows, :]
                m_new = jnp.maximum(m_prev, jnp.max(s, axis=1, keepdims=True))
                alpha = jnp.exp2(m_prev - m_new)
                p = jnp.exp2(s - jnp.concatenate([m_new] * (ncol // LANES), axis=1))
                pv = jnp.dot(p.astype(BF16), vext[0:ncol, :], preferred_element_type=F32)
                acc_ref[rows, :] = alpha * acc_ref[rows, :] + pv[:, :V_DIM]
                l_ref[rows, :] = alpha * l_ref[rows, :] + pv[:, V_DIM:]
                m_ref[rows, :] = m_new

    o = acc_ref[...] / l_ref[...]
    od = o[:seq] - lam * o[seq:]
    ms = jnp.mean(od * od, axis=-1, keepdims=True)
    out = od * lax.rsqrt(ms + LN_EPS) * g_ref[...] * (1.0 - LAMBDA_INIT)
    o_ref[...] = out.astype(BF16)


def _attention(h, slopes, lq1, lk1, lq2, lk2, subln_g, *, batch, seq):
    n = h.shape[0]
    vec = lambda: pl.BlockSpec((1, HEAD_DIM), lambda b, hd, *_: (0, 0))
    grid_spec = pltpu.PrefetchScalarGridSpec(
        num_scalar_prefetch=1,
        grid=(batch, N_HEADS),
        in_specs=[
            vec(), vec(), vec(), vec(),
            pl.BlockSpec((1, V_DIM), lambda b, hd, *_: (0, 0)),
            pl.BlockSpec((seq, V_DIM), lambda b, hd, *_: (b, hd)),
            pl.BlockSpec((seq, V_DIM), lambda b, hd, *_: (b, N_HEADS + hd)),
            pl.BlockSpec((seq, V_DIM), lambda b, hd, *_: (b, 2 * N_HEADS + hd)),
        ],
        out_specs=pl.BlockSpec((seq, V_DIM), lambda b, hd, *_: (b, hd)),
        scratch_shapes=[
            pltpu.VMEM((2 * seq, V_DIM), BF16),
            pltpu.VMEM((2 * seq, LANES), F32),
            pltpu.VMEM((2 * seq, LANES), F32),
            pltpu.VMEM((2 * seq, V_DIM), F32),
        ],
    )
    return pl.pallas_call(
        functools.partial(_attn_body, seq=seq),
        grid_spec=grid_spec,
        out_shape=jax.ShapeDtypeStruct((n, ATTN_WIDTH), BF16),
        compiler_params=_cparams(("arbitrary", "arbitrary"), 32),
        name="diff_attn",
    )(slopes, lq1, lk1, lq2, lk2, subln_g, h, h, h)


CONV_PAD = 32


def _conv_body(a_ref, g_ref, w_ref, cb_ref, lg_ref, lb_ref, o_ref, glu_ref, conv_ref, *, seq):
    n_slab = CONV_WIDTH // LANES
    for c in range(n_slab):
        glu_ref[c, 0:CONV_PAD, :] = jnp.zeros((CONV_PAD, LANES), F32)

    def glu(i, _):
        st = pl.multiple_of(i * CONV_GLU_TILE, CONV_GLU_TILE)
        a = a_ref[pl.ds(st, CONV_GLU_TILE), :].astype(F32)
        g = g_ref[pl.ds(st, CONV_GLU_TILE), :].astype(F32)
        val = a * _sigmoid(g)
        for c in range(n_slab):
            glu_ref[c, pl.ds(CONV_PAD + st, CONV_GLU_TILE), :] = val[:, c * LANES:(c + 1) * LANES]
        return 0

    lax.fori_loop(0, seq // CONV_GLU_TILE, glu, 0, unroll=2)

    span = 8 * CONV_ROW_STRIDE
    for c in range(n_slab):
        lanes = slice(c * LANES, (c + 1) * LANES)
        taps = [jnp.broadcast_to(w_ref[j:j + 1, lanes], (8, LANES)) for j in range(CONV_KERNEL)]
        bias = jnp.broadcast_to(cb_ref[:, lanes], (8, LANES))

        def conv(i, _):
            st = i * CONV_TILE
            for grp in range(CONV_TILE // span):
                for p in range(CONV_ROW_STRIDE):
                    t0 = st + grp * span + p
                    acc = bias
                    for j in range(CONV_KERNEL):
                        src = pl.ds(t0 + (CONV_PAD - (CONV_KERNEL - 1) + j), 8, stride=CONV_ROW_STRIDE)
                        acc = acc + glu_ref[c, src, :] * taps[j]
                    conv_ref[c, pl.ds(t0, 8, stride=CONV_ROW_STRIDE), :] = acc
            return 0

        lax.fori_loop(0, seq // CONV_TILE, conv, 0)

    def norm(i, _):
        st = pl.multiple_of(i * CONV_GLU_TILE, CONV_GLU_TILE)
        x = jnp.concatenate([conv_ref[c, pl.ds(st, CONV_GLU_TILE), :] for c in range(n_slab)], axis=1)
        y = _layer_norm(x, lg_ref[...], lb_ref[...])
        o_ref[pl.ds(st, CONV_GLU_TILE), :] = (y * _sigmoid(y)).astype(BF16)
        return 0

    lax.fori_loop(0, seq // CONV_GLU_TILE, norm, 0, unroll=2)


def _conv_module(h, conv_w, conv_b, ln_g, ln_b, *, batch, seq):
    n = h.shape[0]
    u_col = 3 * ATTN_WIDTH // CONV_WIDTH
    row = lambda: pl.BlockSpec((1, CONV_WIDTH), lambda b: (0, 0))
    return pl.pallas_call(
        functools.partial(_conv_body, seq=seq),
        grid=(batch,),
        in_specs=[
            pl.BlockSpec((seq, CONV_WIDTH), lambda b: (b, u_col)),
            pl.BlockSpec((seq, CONV_WIDTH), lambda b: (b, u_col + 1)),
            pl.BlockSpec((CONV_KERNEL, CONV_WIDTH), lambda b: (0, 0)),
            row(), row(), row(),
        ],
        out_specs=pl.BlockSpec((seq, CONV_WIDTH), lambda b: (b, 0)),
        out_shape=jax.ShapeDtypeStruct((n, CONV_WIDTH), BF16),
        scratch_shapes=[
            pltpu.VMEM((CONV_WIDTH // LANES, CONV_PAD + seq, LANES), F32),
            pltpu.VMEM((CONV_WIDTH // LANES, seq, LANES), F32),
        ],
        compiler_params=_cparams(("arbitrary",), 32),
        name="conformer_conv",
    )(h, h, conv_w, conv_b, ln_g, ln_b)


def _mix_body(x_ref, a_ref, c_ref, wo_ref, g0_ref, b0_ref, g1_ref, b1_ref, rw2_ref, rb_ref,
              x1_ref, xs_ref, pos_ref, gate_ref, cc_ref):
    tb = x_ref.shape[0]
    xn = _layer_norm(x_ref[...], g0_ref[...], b0_ref[...])
    mix = (jnp.dot(a_ref[...], wo_ref[0:ATTN_WIDTH, :], preferred_element_type=F32)
           + jnp.dot(c_ref[...], wo_ref[ATTN_WIDTH:, :], preferred_element_type=F32))
    x1 = _layer_norm(DEEPNORM_ALPHA * xn + mix, g1_ref[...], b1_ref[...])
    x1_ref[...] = x1
    x1b = x1.astype(BF16)

    x_lo = (x1 - x1b.astype(F32)).astype(BF16)
    rw2 = rw2_ref[...]
    nt_dims = (((1,), (1,)), ((), ()))
    both = lax.dot_general(rw2, x1b, nt_dims, preferred_element_type=F32)
    logits = ((lax.dot_general(rw2[:N_EXPERTS], x_lo, nt_dims, preferred_element_type=F32) + both[N_EXPERTS:])
              + both[:N_EXPERTS]) + rb_ref[...]
    eidx = lax.broadcasted_iota(I32, logits.shape, 0)
    work = logits
    vals, hots = [], []
    for _ in range(TOP_K):
        mx = jnp.max(work, axis=0, keepdims=True)
        sel = jnp.min(jnp.where(work == mx, eidx, N_EXPERTS), axis=0, keepdims=True)
        hot = eidx == sel
        vals.append(mx)
        hots.append(hot)
        work = jnp.where(hot, -jnp.inf, work)
    ex = [jnp.exp(v - vals[0]) for v in vals]
    den = ex[0] + ex[1] + ex[2] + ex[3]
    for k in range(TOP_K):
        gate_ref[k:k + 1, :] = ex[k] / den

    member = jnp.zeros(logits.shape, F32)
    for hot in hots:
        member = member + jnp.where(hot, 1.0, 0.0)
    before = (lax.broadcasted_iota(I32, (tb, tb), 0) < lax.broadcasted_iota(I32, (tb, tb), 1))
    rank = jnp.dot(member.astype(BF16), jnp.where(before, 1.0, 0.0).astype(BF16), preferred_element_type=F32)
    count = jnp.sum(member, axis=1, keepdims=True)
    chunks = jnp.floor((count + (BF16_ROWS - 1.0)) * (1.0 / BF16_ROWS))
    cc_ref[...] = jnp.broadcast_to(chunks, cc_ref.shape).astype(I32)
    lower = (lax.broadcasted_iota(I32, (N_EXPERTS, N_EXPERTS), 1) < lax.broadcasted_iota(I32, (N_EXPERTS, N_EXPERTS), 0))
    seg_start = jnp.dot(jnp.where(lower, 1.0, 0.0).astype(BF16),
                        jnp.broadcast_to(chunks, (N_EXPERTS, LANES)).astype(BF16),
                        preferred_element_type=F32)[:, 0:1]
    slot = seg_start * float(BF16_ROWS) + rank
    pos = [jnp.sum(jnp.where(hot, slot, 0.0), axis=0, keepdims=True).astype(I32) for hot in hots]
    for k in range(TOP_K):
        pos_ref[k:k + 1, :] = pos[k]

    rows = 256
    one, nil = jnp.ones((), BF16), jnp.zeros((), BF16)
    pos16 = [p.astype(jnp.int16) for p in pos]
    for rc in range(xs_ref.shape[0] // rows):
        r = lax.broadcasted_iota(jnp.int16, (rows, tb), 0) + jnp.int16(rc * rows)
        hit = (r == pos16[0]) | (r == pos16[1]) | (r == pos16[2]) | (r == pos16[3])
        sel = jnp.where(hit, one, nil)
        xs_ref[rc * rows:(rc + 1) * rows, :] = jnp.dot(sel, x1b, preferred_element_type=F32).astype(BF16)


def _mix(x2, a, c, wo_bf16, g0, b0, g1, b1, rw2, rb):
    n, d = x2.shape
    tb = TOKEN_TILE
    nb = n // tb
    row = lambda: pl.BlockSpec((1, d), lambda i: (0, 0))
    return pl.pallas_call(
        _mix_body,
        grid=(nb,),
        in_specs=[
            pl.BlockSpec((tb, d), lambda i: (i, 0)),
            pl.BlockSpec((tb, ATTN_WIDTH), lambda i: (i, 0)),
            pl.BlockSpec((tb, CONV_WIDTH), lambda i: (i, 0)),
            pl.BlockSpec((d, d), lambda i: (0, 0)),
            row(), row(), row(), row(),
            pl.BlockSpec((2 * N_EXPERTS, d), lambda i: (0, 0)),
            pl.BlockSpec((N_EXPERTS, 1), lambda i: (0, 0)),
        ],
        out_specs=[
            pl.BlockSpec((tb, d), lambda i: (i, 0)),
            pl.BlockSpec((None, LOCAL_ROWS, d), lambda i: (i, 0, 0)),
            pl.BlockSpec((None, TOP_K, tb), lambda i: (i, 0, 0)),
            pl.BlockSpec((None, TOP_K, tb), lambda i: (i, 0, 0)),
            pl.BlockSpec((None, N_EXPERTS, LANES), lambda i: (i, 0, 0)),
        ],
        out_shape=[
            jax.ShapeDtypeStruct((n, d), F32),
            jax.ShapeDtypeStruct((nb, LOCAL_ROWS, d), BF16),
            jax.ShapeDtypeStruct((nb, TOP_K, tb), I32),
            jax.ShapeDtypeStruct((nb, TOP_K, tb), F32),
            jax.ShapeDtypeStruct((nb, N_EXPERTS, LANES), I32),
        ],
        compiler_params=_cparams(("arbitrary",), 48),
        name="mix_router_sort",
    )(x2, a, c, wo_bf16, g0, b0, g1, b1, rw2, rb)


def _max_tiles(nb):
    return (nb * CHUNKS_PER_BLOCK + CHUNKS_PER_TILE - 1) // CHUNKS_PER_TILE + N_EXPERTS + 1


def _build_schedule(cc):
    nb = cc.shape[0]
    cpt = CHUNKS_PER_TILE
    ne1 = N_EXPERTS + 1
    tail = CHUNKS_PER_BLOCK - jnp.sum(cc, axis=1)
    cce = jnp.concatenate([cc, tail[:, None]], axis=1)
    seg_start = jnp.cumsum(cce, axis=1) - cce
    tot = jnp.sum(cce, axis=0)
    tiles_e = (tot + cpt - 1) // cpt
    tile_end = jnp.cumsum(tiles_e)
    max_tiles = _max_tiles(nb)

    def lookup(passed, table):
        return table[0] + jnp.sum(jnp.where(passed, (table[1:] - table[:-1])[None, :], 0), axis=1)

    t = jnp.arange(max_tiles, dtype=I32)
    passed = tile_end[None, :] <= t[:, None]
    te = jnp.minimum(jnp.sum(passed, axis=1), ne1 - 1).astype(I32)
    experts = jnp.arange(ne1, dtype=I32)
    has_tiles = (tiles_e > 0) & (experts < N_EXPERTS)
    later = jnp.where(has_tiles[None, :] & (experts[None, :] > te[:, None]), experts[None, :], ne1)
    nxt = jnp.min(later, axis=1)
    own = jnp.where(te < N_EXPERTS, te, jnp.max(jnp.where(has_tiles, experts, 0)))
    is_first = jnp.any(((tile_end - tiles_e)[None, :] == t[:, None]) & (tiles_e > 0)[None, :], axis=1)
    wsel = jnp.where(is_first | (nxt >= N_EXPERTS), own, nxt).astype(I32)

    pad_len = tiles_e * cpt - tot
    seg_len = jnp.concatenate([cce.T, pad_len[:, None]], axis=1).reshape(-1)
    seg_end = jnp.cumsum(seg_len)
    block = jnp.arange(nb, dtype=I32)[None, :]
    is_real = jnp.concatenate([jnp.ones((ne1, nb), I32), jnp.zeros((ne1, 1), I32)], axis=1).reshape(-1)
    seg_first = (seg_end - seg_len) * is_real
    zero = jnp.zeros((1,), I32)

    base = block * CHUNKS_PER_BLOCK + seg_start.T
    first = jnp.concatenate([jnp.concatenate([base, jnp.zeros((ne1, 1), I32)], axis=1).reshape(-1) - seg_first,
                             zero])
    p = jnp.arange(max_tiles * cpt, dtype=I32)
    passed = seg_end[None, :] <= p[:, None]
    real = lookup(passed, jnp.concatenate([is_real, zero]))
    src = (real * p + lookup(passed, first)).astype(I32)
    dump0 = nb * CHUNKS_PER_BLOCK
    dst = jnp.where(real > 0, src, dump0 + ((p // cpt) % 2) * cpt + p % cpt).astype(I32)
    dummy = dump0 + cpt + jnp.arange(cpt, dtype=I32)
    nv = jnp.sum(real.reshape(max_tiles, cpt), axis=1).astype(I32)
    return te, wsel, nv, src, jnp.concatenate([dummy, dst]), jnp.reshape(tile_end[-1], (1,)).astype(I32)


def _moe_body(te_ref, wsel_ref, nv_ref, src_ref, dst_ref, nt_ref, xs_hbm, wgu_ref, bgu_ref, wd_ref, bd_ref, y_hbm,
              xbuf, ybuf, act, wgu_b, wd_b, in_sem, out_sem):
    t = pl.program_id(0)
    last = nt_ref[0] - 1
    cpt = CHUNKS_PER_TILE
    f = wd_ref.shape[0]
    slot = t % 2
    other = 1 - slot
    e = te_ref[t]

    def rows(c):
        return pl.ds(c * BF16_ROWS, BF16_ROWS)

    def in_copy(tile, buf, c):
        return pltpu.make_async_copy(xs_hbm.at[src_ref[tile * cpt + c]], xbuf.at[buf, rows(c)], in_sem.at[buf])

    def out_copy(tile, buf, c):
        return pltpu.make_async_copy(ybuf.at[buf, rows(c)], y_hbm.at[dst_ref[(tile + 1) * cpt + c]],
                                     out_sem.at[buf])

    def start_all(copy, tile, buf):
        for c in range(cpt):
            copy(tile, buf, c).start()

    def wait_all(copy, tile, buf):
        for c in range(cpt):
            copy(tile, buf, c).wait()

    @pl.when(t == 0)
    def _():
        start_all(in_copy, 0, 0)
        ybuf[...] = jnp.zeros(ybuf.shape, BF16)
        even_dump = [pltpu.make_async_copy(ybuf.at[1, rows(c)], y_hbm.at[y_hbm.shape[0] - 2 * cpt + c],
                                           out_sem.at[0]) for c in range(cpt)]
        for copy in even_dump:
            copy.start()
        for copy in even_dump:
            copy.wait()

    @pl.when(t <= last)
    def _():
        @pl.when(t >= 1)
        def _():
            wait_all(out_copy, t - 2, slot)

        wait_all(in_copy, t, slot)
        start_all(in_copy, jnp.minimum(t + 1, last), other)
        start_all(out_copy, t - 1, other)

        @pl.when((e < N_EXPERTS) & ((t == 0) | (te_ref[jnp.maximum(t - 1, 0)] != e)))
        def _():
            wgu_b[...] = wgu_ref[...].astype(BF16)
            wd_b[...] = wd_ref[...].astype(BF16)

        def expert_mlp(m):
            x = xbuf[slot, 0:m, :]
            nc = 512
            for j in range(f // nc):
                gsl = slice(j * nc, (j + 1) * nc)
                usl = slice(f + j * nc, f + (j + 1) * nc)
                g = jnp.dot(x, wgu_b[:, gsl], preferred_element_type=F32) + bgu_ref[:, gsl]
                u = jnp.dot(x, wgu_b[:, usl], preferred_element_type=F32) + bgu_ref[:, usl]
                g = jnp.minimum(g, SWIGLU_LIMIT)
                u = jnp.clip(u, -SWIGLU_LIMIT, SWIGLU_LIMIT)
                act[0:m, gsl] = (g * _sigmoid(SWIGLU_ALPHA * g) * (u + 1.0)).astype(BF16)
            a = act[0:m, :]
            for j in range(wd_ref.shape[1] // nc):
                sl = slice(j * nc, (j + 1) * nc)
                y = jnp.dot(a, wd_b[:, sl], preferred_element_type=F32) + bd_ref[:, sl]
                ybuf[slot, 0:m, sl] = y.astype(BF16)

        quanta = (nv_ref[t] * BF16_ROWS + (MOE_ROW_QUANTUM - 1)) // MOE_ROW_QUANTUM
        for k in range(1, MOE_TILE // MOE_ROW_QUANTUM + 1):
            pl.when((e < N_EXPERTS) & (quanta == k))(functools.partial(expert_mlp, k * MOE_ROW_QUANTUM))

        @pl.when(e >= N_EXPERTS)
        def _():
            ybuf[slot] = jnp.zeros(ybuf.shape[1:], BF16)

        @pl.when(t == last)
        def _():
            start_all(out_copy, t, slot)
            wait_all(out_copy, t - 1, other)
            wait_all(out_copy, t, slot)
            wait_all(in_copy, t, other)


def _moe(xs_chunks, te, wsel, nv, src, dst, nt, w_gate_up, b_gate_up, w_down, b_down):
    nch, _, d = xs_chunks.shape
    f = w_down.shape[1]
    max_tiles = te.shape[0]
    by_wsel = lambda t, te_r, wsel_r, nv_r, src_r, dst_r, nt_r: (wsel_r[t], 0, 0)
    by_tile = lambda t, te_r, wsel_r, nv_r, src_r, dst_r, nt_r: (jnp.minimum(te_r[t], N_EXPERTS - 1), 0, 0)
    grid_spec = pltpu.PrefetchScalarGridSpec(
        num_scalar_prefetch=6,
        grid=(max_tiles,),
        in_specs=[
            pl.BlockSpec(memory_space=pl.ANY),
            pl.BlockSpec((None, d, 2 * f), by_wsel),
            pl.BlockSpec((None, 1, 2 * f), by_tile),
            pl.BlockSpec((None, f, d), by_wsel),
            pl.BlockSpec((None, 1, d), by_tile),
        ],
        out_specs=pl.BlockSpec(memory_space=pl.ANY),
        scratch_shapes=[
            pltpu.VMEM((2, MOE_TILE, d), BF16),
            pltpu.VMEM((2, MOE_TILE, d), BF16),
            pltpu.VMEM((MOE_TILE, f), BF16),
            pltpu.VMEM((d, 2 * f), BF16),
            pltpu.VMEM((f, d), BF16),
            pltpu.SemaphoreType.DMA((2,)),
            pltpu.SemaphoreType.DMA((2,)),
        ],
    )
    return pl.pallas_call(
        _moe_body,
        grid_spec=grid_spec,
        out_shape=jax.ShapeDtypeStruct((nch + 2 * CHUNKS_PER_TILE, BF16_ROWS, d), BF16),
        compiler_params=_cparams(("arbitrary",), 56),
        name="moe_grouped",
    )(te, wsel, nv, src, dst, nt, xs_chunks, w_gate_up, b_gate_up, w_down, b_down)


def _combine_body(y_ref, pos_ref, gate_ref, x1_ref, g_ref, b_ref, o_ref):
    tb = TOKEN_TILE
    cols = 512
    for blk in range(COMBINE_BLOCKS):
        pos = pos_ref[blk]
        gate = gate_ref[blk]
        y0 = blk * LOCAL_ROWS
        acc = jnp.zeros((tb, x1_ref.shape[1]), F32)
        for rc in range(LOCAL_ROWS // cols):
            r = lax.broadcasted_iota(I32, (tb, cols), 1) + rc * cols
            w = jnp.zeros((tb, cols), F32)
            for k in range(TOP_K):
                w = w + jnp.where(r == pos[:, k:k + 1], gate[:, k:k + 1], 0.0)
            acc = acc + jnp.dot(w.astype(BF16), y_ref[y0 + rc * cols:y0 + (rc + 1) * cols, :],
                                preferred_element_type=F32)
        rows = slice(blk * tb, (blk + 1) * tb)
        o_ref[rows, :] = _layer_norm(DEEPNORM_ALPHA * x1_ref[rows, :] + acc, g_ref[...], b_ref[...])


def _combine(y_local, pos, gates, x1, g, b):
    n, d = x1.shape
    tb = TOKEN_TILE
    nb = n // tb
    per = COMBINE_BLOCKS
    assert nb % per == 0
    row = lambda: pl.BlockSpec((1, d), lambda i: (0, 0))
    return pl.pallas_call(
        _combine_body,
        grid=(nb // per,),
        in_specs=[
            pl.BlockSpec((per * LOCAL_ROWS, d), lambda i: (i, 0)),
            pl.BlockSpec((per, tb, TOP_K), lambda i: (i, 0, 0)),
            pl.BlockSpec((per, tb, TOP_K), lambda i: (i, 0, 0)),
            pl.BlockSpec((per * tb, d), lambda i: (i, 0)),
            row(), row(),
        ],
        out_specs=pl.BlockSpec((per * tb, d), lambda i: (i, 0)),
        out_shape=jax.ShapeDtypeStruct((n, d), F32),
        compiler_params=_cparams(("arbitrary",), 48),
        name="combine_ln",
    )(y_local, pos, gates, x1, g, b)


def kernel(x, emb_ln_g, emb_ln_b, w_in, lambda_q1, lambda_k1, lambda_q2, lambda_k2, subln_g, conv_w, conv_b,
           conv_ln_g, conv_ln_b, w_out, ln1_g, ln1_b, router_w, router_b, w_gate_up, b_gate_up, w_down, b_down,
           ln2_g, ln2_b):
    batch, seq, d = x.shape
    n = batch * seq
    assert n % TOKEN_TILE == 0 and seq % ATTN_KV_BLOCK == 0 and seq % CONV_TILE == 0
    assert w_in.shape[0] == DEPTH
    x2 = x.reshape(n, d)
    row = lambda v: v.reshape(1, -1).astype(F32)
    slopes = jnp.array([(2.0 ** (-8.0 / N_HEADS)) ** (i + 1) for i in range(N_HEADS)], dtype=F32)

    h = _inproj(x2, row(emb_ln_g), row(emb_ln_b), w_in[0].astype(BF16))
    a = _attention(h, slopes, row(lambda_q1[0]), row(lambda_k1[0]), row(lambda_q2[0]), row(lambda_k2[0]),
                   row(subln_g[0]), batch=batch, seq=seq)
    c = _conv_module(h, conv_w[0], row(conv_b[0]), row(conv_ln_g[0]), row(conv_ln_b[0]), batch=batch, seq=seq)
    rw_t = router_w[0].T
    rw_hi = rw_t.astype(BF16)
    rw_lo = (rw_t - rw_hi.astype(F32)).astype(BF16)
    x1, xs_local, pos_t, gates_t, cc = _mix(
        x2, a, c, w_out[0].astype(BF16), row(emb_ln_g), row(emb_ln_b), row(ln1_g[0]), row(ln1_b[0]),
        jnp.concatenate([rw_hi, rw_lo], axis=0), router_b[0].reshape(N_EXPERTS, 1))

    te, wsel, nv, src, dst, nt = _build_schedule(cc[:, :, 0])
    nb = n // TOKEN_TILE
    y_chunks = _moe(xs_local.reshape(nb * CHUNKS_PER_BLOCK, BF16_ROWS, d), te, wsel, nv, src, dst, nt,
                    w_gate_up[0], b_gate_up[0].reshape(N_EXPERTS, 1, -1), w_down[0],
                    b_down[0].reshape(N_EXPERTS, 1, -1))
    y_local = y_chunks.reshape(-1, d)
    out = _combine(y_local, jnp.transpose(pos_t, (0, 2, 1)), jnp.transpose(gates_t, (0, 2, 1)), x1,
                   row(ln2_g[0]), row(ln2_b[0]))
    return out.reshape(batch, seq, d)
```

```python
import functools

import jax
import jax.numpy as jnp
from jax import lax
from jax.experimental import pallas as pl
from jax.experimental.pallas import tpu as pltpu

F32 = jnp.float32
BF16 = jnp.bfloat16
I32 = jnp.int32

N_HEADS = 4
HEAD_DIM = 64
V_DIM = 128
ATTN_WIDTH = N_HEADS * V_DIM
CONV_WIDTH = 512
CONV_KERNEL = 31
N_EXPERTS = 32
TOP_K = 4
SWIGLU_LIMIT = 7.0
SWIGLU_ALPHA = 1.702
LN_EPS = 1e-5
DEPTH = 1
DEEPNORM_ALPHA = (2.0 * DEPTH) ** 0.25
LAMBDA_INIT = 0.2
LOG2_E = 1.4426950408889634
Q_COLS = N_HEADS * 2 * HEAD_DIM
Q_SCALE = HEAD_DIM ** -0.5 * LOG2_E

LANES = 128
BF16_ROWS = 16

TOKEN_TILE = 512
INPROJ_TILE = 1024
ATTN_KV_BLOCK = 256
ATTN_SUB = 256
CONV_TILE = 128
CONV_ROW_STRIDE = 4
CONV_GLU_TILE = 128
CONV_PAD = 32
MOE_TILE = 1024
MOE_ROW_QUANTUM = 256
CHUNKS_PER_TILE = MOE_TILE // BF16_ROWS
LOCAL_ROWS = TOP_K * TOKEN_TILE + N_EXPERTS * BF16_ROWS
CHUNKS_PER_BLOCK = LOCAL_ROWS // BF16_ROWS
COMBINE_BLOCKS = 2


def _cparams(semantics, vmem_mib):
    return pltpu.CompilerParams(dimension_semantics=semantics, vmem_limit_bytes=vmem_mib * 1024 * 1024)


def _layer_norm(x, g, b):
    mu = jnp.mean(x, axis=-1, keepdims=True)
    xc = x - mu
    var = jnp.mean(xc * xc, axis=-1, keepdims=True)
    return xc * lax.rsqrt(var + LN_EPS) * g + b


def _sigmoid(x):
    return 1.0 / (1.0 + jnp.exp(-x))


def _inproj_body(x_ref, g_ref, b_ref, w_ref, h_ref, *, n_chunk):
    half = x_ref.shape[0] // 2
    for r0 in (0, half):
        xn = _layer_norm(x_ref[r0:r0 + half, :], g_ref[...], b_ref[...]).astype(BF16)
        for j in range(w_ref.shape[1] // n_chunk):
            sl = slice(j * n_chunk, (j + 1) * n_chunk)
            hj = jnp.dot(xn, w_ref[:, sl], preferred_element_type=F32)
            if (j + 1) * n_chunk <= Q_COLS:
                hj = hj * Q_SCALE
            h_ref[r0:r0 + half, sl] = hj.astype(BF16)


def _inproj(x2, g, b, w_bf16):
    n, d = x2.shape
    cols = w_bf16.shape[1]
    tm = INPROJ_TILE
    return pl.pallas_call(
        functools.partial(_inproj_body, n_chunk=512),
        grid=(n // tm,),
        in_specs=[
            pl.BlockSpec((tm, d), lambda i: (i, 0)),
            pl.BlockSpec((1, d), lambda i: (0, 0)),
            pl.BlockSpec((1, d), lambda i: (0, 0)),
            pl.BlockSpec((d, cols), lambda i: (0, 0)),
        ],
        out_specs=pl.BlockSpec((tm, cols), lambda i: (i, 0)),
        out_shape=jax.ShapeDtypeStruct((n, cols), BF16),
        compiler_params=_cparams(("arbitrary",), 40),
        name="inproj",
    )(x2, g, b, w_bf16)


def _attn_body(slopes_ref, lq1_ref, lk1_ref, lq2_ref, lk2_ref, g_ref, q_ref, k_ref, v_ref,
               ca_ref, cg_ref, cw_ref, cb_ref, o_ref, cpre_ref,
               q2_ref, m_ref, l_ref, acc_ref, glu_ref, *, seq):
    head = pl.program_id(1)
    slope = slopes_ref[head] * LOG2_E
    lam =(jnp.exp(jnp.sum(lq1_ref[...] * lk1_ref[...], axis=-1, keepdims=True))
           - jnp.exp(jnp.sum(lq2_ref[...] * lk2_ref[...], axis=-1, keepdims=True))
           + LAMBDA_INIT)

    q = q_ref[...]
    lane = lax.broadcasted_iota(I32, q.shape, 1)
    zero = jnp.zeros_like(q)
    q2_ref[0:seq, :] = jnp.where(lane < HEAD_DIM, q, zero)
    q2_ref[seq:, :] = jnp.where(lane >= HEAD_DIM, q, zero)
    m_ref[...] = jnp.full(m_ref.shape, -jnp.inf, F32)
    l_ref[...] = jnp.zeros(l_ref.shape, F32)
    acc_ref[...] = jnp.zeros(acc_ref.shape, F32)
    kb_rows = ATTN_KV_BLOCK
    sub = ATTN_SUB
    ones = jnp.ones((kb_rows, V_DIM), BF16)
    col = lax.broadcasted_iota(I32, (1, kb_rows), 1)

    glu_ref[0:CONV_PAD, :] = jnp.zeros((CONV_PAD, LANES), F32)
    for st in range(0, seq, CONV_GLU_TILE):
        glu_ref[CONV_PAD + st:CONV_PAD + st + CONV_GLU_TILE, :] = (
            ca_ref[st:st + CONV_GLU_TILE, :].astype(F32) * _sigmoid(cg_ref[st:st + CONV_GLU_TILE, :].astype(F32)))

    def conv_tile(st):
        taps = [jnp.broadcast_to(cw_ref[j:j + 1, :], (8, LANES)) for j in range(CONV_KERNEL)]
        bias = jnp.broadcast_to(cb_ref[...], (8, LANES))
        for t0 in range(st, st + CONV_TILE, 8 * CONV_ROW_STRIDE):
            for p in range(CONV_ROW_STRIDE):
                acc = bias
                for j in range(CONV_KERNEL):
                    src = pl.ds(t0 + p + (CONV_PAD - (CONV_KERNEL - 1) + j), 8, stride=CONV_ROW_STRIDE)
                    acc = acc + glu_ref[src, :] * taps[j]
                cpre_ref[pl.ds(t0 + p, 8, stride=CONV_ROW_STRIDE), :] = acc

    conv_tiles = list(range(0, seq, CONV_TILE))
    n_kv = seq // kb_rows

    for d in range(n_kv):
        for st in conv_tiles[d * len(conv_tiles) // n_kv:(d + 1) * len(conv_tiles) // n_kv]:
            conv_tile(st)
        k0 = d * kb_rows
        kb = k_ref[k0:k0 + kb_rows, :]
        vext = jnp.concatenate([v_ref[k0:k0 + kb_rows, :], ones], axis=1)
        for half in range(2):
            for i in range(k0 // sub, seq // sub):
                r0 = i * sub
                rows = slice(half * seq + r0, half * seq + r0 + sub)
                ncol = min(kb_rows, -(-(r0 + sub - k0) // LANES) * LANES)
                bias = (col[:, 0:ncol] + (k0 - r0)).astype(F32) * slope
                s = lax.dot_general(q2_ref[rows, :], kb[0:ncol, :], (((1,), (1,)), ((), ())),
                                    preferred_element_type=F32) + bias
                if k0 + ncol - 1 > r0:
                    row = lax.broadcasted_iota(I32, s.shape, 0) + (r0 - k0)
                    s = jnp.where(lax.broadcasted_iota(I32, s.shape, 1) <= row, s, -jnp.inf)
                m_prev = m_ref[rows, :]
                m_new = jnp.maximum(m_prev, jnp.max(s, axis=1, keepdims=True))
                alpha = jnp.exp2(m_prev - m_new)
                p = jnp.exp2(s - jnp.concatenate([m_new] * (ncol // LANES), axis=1))
                pv = jnp.dot(p.astype(BF16), vext[0:ncol, :], preferred_element_type=F32)
                acc_ref[rows, :] = alpha * acc_ref[rows, :] + pv[:, :V_DIM]
                l_ref[rows, :] = alpha * l_ref[rows, :] + pv[:, V_DIM:]
                m_ref[rows, :] = m_new

    o = acc_ref[...] / l_ref[...]
    od = o[:seq] - lam * o[seq:]
    ms = jnp.mean(od * od, axis=-1, keepdims=True)
    out = od * lax.rsqrt(ms + LN_EPS) * g_ref[...] * (1.0 - LAMBDA_INIT)
    o_ref[...] = out.astype(BF16)


def _attention_and_conv(h, slopes, lq1, lk1, lq2, lk2, subln_g, conv_w, conv_b, *, batch, seq):
    n = h.shape[0]
    assert CONV_WIDTH // LANES == N_HEADS and seq % CONV_TILE == 0 and seq % CONV_GLU_TILE == 0
    vec = lambda: pl.BlockSpec((1, HEAD_DIM), lambda b, hd, *_: (0, 0))
    slab0 = 3 * ATTN_WIDTH // LANES
    grid_spec = pltpu.PrefetchScalarGridSpec(
        num_scalar_prefetch=1,
        grid=(batch, N_HEADS),
        in_specs=[
            vec(), vec(), vec(), vec(),
            pl.BlockSpec((1, V_DIM), lambda b, hd, *_: (0, 0)),
            pl.BlockSpec((seq, V_DIM), lambda b, hd, *_: (b, hd)),
            pl.BlockSpec((seq, V_DIM), lambda b, hd, *_: (b, N_HEADS + hd)),
            pl.BlockSpec((seq, V_DIM), lambda b, hd, *_: (b, 2 * N_HEADS + hd)),
            pl.BlockSpec((seq, LANES), lambda b, hd, *_: (b, slab0 + hd)),
            pl.BlockSpec((seq, LANES), lambda b, hd, *_: (b, slab0 + N_HEADS + hd)),
            pl.BlockSpec((CONV_KERNEL, LANES), lambda b, hd, *_: (0, hd)),
            pl.BlockSpec((1, LANES), lambda b, hd, *_: (0, hd)),
        ],
        out_specs=[
            pl.BlockSpec((seq, V_DIM), lambda b, hd, *_: (b, hd)),
            pl.BlockSpec((seq, LANES), lambda b, hd, *_: (b, hd)),
        ],
        scratch_shapes=[
            pltpu.VMEM((2 * seq, V_DIM), BF16),
            pltpu.VMEM((2 * seq, LANES), F32),
            pltpu.VMEM((2 * seq, LANES), F32),
            pltpu.VMEM((2 * seq, V_DIM), F32),
            pltpu.VMEM((CONV_PAD + seq, LANES), F32),
        ],
    )
    return pl.pallas_call(
        functools.partial(_attn_body, seq=seq),
        grid_spec=grid_spec,
        out_shape=[jax.ShapeDtypeStruct((n, ATTN_WIDTH), BF16), jax.ShapeDtypeStruct((n, CONV_WIDTH), F32)],
        compiler_params=_cparams(("arbitrary", "arbitrary"), 32),
        name="diff_attn_conv",
    )(slopes, lq1, lk1, lq2, lk2, subln_g, h, h, h, h, h, conv_w, conv_b)


def _mix_body(x_ref, a_ref, c_ref, wo_ref, g0_ref, b0_ref, g1_ref, b1_ref, cg_ref, cb_ref, rw2_ref, rb_ref,
              x1_ref, xs_ref, pos_ref, gate_ref, cc_ref):
    tb = x_ref.shape[0]
    xn = _layer_norm(x_ref[...], g0_ref[...], b0_ref[...])
    conv = _layer_norm(c_ref[...], cg_ref[...], cb_ref[...])
    conv = (conv * _sigmoid(conv)).astype(BF16)
    mix = (jnp.dot(a_ref[...], wo_ref[0:ATTN_WIDTH, :], preferred_element_type=F32)
           + jnp.dot(conv, wo_ref[ATTN_WIDTH:, :], preferred_element_type=F32))
    x1 = _layer_norm(DEEPNORM_ALPHA * xn + mix, g1_ref[...], b1_ref[...])
    x1_ref[...] = x1
    x1b = x1.astype(BF16)

    x_lo = (x1 - x1b.astype(F32)).astype(BF16)
    rw2 = rw2_ref[...]
    nt_dims = (((1,), (1,)), ((), ()))
    both = lax.dot_general(rw2, x1b, nt_dims, preferred_element_type=F32)
    logits = ((lax.dot_general(rw2[:N_EXPERTS], x_lo, nt_dims, preferred_element_type=F32) + both[N_EXPERTS:])
              + both[:N_EXPERTS]) + rb_ref[...]
    eidx = lax.broadcasted_iota(I32, logits.shape, 0)
    work = logits
    vals, hots = [], []
    for _ in range(TOP_K):
        mx = jnp.max(work, axis=0, keepdims=True)
        sel = jnp.min(jnp.where(work == mx, eidx, N_EXPERTS), axis=0, keepdims=True)
        hot = eidx == sel
        vals.append(mx)
        hots.append(hot)
        work = jnp.where(hot, -jnp.inf, work)
    ex = [jnp.exp(v - vals[0]) for v in vals]
    den = ex[0] + ex[1] + ex[2] + ex[3]
    for k in range(TOP_K):
        gate_ref[k:k + 1, :] = ex[k] / den

    member = jnp.zeros(logits.shape, F32)
    for hot in hots:
        member = member + jnp.where(hot, 1.0, 0.0)
    before = (lax.broadcasted_iota(I32, (tb, tb), 0) < lax.broadcasted_iota(I32, (tb, tb), 1))
    rank = jnp.dot(member.astype(BF16), jnp.where(before, 1.0, 0.0).astype(BF16), preferred_element_type=F32)
    count = jnp.sum(member, axis=1, keepdims=True)
    chunks = jnp.floor((count + (BF16_ROWS - 1.0)) * (1.0 / BF16_ROWS))
    cc_ref[...] = jnp.broadcast_to(chunks, cc_ref.shape).astype(I32)
    lower = (lax.broadcasted_iota(I32, (N_EXPERTS, N_EXPERTS), 1) < lax.broadcasted_iota(I32, (N_EXPERTS, N_EXPERTS), 0))
    seg_start = jnp.dot(jnp.where(lower, 1.0, 0.0).astype(BF16),
                        jnp.broadcast_to(chunks, (N_EXPERTS, LANES)).astype(BF16),
                        preferred_element_type=F32)[:, 0:1]
    slot = seg_start * float(BF16_ROWS) + rank
    pos = [jnp.sum(jnp.where(hot, slot, 0.0), axis=0, keepdims=True).astype(I32) for hot in hots]
    for k in range(TOP_K):
        pos_ref[k:k + 1, :] = pos[k]

    rows = 256
    one, nil = jnp.ones((), BF16), jnp.zeros((), BF16)
    pos16 = [p.astype(jnp.int16) for p in pos]
    for rc in range(xs_ref.shape[0] // rows):
        r = lax.broadcasted_iota(jnp.int16, (rows, tb), 0) + jnp.int16(rc * rows)
        hit = (r == pos16[0]) | (r == pos16[1]) | (r == pos16[2]) | (r == pos16[3])
        sel = jnp.where(hit, one, nil)
        xs_ref[rc * rows:(rc + 1) * rows, :] = jnp.dot(sel, x1b, preferred_element_type=F32).astype(BF16)


def _mix(x2, a, c, wo_bf16, g0, b0, g1, b1, cg, cb, rw2, rb):
    n, d = x2.shape
    tb = TOKEN_TILE
    nb = n // tb
    row = lambda: pl.BlockSpec((1, d), lambda i: (0, 0))
    return pl.pallas_call(
        _mix_body,
        grid=(nb,),
        in_specs=[
            pl.BlockSpec((tb, d), lambda i: (i, 0)),
            pl.BlockSpec((tb, ATTN_WIDTH), lambda i: (i, 0)),
            pl.BlockSpec((tb, CONV_WIDTH), lambda i: (i, 0)),
            pl.BlockSpec((d, d), lambda i: (0, 0)),
            row(), row(), row(), row(),
            pl.BlockSpec((1, CONV_WIDTH), lambda i: (0, 0)),
            pl.BlockSpec((1, CONV_WIDTH), lambda i: (0, 0)),
            pl.BlockSpec((2 * N_EXPERTS, d), lambda i: (0, 0)),
            pl.BlockSpec((N_EXPERTS, 1), lambda i: (0, 0)),
        ],
        out_specs=[
            pl.BlockSpec((tb, d), lambda i: (i, 0)),
            pl.BlockSpec((None, LOCAL_ROWS, d), lambda i: (i, 0, 0)),
            pl.BlockSpec((None, TOP_K, tb), lambda i: (i, 0, 0)),
            pl.BlockSpec((None, TOP_K, tb), lambda i: (i, 0, 0)),
            pl.BlockSpec((None, N_EXPERTS, LANES), lambda i: (i, 0, 0)),
        ],
        out_shape=[
            jax.ShapeDtypeStruct((n, d), F32),
            jax.ShapeDtypeStruct((nb, LOCAL_ROWS, d), BF16),
            jax.ShapeDtypeStruct((nb, TOP_K, tb), I32),
            jax.ShapeDtypeStruct((nb, TOP_K, tb), F32),
            jax.ShapeDtypeStruct((nb, N_EXPERTS, LANES), I32),
        ],
        compiler_params=_cparams(("arbitrary",), 48),
        name="mix_router_sort",
    )(x2, a, c, wo_bf16, g0, b0, g1, b1, cg, cb, rw2, rb)


def _max_tiles(nb):
    return (nb * CHUNKS_PER_BLOCK + CHUNKS_PER_TILE - 1) // CHUNKS_PER_TILE + N_EXPERTS + 1


def _build_schedule(cc):
    nb = cc.shape[0]
    cpt = CHUNKS_PER_TILE
    ne1 = N_EXPERTS + 1
    tail = CHUNKS_PER_BLOCK - jnp.sum(cc, axis=1)
    cce = jnp.concatenate([cc, tail[:, None]], axis=1)
    seg_start = jnp.cumsum(cce, axis=1) - cce
    tot = jnp.sum(cce, axis=0)
    tiles_e = (tot + cpt - 1) // cpt
    tile_end = jnp.cumsum(tiles_e)
    max_tiles = _max_tiles(nb)

    def lookup(passed, table):
        return table[0] + jnp.sum(jnp.where(passed, (table[1:] - table[:-1])[None, :], 0), axis=1)

    t = jnp.arange(max_tiles, dtype=I32)
    passed = tile_end[None, :] <= t[:, None]
    te = jnp.minimum(jnp.sum(passed, axis=1), ne1 - 1).astype(I32)
    experts = jnp.arange(ne1, dtype=I32)
    has_tiles = (tiles_e > 0) & (experts < N_EXPERTS)
    later = jnp.where(has_tiles[None, :] & (experts[None, :] > te[:, None]), experts[None, :], ne1)
    nxt = jnp.min(later, axis=1)
    own = jnp.where(te < N_EXPERTS, te, jnp.max(jnp.where(has_tiles, experts, 0)))
    is_first = jnp.any(((tile_end - tiles_e)[None, :] == t[:, None]) & (tiles_e > 0)[None, :], axis=1)
    wsel = jnp.where(is_first | (nxt >= N_EXPERTS), own, nxt).astype(I32)

    pad_len = tiles_e * cpt - tot
    seg_len = jnp.concatenate([cce.T, pad_len[:, None]], axis=1).reshape(-1)
    seg_end = jnp.cumsum(seg_len)
    block = jnp.arange(nb, dtype=I32)[None, :]
    is_real = jnp.concatenate([jnp.ones((ne1, nb), I32), jnp.zeros((ne1, 1), I32)], axis=1).reshape(-1)
    seg_first = (seg_end - seg_len) * is_real
    zero = jnp.zeros((1,), I32)

    base = block * CHUNKS_PER_BLOCK + seg_start.T
    first = jnp.concatenate([jnp.concatenate([base, jnp.zeros((ne1, 1), I32)], axis=1).reshape(-1) - seg_first,
                             zero])
    p = jnp.arange(max_tiles * cpt, dtype=I32)
    passed = seg_end[None, :] <= p[:, None]
    real = lookup(passed, jnp.concatenate([is_real, zero]))
    src = (real * p + lookup(passed, first)).astype(I32)
    dump0 = nb * CHUNKS_PER_BLOCK
    dst = jnp.where(real > 0, src, dump0 + ((p // cpt) % 2) * cpt + p % cpt).astype(I32)
    dummy = dump0 + cpt + jnp.arange(cpt, dtype=I32)
    nv = jnp.sum(real.reshape(max_tiles, cpt), axis=1).astype(I32)
    return te, wsel, nv, src, jnp.concatenate([dummy, dst]), jnp.reshape(tile_end[-1], (1,)).astype(I32)


def _moe_body(te_ref, wsel_ref, nv_ref, src_ref, dst_ref, nt_ref, xs_hbm, wgu_ref, bgu_ref, wd_ref, bd_ref, y_hbm,
              xbuf, ybuf, act, wgu_b, wd_b, in_sem, out_sem):
    t = pl.program_id(0)
    last = nt_ref[0] - 1
    cpt = CHUNKS_PER_TILE
    f = wd_ref.shape[0]
    slot = t % 2
    other = 1 - slot
    e = te_ref[t]

    def rows(c):
        return pl.ds(c * BF16_ROWS, BF16_ROWS)

    def in_copy(tile, buf, c):
        return pltpu.make_async_copy(xs_hbm.at[src_ref[tile * cpt + c]], xbuf.at[buf, rows(c)], in_sem.at[buf])

    def out_copy(tile, buf, c):
        return pltpu.make_async_copy(ybuf.at[buf, rows(c)], y_hbm.at[dst_ref[(tile + 1) * cpt + c]],
                                     out_sem.at[buf])

    def start_all(copy, tile, buf):
        for c in range(cpt):
            copy(tile, buf, c).start()

    def wait_all(copy, tile, buf):
        for c in range(cpt):
            copy(tile, buf, c).wait()

    @pl.when(t == 0)
    def _():
        start_all(in_copy, 0, 0)
        ybuf[...] = jnp.zeros(ybuf.shape, BF16)
        even_dump = [pltpu.make_async_copy(ybuf.at[1, rows(c)], y_hbm.at[y_hbm.shape[0] - 2 * cpt + c],
                                           out_sem.at[0]) for c in range(cpt)]
        for copy in even_dump:
            copy.start()
        for copy in even_dump:
            copy.wait()

    @pl.when(t <= last)
    def _():
        @pl.when(t >= 1)
        def _():
            wait_all(out_copy, t - 2, slot)

        wait_all(in_copy, t, slot)
        start_all(in_copy, jnp.minimum(t + 1, last), other)
        start_all(out_copy, t - 1, other)

        @pl.when((e < N_EXPERTS) & ((t == 0) | (te_ref[jnp.maximum(t - 1, 0)] != e)))
        def _():
            wgu_b[...] = wgu_ref[...].astype(BF16)
            wd_b[...] = wd_ref[...].astype(BF16)

        def expert_mlp(m):
            x = xbuf[slot, 0:m, :]
            nc = 512
            for j in range(f // nc):
                gsl = slice(j * nc, (j + 1) * nc)
                usl = slice(f + j * nc, f + (j + 1) * nc)
                g = jnp.dot(x, wgu_b[:, gsl], preferred_element_type=F32) + bgu_ref[:, gsl]
                u = jnp.dot(x, wgu_b[:, usl], preferred_element_type=F32) + bgu_ref[:, usl]
                g = jnp.minimum(g, SWIGLU_LIMIT)
                u = jnp.clip(u, -SWIGLU_LIMIT, SWIGLU_LIMIT)
                act[0:m, gsl] = (g * _sigmoid(SWIGLU_ALPHA * g) * (u + 1.0)).astype(BF16)
            a = act[0:m, :]
            for j in range(wd_ref.shape[1] // nc):
                sl = slice(j * nc, (j + 1) * nc)
                y = jnp.dot(a, wd_b[:, sl], preferred_element_type=F32) + bd_ref[:, sl]
                ybuf[slot, 0:m, sl] = y.astype(BF16)

        quanta = (nv_ref[t] * BF16_ROWS + (MOE_ROW_QUANTUM - 1)) // MOE_ROW_QUANTUM
        for k in range(1, MOE_TILE // MOE_ROW_QUANTUM + 1):
            pl.when((e < N_EXPERTS) & (quanta == k))(functools.partial(expert_mlp, k * MOE_ROW_QUANTUM))

        @pl.when(e >= N_EXPERTS)
        def _():
            ybuf[slot] = jnp.zeros(ybuf.shape[1:], BF16)

        @pl.when(t == last)
        def _():
            start_all(out_copy, t, slot)
            wait_all(out_copy, t - 1, other)
            wait_all(out_copy, t, slot)
            wait_all(in_copy, t, other)


def _moe(xs_chunks, te, wsel, nv, src, dst, nt, w_gate_up, b_gate_up, w_down, b_down):
    nch, _, d = xs_chunks.shape
    f = w_down.shape[1]
    max_tiles = te.shape[0]
    by_wsel = lambda t, te_r, wsel_r, nv_r, src_r, dst_r, nt_r: (wsel_r[t], 0, 0)
    by_tile = lambda t, te_r, wsel_r, nv_r, src_r, dst_r, nt_r: (jnp.minimum(te_r[t], N_EXPERTS - 1), 0, 0)
    grid_spec = pltpu.PrefetchScalarGridSpec(
        num_scalar_prefetch=6,
        grid=(max_tiles,),
        in_specs=[
            pl.BlockSpec(memory_space=pl.ANY),
            pl.BlockSpec((None, d, 2 * f), by_wsel),
            pl.BlockSpec((None, 1, 2 * f), by_tile),
            pl.BlockSpec((None, f, d), by_wsel),
            pl.BlockSpec((None, 1, d), by_tile),
        ],
        out_specs=pl.BlockSpec(memory_space=pl.ANY),
        scratch_shapes=[
            pltpu.VMEM((2, MOE_TILE, d), BF16),
            pltpu.VMEM((2, MOE_TILE, d), BF16),
            pltpu.VMEM((MOE_TILE, f), BF16),
            pltpu.VMEM((d, 2 * f), BF16),
            pltpu.VMEM((f, d), BF16),
            pltpu.SemaphoreType.DMA((2,)),
            pltpu.SemaphoreType.DMA((2,)),
        ],
    )
    return pl.pallas_call(
        _moe_body,
        grid_spec=grid_spec,
        out_shape=jax.ShapeDtypeStruct((nch + 2 * CHUNKS_PER_TILE, BF16_ROWS, d), BF16),
        compiler_params=_cparams(("arbitrary",), 56),
        name="moe_grouped",
    )(te, wsel, nv, src, dst, nt, xs_chunks, w_gate_up, b_gate_up, w_down, b_down)


def _combine_body(y_ref, pos_ref, gate_ref, x1_ref, g_ref, b_ref, o_ref):
    tb = TOKEN_TILE
    cols = 512
    for blk in range(COMBINE_BLOCKS):
        pos = pos_ref[blk]
        gate = gate_ref[blk]
        y0 = blk * LOCAL_ROWS
        acc = jnp.zeros((tb, x1_ref.shape[1]), F32)
        for rc in range(LOCAL_ROWS // cols):
            r = lax.broadcasted_iota(I32, (tb, cols), 1) + rc * cols
            w = jnp.zeros((tb, cols), F32)
            for k in range(TOP_K):
                w = w + jnp.where(r == pos[:, k:k + 1], gate[:, k:k + 1], 0.0)
            acc = acc + jnp.dot(w.astype(BF16), y_ref[y0 + rc * cols:y0 + (rc + 1) * cols, :],
                                preferred_element_type=F32)
        rows = slice(blk * tb, (blk + 1) * tb)
        o_ref[rows, :] = _layer_norm(DEEPNORM_ALPHA * x1_ref[rows, :] + acc, g_ref[...], b_ref[...])


def _combine(y_local, pos, gates, x1, g, b):
    n, d = x1.shape
    tb = TOKEN_TILE
    nb = n // tb
    per = COMBINE_BLOCKS
    assert nb % per == 0
    row = lambda: pl.BlockSpec((1, d), lambda i: (0, 0))
    return pl.pallas_call(
        _combine_body,
        grid=(nb // per,),
        in_specs=[
            pl.BlockSpec((per * LOCAL_ROWS, d), lambda i: (i, 0)),
            pl.BlockSpec((per, tb, TOP_K), lambda i: (i, 0, 0)),
            pl.BlockSpec((per, tb, TOP_K), lambda i: (i, 0, 0)),
            pl.BlockSpec((per * tb, d), lambda i: (i, 0)),
            row(), row(),
        ],
        out_specs=pl.BlockSpec((per * tb, d), lambda i: (i, 0)),
        out_shape=jax.ShapeDtypeStruct((n, d), F32),
        compiler_params=_cparams(("arbitrary",), 48),
        name="combine_ln",
    )(y_local, pos, gates, x1, g, b)


def kernel(x, emb_ln_g, emb_ln_b, w_in, lambda_q1, lambda_k1, lambda_q2, lambda_k2, subln_g, conv_w, conv_b,
           conv_ln_g, conv_ln_b, w_out, ln1_g, ln1_b, router_w, router_b, w_gate_up, b_gate_up, w_down, b_down,
           ln2_g, ln2_b):
    batch, seq, d = x.shape
    n = batch * seq
    assert n % TOKEN_TILE == 0 and seq % ATTN_KV_BLOCK == 0 and seq % CONV_TILE == 0
    assert w_in.shape[0] == DEPTH
    x2 = x.reshape(n, d)
    row = lambda v: v.reshape(1, -1).astype(F32)
    slopes = jnp.array([(2.0 ** (-8.0 / N_HEADS)) ** (i + 1) for i in range(N_HEADS)], dtype=F32)

    h = _inproj(x2, row(emb_ln_g), row(emb_ln_b), w_in[0].astype(BF16))
    a, c = _attention_and_conv(h, slopes, row(lambda_q1[0]), row(lambda_k1[0]), row(lambda_q2[0]),
                               row(lambda_k2[0]), row(subln_g[0]), conv_w[0], row(conv_b[0]), batch=batch, seq=seq)
    rw_t = router_w[0].T
    rw_hi = rw_t.astype(BF16)
    rw_lo = (rw_t - rw_hi.astype(F32)).astype(BF16)
    x1, xs_local, pos_t, gates_t, cc = _mix(
        x2, a, c, w_out[0].astype(BF16), row(emb_ln_g), row(emb_ln_b), row(ln1_g[0]), row(ln1_b[0]),
        row(conv_ln_g[0]), row(conv_ln_b[0]),
        jnp.concatenate([rw_hi, rw_lo], axis=0), router_b[0].reshape(N_EXPERTS, 1))

    te, wsel, nv, src, dst, nt = _build_schedule(cc[:, :, 0])
    nb = n // TOKEN_TILE
    y_chunks = _moe(xs_local.reshape(nb * CHUNKS_PER_BLOCK, BF16_ROWS, d), te, wsel, nv, src, dst, nt,
                    w_gate_up[0], b_gate_up[0].reshape(N_EXPERTS, 1, -1), w_down[0],
                    b_down[0].reshape(N_EXPERTS, 1, -1))
    y_local = y_chunks.reshape(-1, d)
    out = _combine(y_local, jnp.transpose(pos_t, (0, 2, 1)), jnp.transpose(gates_t, (0, 2, 1)), x1,
                   row(ln2_g[0]), row(ln2_b[0]))
    return out.reshape(batch, seq, d)
```

```python
import functools

import jax
import jax.numpy as jnp
from jax import lax
from jax.experimental import pallas as pl
from jax.experimental.pallas import tpu as pltpu

F32 = jnp.float32
BF16 = jnp.bfloat16
I32 = jnp.int32

N_HEADS = 4
HEAD_DIM = 64
V_DIM = 128
ATTN_WIDTH = N_HEADS * V_DIM
CONV_WIDTH = 512
CONV_KERNEL = 31
N_EXPERTS = 32
TOP_K = 4
SWIGLU_LIMIT = 7.0
SWIGLU_ALPHA = 1.702
LN_EPS = 1e-5
DEPTH = 1
DEEPNORM_ALPHA = (2.0 * DEPTH) ** 0.25
LAMBDA_INIT = 0.2
LOG2_E = 1.4426950408889634
Q_COLS = N_HEADS * 2 * HEAD_DIM
Q_SCALE = HEAD_DIM ** -0.5 * LOG2_E

LANES = 128
BF16_ROWS = 16

TOKEN_TILE = 512
INPROJ_TILE = 1024
ATTN_KV_BLOCK = 256
ATTN_SUB = 256
CONV_TILE = 128
CONV_ROW_STRIDE = 4
CONV_GLU_TILE = 128
CONV_PAD = 32
MOE_TILE = 1024
MOE_ROW_QUANTUM = 128
CHUNKS_PER_TILE = MOE_TILE // BF16_ROWS
LOCAL_ROWS = TOP_K * TOKEN_TILE + N_EXPERTS * BF16_ROWS
CHUNKS_PER_BLOCK = LOCAL_ROWS // BF16_ROWS
COMBINE_BLOCKS = 2


def _cparams(semantics, vmem_mib):
    return pltpu.CompilerParams(dimension_semantics=semantics, vmem_limit_bytes=vmem_mib * 1024 * 1024)


def _layer_norm(x, g, b):
    mu = jnp.mean(x, axis=-1, keepdims=True)
    xc = x - mu
    var = jnp.mean(xc * xc, axis=-1, keepdims=True)
    return xc * lax.rsqrt(var + LN_EPS) * g + b


def _sigmoid(x):
    return 1.0 / (1.0 + jnp.exp(-x))


def _inproj_body(x_ref, g_ref, b_ref, w_ref, h_ref, *, n_chunk):
    half = x_ref.shape[0] // 2
    for r0 in (0, half):
        xn = _layer_norm(x_ref[r0:r0 + half, :], g_ref[...], b_ref[...]).astype(BF16)
        for j in range(w_ref.shape[1] // n_chunk):
            sl = slice(j * n_chunk, (j + 1) * n_chunk)
            hj = jnp.dot(xn, w_ref[:, sl], preferred_element_type=F32)
            if (j + 1) * n_chunk <= Q_COLS:
                hj = hj * Q_SCALE
            h_ref[r0:r0 + half, sl] = hj.astype(BF16)


def _inproj(x2, g, b, w_bf16):
    n, d = x2.shape
    cols = w_bf16.shape[1]
    tm = INPROJ_TILE
    return pl.pallas_call(
        functools.partial(_inproj_body, n_chunk=512),
        grid=(n // tm,),
        in_specs=[
            pl.BlockSpec((tm, d), lambda i: (i, 0)),
            pl.BlockSpec((1, d), lambda i: (0, 0)),
            pl.BlockSpec((1, d), lambda i: (0, 0)),
            pl.BlockSpec((d, cols), lambda i: (0, 0)),
        ],
        out_specs=pl.BlockSpec((tm, cols), lambda i: (i, 0)),
        out_shape=jax.ShapeDtypeStruct((n, cols), BF16),
        compiler_params=_cparams(("arbitrary",), 40),
        name="inproj",
    )(x2, g, b, w_bf16)


def _attn_body(slopes_ref, lq1_ref, lk1_ref, lq2_ref, lk2_ref, g_ref, q_ref, k_ref, v_ref,
               ca_ref, cg_ref, cw_ref, cb_ref, o_ref, cpre_ref,
               q2_ref, m_ref, l_ref, acc_ref, glu_ref, *, seq):
    head = pl.program_id(1)
    slope = slopes_ref[head] * LOG2_E
    lam =(jnp.exp(jnp.sum(lq1_ref[...] * lk1_ref[...], axis=-1, keepdims=True))
           - jnp.exp(jnp.sum(lq2_ref[...] * lk2_ref[...], axis=-1, keepdims=True))
           + LAMBDA_INIT)

    q = q_ref[...]
    lane = lax.broadcasted_iota(I32, q.shape, 1)
    zero = jnp.zeros_like(q)
    q2_ref[0:seq, :] = jnp.where(lane < HEAD_DIM, q, zero)
    q2_ref[seq:, :] = jnp.where(lane >= HEAD_DIM, q, zero)
    m_ref[...] = jnp.full(m_ref.shape, -jnp.inf, F32)
    l_ref[...] = jnp.zeros(l_ref.shape, F32)
    acc_ref[...] = jnp.zeros(acc_ref.shape, F32)
    kb_rows = ATTN_KV_BLOCK
    sub = ATTN_SUB
    ones = jnp.ones((kb_rows, V_DIM), BF16)
    col = lax.broadcasted_iota(I32, (1, kb_rows), 1)

    glu_ref[0:CONV_PAD, :] = jnp.zeros((CONV_PAD, LANES), F32)
    for st in range(0, seq, CONV_GLU_TILE):
        glu_ref[CONV_PAD + st:CONV_PAD + st + CONV_GLU_TILE, :] = (
            ca_ref[st:st + CONV_GLU_TILE, :].astype(F32) * _sigmoid(cg_ref[st:st + CONV_GLU_TILE, :].astype(F32)))

    def conv_tile(st):
        taps = [jnp.broadcast_to(cw_ref[j:j + 1, :], (8, LANES)) for j in range(CONV_KERNEL)]
        bias = jnp.broadcast_to(cb_ref[...], (8, LANES))
        for t0 in range(st, st + CONV_TILE, 8 * CONV_ROW_STRIDE):
            for p in range(CONV_ROW_STRIDE):
                acc = bias
                for j in range(CONV_KERNEL):
                    src = pl.ds(t0 + p + (CONV_PAD - (CONV_KERNEL - 1) + j), 8, stride=CONV_ROW_STRIDE)
                    acc = acc + glu_ref[src, :] * taps[j]
                cpre_ref[pl.ds(t0 + p, 8, stride=CONV_ROW_STRIDE), :] = acc

    conv_tiles = list(range(0, seq, CONV_TILE))
    n_kv = seq // kb_rows

    for d in range(n_kv):
        for st in conv_tiles[d * len(conv_tiles) // n_kv:(d + 1) * len(conv_tiles) // n_kv]:
            conv_tile(st)
        k0 = d * kb_rows
        kb = k_ref[k0:k0 + kb_rows, :]
        vext = jnp.concatenate([v_ref[k0:k0 + kb_rows, :], ones], axis=1)
        for half in range(2):
            for i in range(k0 // sub, seq // sub):
                r0 = i * sub
                rows = slice(half * seq + r0, half * seq + r0 + sub)
                ncol = min(kb_rows, -(-(r0 + sub - k0) // LANES) * LANES)
                bias = (col[:, 0:ncol] + (k0 - r0)).astype(F32) * slope
                s = lax.dot_general(q2_ref[rows, :], kb[0:ncol, :], (((1,), (1,)), ((), ())),
                                    preferred_element_type=F32) + bias
                if k0 + ncol - 1 > r0:
                    row = lax.broadcasted_iota(I32, s.shape, 0) + (r0 - k0)
                    s = jnp.where(lax.broadcasted_iota(I32, s.shape, 1) <= row, s, -jnp.inf)
                m_prev = m_ref[rows, :]
                m_new = jnp.maximum(m_prev, jnp.max(s, axis=1, keepdims=True))
                alpha = jnp.exp2(m_prev - m_new)
                p = jnp.exp2(s - jnp.concatenate([m_new] * (ncol // LANES), axis=1))
                pv = jnp.dot(p.astype(BF16), vext[0:ncol, :], preferred_element_type=F32)
                acc_ref[rows, :] = alpha * acc_ref[rows, :] + pv[:, :V_DIM]
                l_ref[rows, :] = alpha * l_ref[rows, :] + pv[:, V_DIM:]
                m_ref[rows, :] = m_new

    o = acc_ref[...] / l_ref[...]
    od = o[:seq] - lam * o[seq:]
    ms = jnp.mean(od * od, axis=-1, keepdims=True)
    out = od * lax.rsqrt(ms + LN_EPS) * g_ref[...] * (1.0 - LAMBDA_INIT)
    o_ref[...] = out.astype(BF16)


def _attention_and_conv(h, slopes, lq1, lk1, lq2, lk2, subln_g, conv_w, conv_b, *, batch, seq):
    n = h.shape[0]
    assert CONV_WIDTH // LANES == N_HEADS and seq % CONV_TILE == 0 and seq % CONV_GLU_TILE == 0
    vec = lambda: pl.BlockSpec((1, HEAD_DIM), lambda b, hd, *_: (0, 0))
    slab0 = 3 * ATTN_WIDTH // LANES
    grid_spec = pltpu.PrefetchScalarGridSpec(
        num_scalar_prefetch=1,
        grid=(batch, N_HEADS),
        in_specs=[
            vec(), vec(), vec(), vec(),
            pl.BlockSpec((1, V_DIM), lambda b, hd, *_: (0, 0)),
            pl.BlockSpec((seq, V_DIM), lambda b, hd, *_: (b, hd)),
            pl.BlockSpec((seq, V_DIM), lambda b, hd, *_: (b, N_HEADS + hd)),
            pl.BlockSpec((seq, V_DIM), lambda b, hd, *_: (b, 2 * N_HEADS + hd)),
            pl.BlockSpec((seq, LANES), lambda b, hd, *_: (b, slab0 + hd)),
            pl.BlockSpec((seq, LANES), lambda b, hd, *_: (b, slab0 + N_HEADS + hd)),
            pl.BlockSpec((CONV_KERNEL, LANES), lambda b, hd, *_: (0, hd)),
            pl.BlockSpec((1, LANES), lambda b, hd, *_: (0, hd)),
        ],
        out_specs=[
            pl.BlockSpec((seq, V_DIM), lambda b, hd, *_: (b, hd)),
            pl.BlockSpec((seq, LANES), lambda b, hd, *_: (b, hd)),
        ],
        scratch_shapes=[
            pltpu.VMEM((2 * seq, V_DIM), BF16),
            pltpu.VMEM((2 * seq, LANES), F32),
            pltpu.VMEM((2 * seq, LANES), F32),
            pltpu.VMEM((2 * seq, V_DIM), F32),
            pltpu.VMEM((CONV_PAD + seq, LANES), F32),
        ],
    )
    return pl.pallas_call(
        functools.partial(_attn_body, seq=seq),
        grid_spec=grid_spec,
        out_shape=[jax.ShapeDtypeStruct((n, ATTN_WIDTH), BF16), jax.ShapeDtypeStruct((n, CONV_WIDTH), F32)],
        compiler_params=_cparams(("arbitrary", "arbitrary"), 32),
        name="diff_attn_conv",
    )(slopes, lq1, lk1, lq2, lk2, subln_g, h, h, h, h, h, conv_w, conv_b)


def _mix_body(x_ref, a_ref, c_ref, wo_ref, g0_ref, b0_ref, g1_ref, b1_ref, cg_ref, cb_ref, rw2_ref, rb_ref,
              x1_ref, xs_ref, pos_ref, gate_ref, cc_ref):
    tb = x_ref.shape[0]
    xn = _layer_norm(x_ref[...], g0_ref[...], b0_ref[...])
    conv = _layer_norm(c_ref[...], cg_ref[...], cb_ref[...])
    conv = (conv * _sigmoid(conv)).astype(BF16)
    mix = (jnp.dot(a_ref[...], wo_ref[0:ATTN_WIDTH, :], preferred_element_type=F32)
           + jnp.dot(conv, wo_ref[ATTN_WIDTH:, :], preferred_element_type=F32))
    x1 = _layer_norm(DEEPNORM_ALPHA * xn + mix, g1_ref[...], b1_ref[...])
    x1_ref[...] = x1
    x1b = x1.astype(BF16)

    x_lo = (x1 - x1b.astype(F32)).astype(BF16)
    rw2 = rw2_ref[...]
    nt_dims = (((1,), (1,)), ((), ()))
    both = lax.dot_general(rw2, x1b, nt_dims, preferred_element_type=F32)
    logits = ((lax.dot_general(rw2[:N_EXPERTS], x_lo, nt_dims, preferred_element_type=F32) + both[N_EXPERTS:])
              + both[:N_EXPERTS]) + rb_ref[...]
    eidx = lax.broadcasted_iota(I32, logits.shape, 0)
    work = logits
    vals, hots = [], []
    for _ in range(TOP_K):
        mx = jnp.max(work, axis=0, keepdims=True)
        sel = jnp.min(jnp.where(work == mx, eidx, N_EXPERTS), axis=0, keepdims=True)
        hot = eidx == sel
        vals.append(mx)
        hots.append(hot)
        work = jnp.where(hot, -jnp.inf, work)
    ex = [jnp.exp(v - vals[0]) for v in vals]
    den = ex[0] + ex[1] + ex[2] + ex[3]
    for k in range(TOP_K):
        gate_ref[k:k + 1, :] = ex[k] / den

    member = jnp.zeros(logits.shape, F32)
    for hot in hots:
        member = member + jnp.where(hot, 1.0, 0.0)
    before = (lax.broadcasted_iota(I32, (tb, tb), 0) < lax.broadcasted_iota(I32, (tb, tb), 1))
    rank = jnp.dot(member.astype(BF16), jnp.where(before, 1.0, 0.0).astype(BF16), preferred_element_type=F32)
    count = jnp.sum(member, axis=1, keepdims=True)
    chunks = jnp.floor((count + (BF16_ROWS - 1.0)) * (1.0 / BF16_ROWS))
    cc_ref[...] = jnp.broadcast_to(chunks, cc_ref.shape).astype(I32)
    lower = (lax.broadcasted_iota(I32, (N_EXPERTS, N_EXPERTS), 1) < lax.broadcasted_iota(I32, (N_EXPERTS, N_EXPERTS), 0))
    seg_start = jnp.dot(jnp.where(lower, 1.0, 0.0).astype(BF16),
                        jnp.broadcast_to(chunks, (N_EXPERTS, LANES)).astype(BF16),
                        preferred_element_type=F32)[:, 0:1]
    slot = seg_start * float(BF16_ROWS) + rank
    pos = [jnp.sum(jnp.where(hot, slot, 0.0), axis=0, keepdims=True).astype(I32) for hot in hots]
    for k in range(TOP_K):
        pos_ref[k:k + 1, :] = pos[k]

    rows = 256
    one, nil = jnp.ones((), BF16), jnp.zeros((), BF16)
    pos16 = [p.astype(jnp.int16) for p in pos]
    for rc in range(xs_ref.shape[0] // rows):
        r = lax.broadcasted_iota(jnp.int16, (rows, tb), 0) + jnp.int16(rc * rows)
        hit = (r == pos16[0]) | (r == pos16[1]) | (r == pos16[2]) | (r == pos16[3])
        sel = jnp.where(hit, one, nil)
        xs_ref[rc * rows:(rc + 1) * rows, :] = jnp.dot(sel, x1b, preferred_element_type=F32).astype(BF16)


def _mix(x2, a, c, wo_bf16, g0, b0, g1, b1, cg, cb, rw2, rb):
    n, d = x2.shape
    tb = TOKEN_TILE
    nb = n // tb
    row = lambda: pl.BlockSpec((1, d), lambda i: (0, 0))
    return pl.pallas_call(
        _mix_body,
        grid=(nb,),
        in_specs=[
            pl.BlockSpec((tb, d), lambda i: (i, 0)),
            pl.BlockSpec((tb, ATTN_WIDTH), lambda i: (i, 0)),
            pl.BlockSpec((tb, CONV_WIDTH), lambda i: (i, 0)),
            pl.BlockSpec((d, d), lambda i: (0, 0)),
            row(), row(), row(), row(),
            pl.BlockSpec((1, CONV_WIDTH), lambda i: (0, 0)),
            pl.BlockSpec((1, CONV_WIDTH), lambda i: (0, 0)),
            pl.BlockSpec((2 * N_EXPERTS, d), lambda i: (0, 0)),
            pl.BlockSpec((N_EXPERTS, 1), lambda i: (0, 0)),
        ],
        out_specs=[
            pl.BlockSpec((tb, d), lambda i: (i, 0)),
            pl.BlockSpec((None, LOCAL_ROWS, d), lambda i: (i, 0, 0)),
            pl.BlockSpec((None, TOP_K, tb), lambda i: (i, 0, 0)),
            pl.BlockSpec((None, TOP_K, tb), lambda i: (i, 0, 0)),
            pl.BlockSpec((None, N_EXPERTS, LANES), lambda i: (i, 0, 0)),
        ],
        out_shape=[
            jax.ShapeDtypeStruct((n, d), F32),
            jax.ShapeDtypeStruct((nb, LOCAL_ROWS, d), BF16),
            jax.ShapeDtypeStruct((nb, TOP_K, tb), I32),
            jax.ShapeDtypeStruct((nb, TOP_K, tb), F32),
            jax.ShapeDtypeStruct((nb, N_EXPERTS, LANES), I32),
        ],
        compiler_params=_cparams(("arbitrary",), 48),
        name="mix_router_sort",
    )(x2, a, c, wo_bf16, g0, b0, g1, b1, cg, cb, rw2, rb)


def _max_tiles(nb):
    return (nb * CHUNKS_PER_BLOCK + CHUNKS_PER_TILE - 1) // CHUNKS_PER_TILE + N_EXPERTS + 1


def _build_schedule(cc):
    nb = cc.shape[0]
    cpt = CHUNKS_PER_TILE
    ne1 = N_EXPERTS + 1
    tail = CHUNKS_PER_BLOCK - jnp.sum(cc, axis=1)
    cce = jnp.concatenate([cc, tail[:, None]], axis=1)
    seg_start = jnp.cumsum(cce, axis=1) - cce
    tot = jnp.sum(cce, axis=0)
    tiles_e = (tot + cpt - 1) // cpt
    tile_end = jnp.cumsum(tiles_e)
    max_tiles = _max_tiles(nb)

    def lookup(passed, table):
        return table[0] + jnp.sum(jnp.where(passed, (table[1:] - table[:-1])[None, :], 0), axis=1)

    t = jnp.arange(max_tiles, dtype=I32)
    passed = tile_end[None, :] <= t[:, None]
    te = jnp.minimum(jnp.sum(passed, axis=1), ne1 - 1).astype(I32)
    experts = jnp.arange(ne1, dtype=I32)
    has_tiles = (tiles_e > 0) & (experts < N_EXPERTS)
    later = jnp.where(has_tiles[None, :] & (experts[None, :] > te[:, None]), experts[None, :], ne1)
    nxt = jnp.min(later, axis=1)
    own = jnp.where(te < N_EXPERTS, te, jnp.max(jnp.where(has_tiles, experts, 0)))
    is_first = jnp.any(((tile_end - tiles_e)[None, :] == t[:, None]) & (tiles_e > 0)[None, :], axis=1)
    wsel = jnp.where(is_first | (nxt >= N_EXPERTS), own, nxt).astype(I32)

    pad_len = tiles_e * cpt - tot
    seg_len = jnp.concatenate([cce.T, pad_len[:, None]], axis=1).reshape(-1)
    seg_end = jnp.cumsum(seg_len)
    block = jnp.arange(nb, dtype=I32)[None, :]
    is_real = jnp.concatenate([jnp.ones((ne1, nb), I32), jnp.zeros((ne1, 1), I32)], axis=1).reshape(-1)
    seg_first = (seg_end - seg_len) * is_real
    zero = jnp.zeros((1,), I32)

    base = block * CHUNKS_PER_BLOCK + seg_start.T
    first = jnp.concatenate([jnp.concatenate([base, jnp.zeros((ne1, 1), I32)], axis=1).reshape(-1) - seg_first,
                             zero])
    p = jnp.arange(max_tiles * cpt, dtype=I32)
    passed = seg_end[None, :] <= p[:, None]
    real = lookup(passed, jnp.concatenate([is_real, zero]))
    src = (real * p + lookup(passed, first)).astype(I32)
    dump0 = nb * CHUNKS_PER_BLOCK
    dst = jnp.where(real > 0, src, dump0 + ((p // cpt) % 2) * cpt + p % cpt).astype(I32)
    dummy = dump0 + cpt + jnp.arange(cpt, dtype=I32)
    nv = jnp.sum(real.reshape(max_tiles, cpt), axis=1).astype(I32)
    return te, wsel, nv, src, jnp.concatenate([dummy, dst]), jnp.reshape(tile_end[-1], (1,)).astype(I32)


def _moe_body(te_ref, wsel_ref, nv_ref, src_ref, dst_ref, nt_ref, xs_hbm, wgu_ref, bgu_ref, wd_ref, bd_ref, y_hbm,
              xbuf, ybuf, act, wgu_b, wd_b, in_sem, out_sem):
    t = pl.program_id(0)
    last = nt_ref[0] - 1
    cpt = CHUNKS_PER_TILE
    f = wd_ref.shape[0]
    slot = t % 2
    other = 1 - slot
    e = te_ref[t]

    def rows(c):
        return pl.ds(c * BF16_ROWS, BF16_ROWS)

    def in_copy(tile, buf, c):
        return pltpu.make_async_copy(xs_hbm.at[src_ref[tile * cpt + c]], xbuf.at[buf, rows(c)], in_sem.at[buf])

    def out_copy(tile, buf, c):
        return pltpu.make_async_copy(ybuf.at[buf, rows(c)], y_hbm.at[dst_ref[(tile + 1) * cpt + c]],
                                     out_sem.at[buf])

    def start_all(copy, tile, buf):
        for c in range(cpt):
            copy(tile, buf, c).start()

    def wait_all(copy, tile, buf):
        for c in range(cpt):
            copy(tile, buf, c).wait()

    @pl.when(t == 0)
    def _():
        start_all(in_copy, 0, 0)
        ybuf[...] = jnp.zeros(ybuf.shape, BF16)
        even_dump = [pltpu.make_async_copy(ybuf.at[1, rows(c)], y_hbm.at[y_hbm.shape[0] - 2 * cpt + c],
                                           out_sem.at[0]) for c in range(cpt)]
        for copy in even_dump:
            copy.start()
        for copy in even_dump:
            copy.wait()

    @pl.when(t <= last)
    def _():
        @pl.when(t >= 1)
        def _():
            wait_all(out_copy, t - 2, slot)

        wait_all(in_copy, t, slot)
        start_all(in_copy, jnp.minimum(t + 1, last), other)
        start_all(out_copy, t - 1, other)

        @pl.when((e < N_EXPERTS) & ((t == 0) | (te_ref[jnp.maximum(t - 1, 0)] != e)))
        def _():
            wgu_b[...] = wgu_ref[...].astype(BF16)
            wd_b[...] = wd_ref[...].astype(BF16)

        def expert_mlp(m):
            x = xbuf[slot, 0:m, :]
            nc = 512
            for j in range(f // nc):
                gsl = slice(j * nc, (j + 1) * nc)
                usl = slice(f + j * nc, f + (j + 1) * nc)
                g = jnp.dot(x, wgu_b[:, gsl], preferred_element_type=F32) + bgu_ref[:, gsl]
                u = jnp.dot(x, wgu_b[:, usl], preferred_element_type=F32) + bgu_ref[:, usl]
                g = jnp.minimum(g, SWIGLU_LIMIT)
                u = jnp.clip(u, -SWIGLU_LIMIT, SWIGLU_LIMIT)
                act[0:m, gsl] = (g * _sigmoid(SWIGLU_ALPHA * g) * (u + 1.0)).astype(BF16)
            a = act[0:m, :]
            for j in range(wd_ref.shape[1] // nc):
                sl = slice(j * nc, (j + 1) * nc)
                y = jnp.dot(a, wd_b[:, sl], preferred_element_type=F32) + bd_ref[:, sl]
                ybuf[slot, 0:m, sl] = y.astype(BF16)

        quanta = (nv_ref[t] * BF16_ROWS + (MOE_ROW_QUANTUM - 1)) // MOE_ROW_QUANTUM
        for k in range(1, MOE_TILE // MOE_ROW_QUANTUM + 1):
            pl.when((e < N_EXPERTS) & (quanta == k))(functools.partial(expert_mlp, k * MOE_ROW_QUANTUM))

        @pl.when(e >= N_EXPERTS)
        def _():
            ybuf[slot] = jnp.zeros(ybuf.shape[1:], BF16)

        @pl.when(t == last)
        def _():
            start_all(out_copy, t, slot)
            wait_all(out_copy, t - 1, other)
            wait_all(out_copy, t, slot)
            wait_all(in_copy, t, other)


def _moe(xs_chunks, te, wsel, nv, src, dst, nt, w_gate_up, b_gate_up, w_down, b_down):
    nch, _, d = xs_chunks.shape
    f = w_down.shape[1]
    max_tiles = te.shape[0]
    by_wsel = lambda t, te_r, wsel_r, nv_r, src_r, dst_r, nt_r: (wsel_r[t], 0, 0)
    by_tile = lambda t, te_r, wsel_r, nv_r, src_r, dst_r, nt_r: (jnp.minimum(te_r[t], N_EXPERTS - 1), 0, 0)
    grid_spec = pltpu.PrefetchScalarGridSpec(
        num_scalar_prefetch=6,
        grid=(max_tiles,),
        in_specs=[
            pl.BlockSpec(memory_space=pl.ANY),
            pl.BlockSpec((None, d, 2 * f), by_wsel),
            pl.BlockSpec((None, 1, 2 * f), by_tile),
            pl.BlockSpec((None, f, d), by_wsel),
            pl.BlockSpec((None, 1, d), by_tile),
        ],
        out_specs=pl.BlockSpec(memory_space=pl.ANY),
        scratch_shapes=[
            pltpu.VMEM((2, MOE_TILE, d), BF16),
            pltpu.VMEM((2, MOE_TILE, d), BF16),
            pltpu.VMEM((MOE_TILE, f), BF16),
            pltpu.VMEM((d, 2 * f), BF16),
            pltpu.VMEM((f, d), BF16),
            pltpu.SemaphoreType.DMA((2,)),
            pltpu.SemaphoreType.DMA((2,)),
        ],
    )
    return pl.pallas_call(
        _moe_body,
        grid_spec=grid_spec,
        out_shape=jax.ShapeDtypeStruct((nch + 2 * CHUNKS_PER_TILE, BF16_ROWS, d), BF16),
        compiler_params=_cparams(("arbitrary",), 56),
        name="moe_grouped",
    )(te, wsel, nv, src, dst, nt, xs_chunks, w_gate_up, b_gate_up, w_down, b_down)


def _combine_body(y_ref, pos_ref, gate_ref, x1_ref, g_ref, b_ref, o_ref):
    tb = TOKEN_TILE
    cols = 512
    for blk in range(COMBINE_BLOCKS):
        pos = pos_ref[blk]
        gate = gate_ref[blk]
        y0 = blk * LOCAL_ROWS
        acc = jnp.zeros((tb, x1_ref.shape[1]), F32)
        for rc in range(LOCAL_ROWS // cols):
            r = lax.broadcasted_iota(I32, (tb, cols), 1) + rc * cols
            w = jnp.zeros((tb, cols), F32)
            for k in range(TOP_K):
                w = w + jnp.where(r == pos[:, k:k + 1], gate[:, k:k + 1], 0.0)
            acc = acc + jnp.dot(w.astype(BF16), y_ref[y0 + rc * cols:y0 + (rc + 1) * cols, :],
                                preferred_element_type=F32)
        rows = slice(blk * tb, (blk + 1) * tb)
        o_ref[rows, :] = _layer_norm(DEEPNORM_ALPHA * x1_ref[rows, :] + acc, g_ref[...], b_ref[...])


def _combine(y_local, pos, gates, x1, g, b):
    n, d = x1.shape
    tb = TOKEN_TILE
    nb = n // tb
    per = COMBINE_BLOCKS
    assert nb % per == 0
    row = lambda: pl.BlockSpec((1, d), lambda i: (0, 0))
    return pl.pallas_call(
        _combine_body,
        grid=(nb // per,),
        in_specs=[
            pl.BlockSpec((per * LOCAL_ROWS, d), lambda i: (i, 0)),
            pl.BlockSpec((per, tb, TOP_K), lambda i: (i, 0, 0)),
            pl.BlockSpec((per, tb, TOP_K), lambda i: (i, 0, 0)),
            pl.BlockSpec((per * tb, d), lambda i: (i, 0)),
            row(), row(),
        ],
        out_specs=pl.BlockSpec((per * tb, d), lambda i: (i, 0)),
        out_shape=jax.ShapeDtypeStruct((n, d), F32),
        compiler_params=_cparams(("arbitrary",), 48),
        name="combine_ln",
    )(y_local, pos, gates, x1, g, b)


def kernel(x, emb_ln_g, emb_ln_b, w_in, lambda_q1, lambda_k1, lambda_q2, lambda_k2, subln_g, conv_w, conv_b,
           conv_ln_g, conv_ln_b, w_out, ln1_g, ln1_b, router_w, router_b, w_gate_up, b_gate_up, w_down, b_down,
           ln2_g, ln2_b):
    batch, seq, d = x.shape
    n = batch * seq
    assert n % TOKEN_TILE == 0 and seq % ATTN_KV_BLOCK == 0 and seq % CONV_TILE == 0
    assert w_in.shape[0] == DEPTH
    x2 = x.reshape(n, d)
    row = lambda v: v.reshape(1, -1).astype(F32)
    slopes = jnp.array([(2.0 ** (-8.0 / N_HEADS)) ** (i + 1) for i in range(N_HEADS)], dtype=F32)

    h = _inproj(x2, row(emb_ln_g), row(emb_ln_b), w_in[0].astype(BF16))
    a, c = _attention_and_conv(h, slopes, row(lambda_q1[0]), row(lambda_k1[0]), row(lambda_q2[0]),
                               row(lambda_k2[0]), row(subln_g[0]), conv_w[0], row(conv_b[0]), batch=batch, seq=seq)
    rw_t = router_w[0].T
    rw_hi = rw_t.astype(BF16)
    rw_lo = (rw_t - rw_hi.astype(F32)).astype(BF16)
    x1, xs_local, pos_t, gates_t, cc = _mix(
        x2, a, c, w_out[0].astype(BF16), row(emb_ln_g), row(emb_ln_b), row(ln1_g[0]), row(ln1_b[0]),
        row(conv_ln_g[0]), row(conv_ln_b[0]),
        jnp.concatenate([rw_hi, rw_lo], axis=0), router_b[0].reshape(N_EXPERTS, 1))

    te, wsel, nv, src, dst, nt = _build_schedule(cc[:, :, 0])
    nb = n // TOKEN_TILE
    y_chunks = _moe(xs_local.reshape(nb * CHUNKS_PER_BLOCK, BF16_ROWS, d), te, wsel, nv, src, dst, nt,
                    w_gate_up[0], b_gate_up[0].reshape(N_EXPERTS, 1, -1), w_down[0],
                    b_down[0].reshape(N_EXPERTS, 1, -1))
    y_local = y_chunks.reshape(-1, d)
    out = _combine(y_local, jnp.transpose(pos_t, (0, 2, 1)), jnp.transpose(gates_t, (0, 2, 1)), x1,
                   row(ln2_g[0]), row(ln2_b[0]))
    return out.reshape(batch, seq, d)
```

```python
import functools

import jax
import jax.numpy as jnp
from jax import lax
from jax.experimental import pallas as pl
from jax.experimental.pallas import tpu as pltpu

F32 = jnp.float32
BF16 = jnp.bfloat16
I32 = jnp.int32

N_HEADS = 4
HEAD_DIM = 64
V_DIM = 128
ATTN_WIDTH = N_HEADS * V_DIM
CONV_WIDTH = 512
CONV_KERNEL = 31
N_EXPERTS = 32
TOP_K = 4
SWIGLU_LIMIT = 7.0
SWIGLU_ALPHA = 1.702
LN_EPS = 1e-5
DEPTH = 1
DEEPNORM_ALPHA = (2.0 * DEPTH) ** 0.25
LAMBDA_INIT = 0.2
LOG2_E = 1.4426950408889634
Q_COLS = N_HEADS * 2 * HEAD_DIM
Q_SCALE = HEAD_DIM ** -0.5 * LOG2_E

LANES = 128
BF16_ROWS = 16

TOKEN_TILE = 512
INPROJ_TILE = 1024
ATTN_KV_BLOCK = 256
ATTN_SUB = 256
CONV_TILE = 128
CONV_ROW_STRIDE = 4
CONV_GLU_TILE = 128
CONV_PAD = 32
MOE_TILE = 1024
MOE_ROW_QUANTUM = 256
CHUNKS_PER_TILE = MOE_TILE // BF16_ROWS
LOCAL_ROWS = TOP_K * TOKEN_TILE + N_EXPERTS * BF16_ROWS
CHUNKS_PER_BLOCK = LOCAL_ROWS // BF16_ROWS
COMBINE_BLOCKS = 2


def _cparams(semantics, vmem_mib):
    return pltpu.CompilerParams(dimension_semantics=semantics, vmem_limit_bytes=vmem_mib * 1024 * 1024)


def _layer_norm(x, g, b):
    mu = jnp.mean(x, axis=-1, keepdims=True)
    xc = x - mu
    var = jnp.mean(xc * xc, axis=-1, keepdims=True)
    return xc * lax.rsqrt(var + LN_EPS) * g + b


def _sigmoid(x):
    return 1.0 / (1.0 + jnp.exp(-x))


def _inproj_body(x_ref, g_ref, b_ref, w_ref, h_ref, *, n_chunk):
    half = x_ref.shape[0] // 2
    for r0 in (0, half):
        xn = _layer_norm(x_ref[r0:r0 + half, :], g_ref[...], b_ref[...]).astype(BF16)
        for j in range(w_ref.shape[1] // n_chunk):
            sl = slice(j * n_chunk, (j + 1) * n_chunk)
            hj = jnp.dot(xn, w_ref[:, sl], preferred_element_type=F32)
            if (j + 1) * n_chunk <= Q_COLS:
                hj = hj * Q_SCALE
            h_ref[r0:r0 + half, sl] = hj.astype(BF16)


def _inproj(x2, g, b, w_bf16):
    n, d = x2.shape
    cols = w_bf16.shape[1]
    tm = INPROJ_TILE
    return pl.pallas_call(
        functools.partial(_inproj_body, n_chunk=512),
        grid=(n // tm,),
        in_specs=[
            pl.BlockSpec((tm, d), lambda i: (i, 0)),
            pl.BlockSpec((1, d), lambda i: (0, 0)),
            pl.BlockSpec((1, d), lambda i: (0, 0)),
            pl.BlockSpec((d, cols), lambda i: (0, 0)),
        ],
        out_specs=pl.BlockSpec((tm, cols), lambda i: (i, 0)),
        out_shape=jax.ShapeDtypeStruct((n, cols), BF16),
        compiler_params=_cparams(("arbitrary",), 40),
        name="inproj",
    )(x2, g, b, w_bf16)


def _attn_body(slopes_ref, lq1_ref, lk1_ref, lq2_ref, lk2_ref, g_ref, q_ref, k_ref, v_ref,
               ca_ref, cg_ref, cw_ref, cb_ref, o_ref, cpre_ref,
               q2_ref, m_ref, l_ref, acc_ref, glu_ref, *, seq):
    head = pl.program_id(1)
    slope = slopes_ref[head] * LOG2_E
    lam =(jnp.exp(jnp.sum(lq1_ref[...] * lk1_ref[...], axis=-1, keepdims=True))
           - jnp.exp(jnp.sum(lq2_ref[...] * lk2_ref[...], axis=-1, keepdims=True))
           + LAMBDA_INIT)

    q = q_ref[...]
    lane = lax.broadcasted_iota(I32, q.shape, 1)
    zero = jnp.zeros_like(q)
    q2_ref[0:seq, :] = jnp.where(lane < HEAD_DIM, q, zero)
    q2_ref[seq:, :] = jnp.where(lane >= HEAD_DIM, q, zero)
    m_ref[...] = jnp.full(m_ref.shape, -jnp.inf, F32)
    l_ref[...] = jnp.zeros(l_ref.shape, F32)
    acc_ref[...] = jnp.zeros(acc_ref.shape, F32)
    kb_rows = ATTN_KV_BLOCK
    sub = ATTN_SUB
    ones = jnp.ones((kb_rows, V_DIM), BF16)
    col = lax.broadcasted_iota(I32, (1, kb_rows), 1)

    glu_ref[0:CONV_PAD, :] = jnp.zeros((CONV_PAD, LANES), F32)
    for st in range(0, seq, CONV_GLU_TILE):
        glu_ref[CONV_PAD + st:CONV_PAD + st + CONV_GLU_TILE, :] = (
            ca_ref[st:st + CONV_GLU_TILE, :].astype(F32) * _sigmoid(cg_ref[st:st + CONV_GLU_TILE, :].astype(F32)))

    def conv_tile(st):
        taps = [jnp.broadcast_to(cw_ref[j:j + 1, :], (8, LANES)) for j in range(CONV_KERNEL)]
        bias = jnp.broadcast_to(cb_ref[...], (8, LANES))
        for t0 in range(st, st + CONV_TILE, 8 * CONV_ROW_STRIDE):
            for p in range(CONV_ROW_STRIDE):
                acc = bias
                for j in range(CONV_KERNEL):
                    src = pl.ds(t0 + p + (CONV_PAD - (CONV_KERNEL - 1) + j), 8, stride=CONV_ROW_STRIDE)
                    acc = acc + glu_ref[src, :] * taps[j]
                cpre_ref[pl.ds(t0 + p, 8, stride=CONV_ROW_STRIDE), :] = acc

    conv_tiles = list(range(0, seq, CONV_TILE))
    n_kv = seq // kb_rows

    for d in range(n_kv):
        for st in conv_tiles[d * len(conv_tiles) // n_kv:(d + 1) * len(conv_tiles) // n_kv]:
            conv_tile(st)
        k0 = d * kb_rows
        kb = k_ref[k0:k0 + kb_rows, :]
        vext = jnp.concatenate([v_ref[k0:k0 + kb_rows, :], ones], axis=1)
        for half in range(2):
            for i in range(k0 // sub, seq // sub):
                r0 = i * sub
                rows = slice(half * seq + r0, half * seq + r0 + sub)
                ncol = min(kb_rows, -(-(r0 + sub - k0) // LANES) * LANES)
                bias = (col[:, 0:ncol] + (k0 - r0)).astype(F32) * slope
                s = lax.dot_general(q2_ref[rows, :], kb[0:ncol, :], (((1,), (1,)), ((), ())),
                                    preferred_element_type=F32) + bias
                if k0 + ncol - 1 > r0:
                    row = lax.broadcasted_iota(I32, s.shape, 0) + (r0 - k0)
                    s = jnp.where(lax.broadcasted_iota(I32, s.shape, 1) <= row, s, -jnp.inf)
                m_prev = m_ref[rows, :]
                m_new = jnp.maximum(m_prev, jnp.max(s, axis=1, keepdims=True))
                alpha = jnp.exp2(m_prev - m_new)
                p = jnp.exp2(s - jnp.concatenate([m_new] * (ncol // LANES), axis=1))
                pv = jnp.dot(p.astype(BF16), vext[0:ncol, :], preferred_element_type=F32)
                acc_ref[rows, :] = alpha * acc_ref[rows, :] + pv[:, :V_DIM]
                l_ref[rows, :] = alpha * l_ref[rows, :] + pv[:, V_DIM:]
                m_ref[rows, :] = m_new

    o = acc_ref[...] / l_ref[...]
    od = o[:seq] - lam * o[seq:]
    ms = jnp.mean(od * od, axis=-1, keepdims=True)
    out = od * lax.rsqrt(ms + LN_EPS) * g_ref[...] * (1.0 - LAMBDA_INIT)
    o_ref[...] = out.astype(BF16)


def _attention_and_conv(h, slopes, lq1, lk1, lq2, lk2, subln_g, conv_w, conv_b, *, batch, seq):
    n = h.shape[0]
    assert CONV_WIDTH // LANES == N_HEADS and seq % CONV_TILE == 0 and seq % CONV_GLU_TILE == 0
    vec = lambda: pl.BlockSpec((1, HEAD_DIM), lambda b, hd, *_: (0, 0))
    slab0 = 3 * ATTN_WIDTH // LANES
    grid_spec = pltpu.PrefetchScalarGridSpec(
        num_scalar_prefetch=1,
        grid=(batch, N_HEADS),
        in_specs=[
            vec(), vec(), vec(), vec(),
            pl.BlockSpec((1, V_DIM), lambda b, hd, *_: (0, 0)),
            pl.BlockSpec((seq, V_DIM), lambda b, hd, *_: (b, hd)),
            pl.BlockSpec((seq, V_DIM), lambda b, hd, *_: (b, N_HEADS + hd)),
            pl.BlockSpec((seq, V_DIM), lambda b, hd, *_: (b, 2 * N_HEADS + hd)),
            pl.BlockSpec((seq, LANES), lambda b, hd, *_: (b, slab0 + hd)),
            pl.BlockSpec((seq, LANES), lambda b, hd, *_: (b, slab0 + N_HEADS + hd)),
            pl.BlockSpec((CONV_KERNEL, LANES), lambda b, hd, *_: (0, hd)),
            pl.BlockSpec((1, LANES), lambda b, hd, *_: (0, hd)),
        ],
        out_specs=[
            pl.BlockSpec((seq, V_DIM), lambda b, hd, *_: (b, hd)),
            pl.BlockSpec((seq, LANES), lambda b, hd, *_: (b, hd)),
        ],
        scratch_shapes=[
            pltpu.VMEM((2 * seq, V_DIM), BF16),
            pltpu.VMEM((2 * seq, LANES), F32),
            pltpu.VMEM((2 * seq, LANES), F32),
            pltpu.VMEM((2 * seq, V_DIM), F32),
            pltpu.VMEM((CONV_PAD + seq, LANES), F32),
        ],
    )
    return pl.pallas_call(
        functools.partial(_attn_body, seq=seq),
        grid_spec=grid_spec,
        out_shape=[jax.ShapeDtypeStruct((n, ATTN_WIDTH), BF16), jax.ShapeDtypeStruct((n, CONV_WIDTH), F32)],
        compiler_params=_cparams(("arbitrary", "arbitrary"), 32),
        name="diff_attn_conv",
    )(slopes, lq1, lk1, lq2, lk2, subln_g, h, h, h, h, h, conv_w, conv_b)


def _mix_body(x_ref, a_ref, c_ref, wo_ref, g0_ref, b0_ref, g1_ref, b1_ref, cg_ref, cb_ref, rw2_ref, rb_ref,
              x1_ref, xs_ref, pos_ref, gate_ref, cc_ref):
    tb = x_ref.shape[0]
    xn = _layer_norm(x_ref[...], g0_ref[...], b0_ref[...])
    conv = _layer_norm(c_ref[...], cg_ref[...], cb_ref[...])
    conv = (conv * _sigmoid(conv)).astype(BF16)
    mix = (jnp.dot(a_ref[...], wo_ref[0:ATTN_WIDTH, :], preferred_element_type=F32)
           + jnp.dot(conv, wo_ref[ATTN_WIDTH:, :], preferred_element_type=F32))
    x1 = _layer_norm(DEEPNORM_ALPHA * xn + mix, g1_ref[...], b1_ref[...])
    x1_ref[...] = x1
    x1b = x1.astype(BF16)

    x_lo = (x1 - x1b.astype(F32)).astype(BF16)
    rw2 = rw2_ref[...]
    nt_dims = (((1,), (1,)), ((), ()))
    both = lax.dot_general(rw2, x1b, nt_dims, preferred_element_type=F32)
    logits = ((lax.dot_general(rw2[:N_EXPERTS], x_lo, nt_dims, preferred_element_type=F32) + both[N_EXPERTS:])
              + both[:N_EXPERTS]) + rb_ref[...]
    eidx = lax.broadcasted_iota(I32, logits.shape, 0)
    work = logits
    vals, hots = [], []
    for _ in range(TOP_K):
        mx = jnp.max(work, axis=0, keepdims=True)
        sel = jnp.min(jnp.where(work == mx, eidx, N_EXPERTS), axis=0, keepdims=True)
        hot = eidx == sel
        vals.append(mx)
        hots.append(hot)
        work = jnp.where(hot, -jnp.inf, work)
    ex = [jnp.exp(v - vals[0]) for v in vals]
    den = ex[0] + ex[1] + ex[2] + ex[3]
    for k in range(TOP_K):
        gate_ref[k:k + 1, :] = ex[k] / den

    member = jnp.zeros(logits.shape, F32)
    for hot in hots:
        member = member + jnp.where(hot, 1.0, 0.0)
    before = (lax.broadcasted_iota(I32, (tb, tb), 0) < lax.broadcasted_iota(I32, (tb, tb), 1))
    rank = jnp.dot(member.astype(BF16), jnp.where(before, 1.0, 0.0).astype(BF16), preferred_element_type=F32)
    count = jnp.sum(member, axis=1, keepdims=True)
    chunks = jnp.floor((count + (BF16_ROWS - 1.0)) * (1.0 / BF16_ROWS))
    cc_ref[...] = jnp.broadcast_to(chunks, cc_ref.shape).astype(I32)
    lower = (lax.broadcasted_iota(I32, (N_EXPERTS, N_EXPERTS), 1) < lax.broadcasted_iota(I32, (N_EXPERTS, N_EXPERTS), 0))
    seg_start = jnp.dot(jnp.where(lower, 1.0, 0.0).astype(BF16),
                        jnp.broadcast_to(chunks, (N_EXPERTS, LANES)).astype(BF16),
                        preferred_element_type=F32)[:, 0:1]
    slot = seg_start * float(BF16_ROWS) + rank
    pos = [jnp.sum(jnp.where(hot, slot, 0.0), axis=0, keepdims=True).astype(I32) for hot in hots]
    for k in range(TOP_K):
        pos_ref[k:k + 1, :] = pos[k]

    rows = 256
    one, nil = jnp.ones((), BF16), jnp.zeros((), BF16)
    pos16 = [p.astype(jnp.int16) for p in pos]
    for rc in range(xs_ref.shape[0] // rows):
        r = lax.broadcasted_iota(jnp.int16, (rows, tb), 0) + jnp.int16(rc * rows)
        hit = (r == pos16[0]) | (r == pos16[1]) | (r == pos16[2]) | (r == pos16[3])
        sel = jnp.where(hit, one, nil)
        xs_ref[rc * rows:(rc + 1) * rows, :] = jnp.dot(sel, x1b, preferred_element_type=F32).astype(BF16)


def _mix(x2, a, c, wo_bf16, g0, b0, g1, b1, cg, cb, rw2, rb):
    n, d = x2.shape
    tb = TOKEN_TILE
    nb = n // tb
    row = lambda: pl.BlockSpec((1, d), lambda i: (0, 0))
    return pl.pallas_call(
        _mix_body,
        grid=(nb,),
        in_specs=[
            pl.BlockSpec((tb, d), lambda i: (i, 0)),
            pl.BlockSpec((tb, ATTN_WIDTH), lambda i: (i, 0)),
            pl.BlockSpec((tb, CONV_WIDTH), lambda i: (i, 0)),
            pl.BlockSpec((d, d), lambda i: (0, 0)),
            row(), row(), row(), row(),
            pl.BlockSpec((1, CONV_WIDTH), lambda i: (0, 0)),
            pl.BlockSpec((1, CONV_WIDTH), lambda i: (0, 0)),
            pl.BlockSpec((2 * N_EXPERTS, d), lambda i: (0, 0)),
            pl.BlockSpec((N_EXPERTS, 1), lambda i: (0, 0)),
        ],
        out_specs=[
            pl.BlockSpec((tb, d), lambda i: (i, 0)),
            pl.BlockSpec((None, LOCAL_ROWS, d), lambda i: (i, 0, 0)),
            pl.BlockSpec((None, TOP_K, tb), lambda i: (i, 0, 0)),
            pl.BlockSpec((None, TOP_K, tb), lambda i: (i, 0, 0)),
            pl.BlockSpec((None, N_EXPERTS, LANES), lambda i: (i, 0, 0)),
        ],
        out_shape=[
            jax.ShapeDtypeStruct((n, d), F32),
            jax.ShapeDtypeStruct((nb, LOCAL_ROWS, d), BF16),
            jax.ShapeDtypeStruct((nb, TOP_K, tb), I32),
            jax.ShapeDtypeStruct((nb, TOP_K, tb), F32),
            jax.ShapeDtypeStruct((nb, N_EXPERTS, LANES), I32),
        ],
        compiler_params=_cparams(("arbitrary",), 48),
        name="mix_router_sort",
    )(x2, a, c, wo_bf16, g0, b0, g1, b1, cg, cb, rw2, rb)


def _max_tiles(nb):
    return (nb * CHUNKS_PER_BLOCK + CHUNKS_PER_TILE - 1) // CHUNKS_PER_TILE + N_EXPERTS + 1


def _build_schedule(cc):
    nb = cc.shape[0]
    cpt = CHUNKS_PER_TILE
    ne1 = N_EXPERTS + 1
    tail = CHUNKS_PER_BLOCK - jnp.sum(cc, axis=1)
    cce = jnp.concatenate([cc, tail[:, None]], axis=1)
    seg_start = jnp.cumsum(cce, axis=1) - cce
    tot = jnp.sum(cce, axis=0)
    tiles_e = (tot + cpt - 1) // cpt
    tile_end = jnp.cumsum(tiles_e)
    max_tiles = _max_tiles(nb)

    def lookup(passed, table):
        return table[0] + jnp.sum(jnp.where(passed, (table[1:] - table[:-1])[None, :], 0), axis=1)

    t = jnp.arange(max_tiles, dtype=I32)
    passed = tile_end[None, :] <= t[:, None]
    te = jnp.minimum(jnp.sum(passed, axis=1), ne1 - 1).astype(I32)
    experts = jnp.arange(ne1, dtype=I32)
    has_tiles = (tiles_e > 0) & (experts < N_EXPERTS)
    later = jnp.where(has_tiles[None, :] & (experts[None, :] > te[:, None]), experts[None, :], ne1)
    nxt = jnp.min(later, axis=1)
    own = jnp.where(te < N_EXPERTS, te, jnp.max(jnp.where(has_tiles, experts, 0)))
    is_first = jnp.any(((tile_end - tiles_e)[None, :] == t[:, None]) & (tiles_e > 0)[None, :], axis=1)
    wsel = jnp.where(is_first | (nxt >= N_EXPERTS), own, nxt).astype(I32)

    pad_len = tiles_e * cpt - tot
    seg_len = jnp.concatenate([cce.T, pad_len[:, None]], axis=1).reshape(-1)
    seg_end = jnp.cumsum(seg_len)
    block = jnp.arange(nb, dtype=I32)[None, :]
    is_real = jnp.concatenate([jnp.ones((ne1, nb), I32), jnp.zeros((ne1, 1), I32)], axis=1).reshape(-1)
    seg_first = (seg_end - seg_len) * is_real
    zero = jnp.zeros((1,), I32)

    base = block * CHUNKS_PER_BLOCK + seg_start.T
    first = jnp.concatenate([jnp.concatenate([base, jnp.zeros((ne1, 1), I32)], axis=1).reshape(-1) - seg_first,
                             zero])
    p = jnp.arange(max_tiles * cpt, dtype=I32)
    passed = seg_end[None, :] <= p[:, None]
    real = lookup(passed, jnp.concatenate([is_real, zero]))
    src = (real * p + lookup(passed, first)).astype(I32)
    dump0 = nb * CHUNKS_PER_BLOCK
    dst = jnp.where(real > 0, src, dump0 + ((p // cpt) % 2) * cpt + p % cpt).astype(I32)
    dummy = dump0 + cpt + jnp.arange(cpt, dtype=I32)
    nv = jnp.sum(real.reshape(max_tiles, cpt), axis=1).astype(I32)
    return te, wsel, nv, src, jnp.concatenate([dummy, dst]), jnp.reshape(tile_end[-1], (1,)).astype(I32)


def _moe_body(te_ref, wsel_ref, nv_ref, src_ref, dst_ref, nt_ref, xs_hbm, wgu_ref, bgu_ref, wd_ref, bd_ref, y_hbm,
              xbuf, ybuf, act, wgu_b, wd_b, in_sem, out_sem):
    t = pl.program_id(0)
    last = nt_ref[0] - 1
    cpt = CHUNKS_PER_TILE
    f = wd_ref.shape[0]
    slot = t % 2
    other = 1 - slot
    e = te_ref[t]

    def rows(c):
        return pl.ds(c * BF16_ROWS, BF16_ROWS)

    def in_copy(tile, buf, c):
        return pltpu.make_async_copy(xs_hbm.at[src_ref[tile * cpt + c]], xbuf.at[buf, rows(c)], in_sem.at[buf])

    def out_copy(tile, buf, c):
        return pltpu.make_async_copy(ybuf.at[buf, rows(c)], y_hbm.at[dst_ref[(tile + 1) * cpt + c]],
                                     out_sem.at[buf])

    def start_all(copy, tile, buf):
        for c in range(cpt):
            copy(tile, buf, c).start(priority=c % 2)

    def wait_all(copy, tile, buf):
        for c in range(cpt):
            copy(tile, buf, c).wait()

    @pl.when(t == 0)
    def _():
        start_all(in_copy, 0, 0)
        ybuf[...] = jnp.zeros(ybuf.shape, BF16)
        even_dump = [pltpu.make_async_copy(ybuf.at[1, rows(c)], y_hbm.at[y_hbm.shape[0] - 2 * cpt + c],
                                           out_sem.at[0]) for c in range(cpt)]
        for copy in even_dump:
            copy.start()
        for copy in even_dump:
            copy.wait()

    @pl.when(t <= last)
    def _():
        @pl.when(t >= 1)
        def _():
            wait_all(out_copy, t - 2, slot)

        wait_all(in_copy, t, slot)
        start_all(in_copy, jnp.minimum(t + 1, last), other)
        start_all(out_copy, t - 1, other)

        @pl.when((e < N_EXPERTS) & ((t == 0) | (te_ref[jnp.maximum(t - 1, 0)] != e)))
        def _():
            wgu_b[...] = wgu_ref[...].astype(BF16)
            wd_b[...] = wd_ref[...].astype(BF16)

        def expert_mlp(m):
            x = xbuf[slot, 0:m, :]
            nc = 512
            for j in range(f // nc):
                gsl = slice(j * nc, (j + 1) * nc)
                usl = slice(f + j * nc, f + (j + 1) * nc)
                g = jnp.dot(x, wgu_b[:, gsl], preferred_element_type=F32) + bgu_ref[:, gsl]
                u = jnp.dot(x, wgu_b[:, usl], preferred_element_type=F32) + bgu_ref[:, usl]
                g = jnp.minimum(g, SWIGLU_LIMIT)
                u = jnp.clip(u, -SWIGLU_LIMIT, SWIGLU_LIMIT)
                act[0:m, gsl] = (g * _sigmoid(SWIGLU_ALPHA * g) * (u + 1.0)).astype(BF16)
            a = act[0:m, :]
            for j in range(wd_ref.shape[1] // nc):
                sl = slice(j * nc, (j + 1) * nc)
                y = jnp.dot(a, wd_b[:, sl], preferred_element_type=F32) + bd_ref[:, sl]
                ybuf[slot, 0:m, sl] = y.astype(BF16)

        quanta = (nv_ref[t] * BF16_ROWS + (MOE_ROW_QUANTUM - 1)) // MOE_ROW_QUANTUM
        for k in range(1, MOE_TILE // MOE_ROW_QUANTUM + 1):
            pl.when((e < N_EXPERTS) & (quanta == k))(functools.partial(expert_mlp, k * MOE_ROW_QUANTUM))

        @pl.when(e >= N_EXPERTS)
        def _():
            ybuf[slot] = jnp.zeros(ybuf.shape[1:], BF16)

        @pl.when(t == last)
        def _():
            start_all(out_copy, t, slot)
            wait_all(out_copy, t - 1, other)
            wait_all(out_copy, t, slot)
            wait_all(in_copy, t, other)


def _moe(xs_chunks, te, wsel, nv, src, dst, nt, w_gate_up, b_gate_up, w_down, b_down):
    nch, _, d = xs_chunks.shape
    f = w_down.shape[1]
    max_tiles = te.shape[0]
    by_wsel = lambda t, te_r, wsel_r, nv_r, src_r, dst_r, nt_r: (wsel_r[t], 0, 0)
    by_tile = lambda t, te_r, wsel_r, nv_r, src_r, dst_r, nt_r: (jnp.minimum(te_r[t], N_EXPERTS - 1), 0, 0)
    grid_spec = pltpu.PrefetchScalarGridSpec(
        num_scalar_prefetch=6,
        grid=(max_tiles,),
        in_specs=[
            pl.BlockSpec(memory_space=pl.ANY),
            pl.BlockSpec((None, d, 2 * f), by_wsel),
            pl.BlockSpec((None, 1, 2 * f), by_tile),
            pl.BlockSpec((None, f, d), by_wsel),
            pl.BlockSpec((None, 1, d), by_tile),
        ],
        out_specs=pl.BlockSpec(memory_space=pl.ANY),
        scratch_shapes=[
            pltpu.VMEM((2, MOE_TILE, d), BF16),
            pltpu.VMEM((2, MOE_TILE, d), BF16),
            pltpu.VMEM((MOE_TILE, f), BF16),
            pltpu.VMEM((d, 2 * f), BF16),
            pltpu.VMEM((f, d), BF16),
            pltpu.SemaphoreType.DMA((2,)),
            pltpu.SemaphoreType.DMA((2,)),
        ],
    )
    return pl.pallas_call(
        _moe_body,
        grid_spec=grid_spec,
        out_shape=jax.ShapeDtypeStruct((nch + 2 * CHUNKS_PER_TILE, BF16_ROWS, d), BF16),
        compiler_params=_cparams(("arbitrary",), 56),
        name="moe_grouped",
    )(te, wsel, nv, src, dst, nt, xs_chunks, w_gate_up, b_gate_up, w_down, b_down)


def _combine_body(y_ref, pos_ref, gate_ref, x1_ref, g_ref, b_ref, o_ref):
    tb = TOKEN_TILE
    cols = 512
    for blk in range(COMBINE_BLOCKS):
        pos = pos_ref[blk]
        gate = gate_ref[blk]
        y0 = blk * LOCAL_ROWS
        acc = jnp.zeros((tb, x1_ref.shape[1]), F32)
        for rc in range(LOCAL_ROWS // cols):
            r = lax.broadcasted_iota(I32, (tb, cols), 1) + rc * cols
            w = jnp.zeros((tb, cols), F32)
            for k in range(TOP_K):
                w = w + jnp.where(r == pos[:, k:k + 1], gate[:, k:k + 1], 0.0)
            acc = acc + jnp.dot(w.astype(BF16), y_ref[y0 + rc * cols:y0 + (rc + 1) * cols, :],
                                preferred_element_type=F32)
        rows = slice(blk * tb, (blk + 1) * tb)
        o_ref[rows, :] = _layer_norm(DEEPNORM_ALPHA * x1_ref[rows, :] + acc, g_ref[...], b_ref[...])


def _combine(y_local, pos, gates, x1, g, b):
    n, d = x1.shape
    tb = TOKEN_TILE
    nb = n // tb
    per = COMBINE_BLOCKS
    assert nb % per == 0
    row = lambda: pl.BlockSpec((1, d), lambda i: (0, 0))
    return pl.pallas_call(
        _combine_body,
        grid=(nb // per,),
        in_specs=[
            pl.BlockSpec((per * LOCAL_ROWS, d), lambda i: (i, 0)),
            pl.BlockSpec((per, tb, TOP_K), lambda i: (i, 0, 0)),
            pl.BlockSpec((per, tb, TOP_K), lambda i: (i, 0, 0)),
            pl.BlockSpec((per * tb, d), lambda i: (i, 0)),
            row(), row(),
        ],
        out_specs=pl.BlockSpec((per * tb, d), lambda i: (i, 0)),
        out_shape=jax.ShapeDtypeStruct((n, d), F32),
        compiler_params=_cparams(("arbitrary",), 48),
        name="combine_ln",
    )(y_local, pos, gates, x1, g, b)


def kernel(x, emb_ln_g, emb_ln_b, w_in, lambda_q1, lambda_k1, lambda_q2, lambda_k2, subln_g, conv_w, conv_b,
           conv_ln_g, conv_ln_b, w_out, ln1_g, ln1_b, router_w, router_b, w_gate_up, b_gate_up, w_down, b_down,
           ln2_g, ln2_b):
    batch, seq, d = x.shape
    n = batch * seq
    assert n % TOKEN_TILE == 0 and seq % ATTN_KV_BLOCK == 0 and seq % CONV_TILE == 0
    assert w_in.shape[0] == DEPTH
    x2 = x.reshape(n, d)
    row = lambda v: v.reshape(1, -1).astype(F32)
    slopes = jnp.array([(2.0 ** (-8.0 / N_HEADS)) ** (i + 1) for i in range(N_HEADS)], dtype=F32)

    h = _inproj(x2, row(emb_ln_g), row(emb_ln_b), w_in[0].astype(BF16))
    a, c = _attention_and_conv(h, slopes, row(lambda_q1[0]), row(lambda_k1[0]), row(lambda_q2[0]),
                               row(lambda_k2[0]), row(subln_g[0]), conv_w[0], row(conv_b[0]), batch=batch, seq=seq)
    rw_t = router_w[0].T
    rw_hi = rw_t.astype(BF16)
    rw_lo = (rw_t - rw_hi.astype(F32)).astype(BF16)
    x1, xs_local, pos_t, gates_t, cc = _mix(
        x2, a, c, w_out[0].astype(BF16), row(emb_ln_g), row(emb_ln_b), row(ln1_g[0]), row(ln1_b[0]),
        row(conv_ln_g[0]), row(conv_ln_b[0]),
        jnp.concatenate([rw_hi, rw_lo], axis=0), router_b[0].reshape(N_EXPERTS, 1))

    te, wsel, nv, src, dst, nt = _build_schedule(cc[:, :, 0])
    nb = n // TOKEN_TILE
    y_chunks = _moe(xs_local.reshape(nb * CHUNKS_PER_BLOCK, BF16_ROWS, d), te, wsel, nv, src, dst, nt,
                    w_gate_up[0], b_gate_up[0].reshape(N_EXPERTS, 1, -1), w_down[0],
                    b_down[0].reshape(N_EXPERTS, 1, -1))
    y_local = y_chunks.reshape(-1, d)
    out = _combine(y_local, jnp.transpose(pos_t, (0, 2, 1)), jnp.transpose(gates_t, (0, 2, 1)), x1,
                   row(ln2_g[0]), row(ln2_b[0]))
    return out.reshape(batch, seq, d)
```
